```python
import math
import jax, jax.numpy as jnp
from jax import lax
import numpy as np

D_MODEL = 4096
BATCH = 1
SEQ = 8192
DEPTH = 2

N_EVEN = (DEPTH + 1) // 2
N_ODD = DEPTH // 2
MIX_WIDTH = D_MODEL
GROUP_WIDTH = MIX_WIDTH // 2

A_HEAD_DIM = 128
A_HEADS = GROUP_WIDTH // A_HEAD_DIM
A_KV_HEADS = A_HEADS // 4
A_ROT = A_HEAD_DIM // 4
IDX_HEADS = 32
IDX_DIM = 64
IDX_ROT = IDX_DIM // 4
TOPK_MAX = 256
Q_BLOCK = 128
ROPE_THETA = 500000.0

RWKV_HEAD = 64
RWKV_DIM = GROUP_WIDTH
RWKV_HEADS = RWKV_DIM // RWKV_HEAD
DECAY_RANK = max(32, int(round(1.8 * RWKV_DIM ** 0.5 / 32)) * 32)
AAA_RANK = max(32, int(round(1.8 * RWKV_DIM ** 0.5 / 32)) * 32)
GATE_RANK = max(32, int(round(0.6 * RWKV_DIM ** 0.8 / 32)) * 32)
RWKV_GN_EPS = 1e-5 * RWKV_HEAD

S5_DIM = GROUP_WIDTH
S5_GROUP = 16
S5_GROUPS = S5_DIM // S5_GROUP
S5_STATE = 64
S5_STEP_MIN = 0.001
S5_STEP_MAX = 0.1

RET_HEAD_DIM = 256
RET_HEADS = GROUP_WIDTH // RET_HEAD_DIM
RET_CHUNK = 128
RET_ROPE_BASE = 10000.0
RET_GN_EPS = 1e-5

D_FF = ((8 * D_MODEL + 3 * 256 - 1) // (3 * 256)) * 256
NORM_EPS = 1e-6

A_SPLITS = (A_HEADS * A_HEAD_DIM, A_KV_HEADS * A_HEAD_DIM, A_KV_HEADS * A_HEAD_DIM,
            IDX_HEADS * IDX_DIM, IDX_DIM, IDX_HEADS)
RWKV_SPLITS = (RWKV_DIM, RWKV_DIM, RWKV_DIM, DECAY_RANK, AAA_RANK, GATE_RANK)
RWKV_COLS = sum(RWKV_SPLITS)
EVEN_SPLITS = A_SPLITS + (RWKV_COLS,)
EVEN_IN = sum(EVEN_SPLITS)
ODD_SPLITS = (S5_DIM, RET_HEADS * RET_HEAD_DIM, RET_HEADS * RET_HEAD_DIM,
              RET_HEADS * RET_HEAD_DIM, RET_HEADS * RET_HEAD_DIM)
ODD_IN = sum(ODD_SPLITS)

kernel_name = "hybrid_dsa_rwkv7_s5_retention_block"


def _rms_norm(x, g):
    x32 = x.astype(jnp.float32)
    y = x32 * lax.rsqrt(jnp.mean(x32 * x32, axis=-1, keepdims=True) + NORM_EPS)
    return (y * g.astype(jnp.float32)).astype(x.dtype)


def _head_norm(y, eps):
    mean = jnp.mean(y, axis=-1, keepdims=True)
    var = jnp.mean(jnp.square(y - mean), axis=-1, keepdims=True)
    return (y - mean) * lax.rsqrt(var + eps)


def _split_cols(z, sizes):
    out, off = [], 0
    for s in sizes:
        out.append(z[..., off:off + s])
        off += s
    return out


def _partial_inv_freq(rot_dim):
    return ROPE_THETA ** (-jnp.arange(0, rot_dim, 2, dtype=jnp.float32) / rot_dim)


def _rotate(x, inv_freq):
    n_half = inv_freq.shape[0]
    L = x.shape[1]
    ang = jnp.arange(L, dtype=jnp.float32)[:, None] * inv_freq[None, :]
    cos = jnp.cos(ang)[None, :, None, :]
    sin = jnp.sin(ang)[None, :, None, :]
    x32 = x.astype(jnp.float32)
    x1 = x32[..., :n_half]
    x2 = x32[..., n_half:2 * n_half]
    out = jnp.concatenate([x1 * cos - x2 * sin, x1 * sin + x2 * cos, x32[..., 2 * n_half:]], axis=-1)
    return out.astype(x.dtype)


def _token_shift(z):
    return jnp.pad(z[:, :-1], ((0, 0), (1, 0), (0, 0)))


def _swiglu(h, w_gate, w_up, w_down):
    return (jax.nn.silu(h @ w_gate) * (h @ w_up)) @ w_down


def _dsa_attention(q, k, v, qi, ki, wi):
    B, L = q.shape[:2]
    n_top = min(TOPK_MAX, L // 4)
    nb = L // Q_BLOCK
    grp = A_HEADS // A_KV_HEADS
    key_pos = jnp.arange(L)
    ki32 = ki.astype(jnp.float32)

    def to_blocks(t):
        return t.reshape((B, nb, Q_BLOCK) + t.shape[2:]).swapaxes(0, 1)

    def one_block(args):
        bi, qb, qib, wib = args
        q_pos = bi * Q_BLOCK + jnp.arange(Q_BLOCK)
        logits = jnp.einsum('bqhd,bsd->bqhs', qib.astype(jnp.float32), ki32) * (IDX_DIM ** -0.5)
        score = jnp.einsum('bqhs,bqh->bqs', jax.nn.relu(logits),
                           wib.astype(jnp.float32) * (IDX_HEADS ** -0.5))
        causal = key_pos[None, :] <= q_pos[:, None]
        score = jnp.where(causal[None], score, -jnp.inf)
        _, idx = lax.top_k(score, n_top)
        valid = idx <= q_pos[None, :, None]
        k_sel = jax.vmap(lambda kk, ii: kk[ii])(k, idx)
        v_sel = jax.vmap(lambda vv, ii: vv[ii])(v, idx)
        qg = qb.reshape(B, Q_BLOCK, A_KV_HEADS, grp, A_HEAD_DIM).astype(jnp.float32)
        s = jnp.einsum('bqhgd,bqkhd->bqhgk', qg, k_sel.astype(jnp.float32)) * (A_HEAD_DIM ** -0.5)
        s = jnp.where(valid[:, :, None, None, :], s, -jnp.inf)
        p = jax.nn.softmax(s, axis=-1)
        o = jnp.einsum('bqhgk,bqkhd->bqhgd', p, v_sel.astype(jnp.float32))
        return o.reshape(B, Q_BLOCK, A_HEADS * A_HEAD_DIM)

    out = lax.map(one_block, (jnp.arange(nb), to_blocks(q), to_blocks(qi), to_blocks(wi)))
    return out.swapaxes(0, 1).reshape(B, L, A_HEADS * A_HEAD_DIM)


def _rwkv7_step(state, inp):
    r, w, k, v, kk, a = inp
    sa = jnp.einsum('bhij,bhj->bhi', state, -kk)
    state = (state * w[:, :, None, :] + sa[..., None] * (kk * a)[:, :, None, :]
             + v[..., None] * k[:, :, None, :])
    return state, jnp.einsum('bhij,bhj->bhi', state, r)


def _rwkv7_mix(z, mu, w0, w_up, a0, a_up, g_up, k_k, k_a, r_k, ln_w, ln_b):
    B, L, _ = z.shape
    z = z.astype(jnp.float32)
    z = z + (_token_shift(z) - z) * mu
    r, k, v, wd, ad, gd = _split_cols(z, RWKV_SPLITS)
    w_log = -jax.nn.softplus(-(w0 + jnp.tanh(wd) @ w_up)) - 0.5
    decay = jnp.exp(-jnp.exp(w_log))
    a = jax.nn.sigmoid(a0 + ad @ a_up)
    g = jax.nn.sigmoid(gd) @ g_up

    def heads(t):
        return t.reshape(B, L, RWKV_HEADS, RWKV_HEAD)

    kk = heads(k * k_k)
    kk = kk * lax.rsqrt(jnp.maximum(jnp.sum(kk * kk, axis=-1, keepdims=True), 1e-24))
    k = k * (1.0 + (a - 1.0) * k_a)
    r_h, k_h, v_h, w_h, a_h = heads(r), heads(k), heads(v), heads(decay), heads(a)
    seq = tuple(t.swapaxes(0, 1) for t in (r_h, w_h, k_h, v_h, kk, a_h))
    s0 = jnp.zeros((B, RWKV_HEADS, RWKV_HEAD, RWKV_HEAD), jnp.float32)
    _, y = lax.scan(_rwkv7_step, s0, seq)
    y = y.swapaxes(0, 1)
    y = (_head_norm(y, RWKV_GN_EPS) * ln_w.reshape(RWKV_HEADS, RWKV_HEAD)
         + ln_b.reshape(RWKV_HEADS, RWKV_HEAD))
    y = y + jnp.sum(r_h * k_h * r_k, axis=-1, keepdims=True) * v_h
    return y.reshape(B, L, RWKV_DIM) * g


def _complex_combine(e1, e2):
    a1r, a1i, b1r, b1i = e1
    a2r, a2i, b2r, b2i = e2
    return (a2r * a1r - a2i * a1i,
            a2r * a1i + a2i * a1r,
            a2r * b1r - a2i * b1i + b2r,
            a2r * b1i + a2i * b1r + b2i)


def _s5_mix(u, lam_re, lam_im, log_step, b_re, b_im, c_re, c_im, d_skip, w_glu, b_glu):
    B, L, _ = u.shape
    u32 = u.astype(jnp.float32)
    ug = u32.reshape(B, L, S5_GROUPS, S5_GROUP)
    lr = jnp.minimum(lam_re.astype(jnp.float32), -1e-4)
    li = lam_im.astype(jnp.float32)
    step = jnp.exp(log_step.astype(jnp.float32))[:, None]
    mag = jnp.exp(lr * step)
    abar_r = mag * jnp.cos(li * step)
    abar_i = mag * jnp.sin(li * step)
    den = lr * lr + li * li
    cr = (lr * (abar_r - 1.0) + li * abar_i) / den
    ci = (lr * abar_i - li * (abar_r - 1.0)) / den
    bbar_r = cr[..., None] * b_re - ci[..., None] * b_im
    bbar_i = cr[..., None] * b_im + ci[..., None] * b_re
    bu_r = jnp.einsum('blgi,gpi->blgp', ug, bbar_r)
    bu_i = jnp.einsum('blgi,gpi->blgp', ug, bbar_i)
    ar = jnp.broadcast_to(abar_r, bu_r.shape)
    ai = jnp.broadcast_to(abar_i, bu_i.shape)
    _, _, xr, xi = lax.associative_scan(_complex_combine, (ar, ai, bu_r, bu_i), axis=1)
    y = jnp.einsum('blgp,gop->blgo', xr, c_re) - jnp.einsum('blgp,gop->blgo', xi, c_im)
    y = y.reshape(B, L, S5_DIM) + d_skip * u32
    zg = jax.nn.gelu(y)
    return zg * jax.nn.sigmoid(zg @ w_glu + b_glu)


def _retention_mix(q, k, v, gate):
    B, L, _ = q.shape
    nc = L // RET_CHUNK

    def heads(t):
        return t.astype(jnp.float32).reshape(B, L, RET_HEADS, RET_HEAD_DIM)

    inv = 1.0 / (RET_ROPE_BASE ** jnp.linspace(0.0, 1.0, RET_HEAD_DIM // 2, dtype=jnp.float32))
    q = _rotate(heads(q), inv)
    k = _rotate(heads(k), inv) * (RET_HEAD_DIM ** -0.5)
    v = heads(v)
    log_g = jnp.log(1.0 - 2.0 ** (-5.0 - jnp.arange(RET_HEADS, dtype=jnp.float32)))
    pos = jnp.arange(RET_CHUNK, dtype=jnp.float32)
    diff = pos[:, None] - pos[None, :]
    intra = jnp.where(diff >= 0, jnp.exp(jnp.maximum(diff, 0.0)[None] * log_g[:, None, None]), 0.0)
    xi = jnp.exp((pos + 1.0)[None, :] * log_g[:, None])
    zeta = jnp.exp((RET_CHUNK - 1.0 - pos)[None, :] * log_g[:, None])
    g_chunk = jnp.exp(RET_CHUNK * log_g)

    def chunks(t):
        return t.reshape(B, nc, RET_CHUNK, RET_HEADS, RET_HEAD_DIM).transpose(1, 0, 3, 2, 4)

    def step(state, inp):
        qc, kc, vc = inp
        att = jnp.einsum('bhqd,bhkd->bhqk', qc, kc) * intra
        o = (jnp.einsum('bhqk,bhkv->bhqv', att, vc)
             + jnp.einsum('bhqd,bhdv->bhqv', qc, state) * xi[None, :, :, None])
        state = (state * g_chunk[None, :, None, None]
                 + jnp.einsum('bhkd,bhkv->bhdv', kc * zeta[None, :, :, None], vc))
        return state, o

    s0 = jnp.zeros((B, RET_HEADS, RET_HEAD_DIM, RET_HEAD_DIM), jnp.float32)
    _, o = lax.scan(step, s0, (chunks(q), chunks(k), chunks(v)))
    o = o.transpose(1, 0, 3, 2, 4).reshape(B, L, RET_HEADS, RET_HEAD_DIM)
    o = _head_norm(o, RET_GN_EPS).reshape(B, L, RET_HEADS * RET_HEAD_DIM)
    return jax.nn.silu(gate.astype(jnp.float32)) * o


def _even_mixer(h, w_in, w_out, mu, w0, w_up, a0, a_up, g_up, k_k, k_a, r_k, ln_w, ln_b):
    B, L, _ = h.shape
    z = h @ w_in
    q, k, v, qi, ki, wi, zr = _split_cols(z, EVEN_SPLITS)
    inv_a = _partial_inv_freq(A_ROT)
    inv_i = _partial_inv_freq(IDX_ROT)
    q = _rotate(q.reshape(B, L, A_HEADS, A_HEAD_DIM), inv_a)
    k = _rotate(k.reshape(B, L, A_KV_HEADS, A_HEAD_DIM), inv_a)
    v = v.reshape(B, L, A_KV_HEADS, A_HEAD_DIM)
    qi = _rotate(qi.reshape(B, L, IDX_HEADS, IDX_DIM), inv_i)
    ki = _rotate(ki.reshape(B, L, 1, IDX_DIM), inv_i)[:, :, 0]
    o_a = _dsa_attention(q, k, v, qi, ki, wi)
    o_b = _rwkv7_mix(zr, mu, w0, w_up, a0, a_up, g_up, k_k, k_a, r_k, ln_w, ln_b)
    return jnp.concatenate([o_a, o_b], axis=-1).astype(h.dtype) @ w_out


def _odd_mixer(h, w_in, w_out, lam_re, lam_im, log_step, b_re, b_im, c_re, c_im,
               d_skip, w_glu, b_glu):
    z = h @ w_in
    u, rq, rk, rv, rg = _split_cols(z, ODD_SPLITS)
    o_c = _s5_mix(u, lam_re, lam_im, log_step, b_re, b_im, c_re, c_im, d_skip, w_glu, b_glu)
    o_d = _retention_mix(rq, rk, rv, rg)
    return jnp.concatenate([o_c, o_d], axis=-1).astype(h.dtype) @ w_out


def setup_inputs(seed: int = 0) -> dict:
    key = jax.random.key(seed)
    ks = iter(jax.random.split(key, 40))
    f32 = jnp.float32

    def nrm(shape, scale):
        return jax.random.normal(next(ks), shape, f32) * scale

    def uni(shape, lo, hi):
        return jax.random.uniform(next(ks), shape, f32, lo, hi)

    return {
        "x": nrm((BATCH, SEQ, D_MODEL), 1.0),
        "norm_mix": 1.0 + nrm((DEPTH, D_MODEL), 0.01),
        "norm_ffn": 1.0 + nrm((DEPTH, D_MODEL), 0.01),
        "ffn_gate": nrm((DEPTH, D_MODEL, D_FF), D_MODEL ** -0.5),
        "ffn_up": nrm((DEPTH, D_MODEL, D_FF), D_MODEL ** -0.5),
        "ffn_down": nrm((DEPTH, D_FF, D_MODEL), D_FF ** -0.5),
        "e_w_in": nrm((N_EVEN, D_MODEL, EVEN_IN), D_MODEL ** -0.5),
        "e_w_out": nrm((N_EVEN, MIX_WIDTH, D_MODEL), MIX_WIDTH ** -0.5),
        "e_mu": uni((N_EVEN, RWKV_COLS), 0.0, 1.0),
        "e_w0": uni((N_EVEN, RWKV_DIM), -6.5, -1.5),
        "e_w_up": nrm((N_EVEN, DECAY_RANK, RWKV_DIM), 0.5 * DECAY_RANK ** -0.5),
        "e_a0": nrm((N_EVEN, RWKV_DIM), 0.1),
        "e_a_up": nrm((N_EVEN, AAA_RANK, RWKV_DIM), 0.5 * AAA_RANK ** -0.5),
        "e_g_up": nrm((N_EVEN, GATE_RANK, RWKV_DIM), GATE_RANK ** -0.5),
        "e_k_k": 0.85 + nrm((N_EVEN, RWKV_DIM), 0.05),
        "e_k_a": 1.0 + nrm((N_EVEN, RWKV_DIM), 0.05),
        "e_r_k": nrm((N_EVEN, RWKV_HEADS, RWKV_HEAD), 0.1),
        "e_ln_w": 1.0 + nrm((N_EVEN, RWKV_DIM), 0.01),
        "e_ln_b": nrm((N_EVEN, RWKV_DIM), 0.01),
        "o_w_in": nrm((N_ODD, D_MODEL, ODD_IN), D_MODEL ** -0.5),
        "o_w_out": nrm((N_ODD, MIX_WIDTH, D_MODEL), MIX_WIDTH ** -0.5),
        "o_lam_re": -0.5 + nrm((N_ODD, S5_GROUPS, S5_STATE), 0.01),
        "o_lam_im": math.pi * jnp.arange(S5_STATE, dtype=f32) + nrm((N_ODD, S5_GROUPS, S5_STATE), 0.01),
        "o_log_step": uni((N_ODD, S5_GROUPS), math.log(S5_STEP_MIN), math.log(S5_STEP_MAX)),
        "o_b_re": nrm((N_ODD, S5_GROUPS, S5_STATE, S5_GROUP), S5_GROUP ** -0.5),
        "o_b_im": nrm((N_ODD, S5_GROUPS, S5_STATE, S5_GROUP), S5_GROUP ** -0.5),
        "o_c_re": nrm((N_ODD, S5_GROUPS, S5_GROUP, S5_STATE), 2.0 * S5_STATE ** -0.5),
        "o_c_im": nrm((N_ODD, S5_GROUPS, S5_GROUP, S5_STATE), 2.0 * S5_STATE ** -0.5),
        "o_d_skip": nrm((N_ODD, S5_DIM), 0.5),
        "o_w_glu": nrm((N_ODD, S5_DIM, S5_DIM), S5_DIM ** -0.5),
        "o_b_glu": nrm((N_ODD, S5_DIM), 0.01),
        "final_norm": 1.0 + nrm((D_MODEL,), 0.01),
    }


def reference(x, norm_mix, norm_ffn, ffn_gate, ffn_up, ffn_down,
              e_w_in, e_w_out, e_mu, e_w0, e_w_up, e_a0, e_a_up, e_g_up, e_k_k, e_k_a,
              e_r_k, e_ln_w, e_ln_b,
              o_w_in, o_w_out, o_lam_re, o_lam_im, o_log_step, o_b_re, o_b_im, o_c_re,
              o_c_im, o_d_skip, o_w_glu, o_b_glu,
              final_norm):
    h = x
    for layer in range(DEPTH):
        i = layer // 2
        hn = _rms_norm(h, norm_mix[layer])
        if layer % 2 == 0:
            mix = _even_mixer(hn, e_w_in[i], e_w_out[i], e_mu[i], e_w0[i], e_w_up[i], e_a0[i],
                              e_a_up[i], e_g_up[i], e_k_k[i], e_k_a[i], e_r_k[i], e_ln_w[i],
                              e_ln_b[i])
        else:
            mix = _odd_mixer(hn, o_w_in[i], o_w_out[i], o_lam_re[i], o_lam_im[i], o_log_step[i],
                             o_b_re[i], o_b_im[i], o_c_re[i], o_c_im[i], o_d_skip[i],
                             o_w_glu[i], o_b_glu[i])
        h = h + mix
        h = h + _swiglu(_rms_norm(h, norm_ffn[layer]), ffn_gate[layer], ffn_up[layer], ffn_down[layer])
    return _rms_norm(h, final_norm)
```

```python
import functools
import math

import jax
import jax.numpy as jnp
from jax import lax
from jax.experimental import pallas as pl
from jax.experimental.pallas import tpu as pltpu

NORM_EPS = 1e-6
A_HEAD_DIM = 128
A_HEADS = 16
A_KV_HEADS = 4
A_GROUP = A_HEADS // A_KV_HEADS
A_ROT_HALF = 16
IDX_HEADS = 32
IDX_DIM = 64
IDX_ROT_HALF = 8
TOPK_MAX = 256
ROPE_THETA = 500000.0
RWKV_HEAD = 64
RWKV_DIM = 2048
RWKV_HEADS = 32
DECAY_RANK = 96
AAA_RANK = 96
GATE_RANK = 256
RWKV_GN_EPS = 1e-5 * RWKV_HEAD
S5_DIM = 2048
S5_GROUP = 16
S5_GROUPS = 128
S5_STATE = 64
RET_HEAD_DIM = 256
RET_HEADS = 8
RET_ROPE_BASE = 10000.0
RET_GN_EPS = 1e-5

LANES = 128
SUBLANES = 8
VMEM_LIMIT_BYTES = 56 * 2**20

MXU_DTYPE = jnp.bfloat16
NEG_BIG = -1e30
INT_MIN = -2**31

ZE_Q, ZE_K, ZE_V, ZE_QI = 0, 2048, 2560, 3072
ZE_R, ZE_RK, ZE_RV = 5120, 7168, 9216
ZE_KI2, ZE_WI, ZE_WD, ZE_AD, ZE_GD = 11264, 11392, 11776, 11904, 12032
ZE_COLS = 12288


def _cparams(sem):
    return pltpu.CompilerParams(dimension_semantics=sem, vmem_limit_bytes=VMEM_LIMIT_BYTES)


def _bf(x):
    return x.astype(MXU_DTYPE)


def _dot(a, b):
    return jnp.dot(a, b, preferred_element_type=jnp.float32)


def _dot_nt(a, b):
    return lax.dot_general(a, b, (((1,), (1,)), ((), ())), preferred_element_type=jnp.float32)


def _split3(x):
    h1 = _bf(x)
    r1 = x - h1.astype(jnp.float32)
    h2 = _bf(r1)
    h3 = _bf(r1 - h2.astype(jnp.float32))
    return h1, h2, h3


def _dot_x3(a, b):
    ah = _bf(a)
    al = _bf(a - ah.astype(jnp.float32))
    bh = _bf(b)
    bl = _bf(b - bh.astype(jnp.float32))
    return _dot(ah, bh) + _dot(ah, bl) + _dot(al, bh)


def _dot_exact_rhs(a, b_bf):
    h1, h2, h3 = _split3(a)
    return _dot(h1, b_bf) + _dot(h2, b_bf) + _dot(h3, b_bf)


def _sigmoid(x):
    return 1.0 / (1.0 + jnp.exp(-x))


def _rmsnorm_kernel(x_ref, g_ref, o_ref):
    x = x_ref[...]
    ms = jnp.mean(x * x, axis=-1, keepdims=True)
    o_ref[...] = (x * lax.rsqrt(ms + NORM_EPS) * g_ref[...]).astype(o_ref.dtype)


def _rmsnorm(x, g, out_dtype, tm=256):
    m, d = x.shape
    return pl.pallas_call(
        _rmsnorm_kernel,
        out_shape=jax.ShapeDtypeStruct((m, d), out_dtype),
        grid=(m // tm,),
        in_specs=[pl.BlockSpec((tm, d), lambda i: (i, 0)), pl.BlockSpec((1, d), lambda i: (0, 0))],
        out_specs=pl.BlockSpec((tm, d), lambda i: (i, 0)),
        compiler_params=_cparams(("parallel",)),
        name="rmsnorm",
    )(x, g.reshape(1, d))


def _mm_kernel(*refs, n_a, n_w, n_e, pairs, epilogue, nk):
    a_refs = refs[:n_a]
    w_refs = refs[n_a:n_a + n_w]
    e_refs = refs[n_a + n_w:n_a + n_w + n_e]
    o_ref = refs[n_a + n_w + n_e]
    acc_refs = refs[n_a + n_w + n_e + 1:]
    if nk == 1:
        accs = [_dot(a_refs[i][...], w_refs[j][...]) for i, j in pairs]
        o_ref[...] = epilogue(accs, [e[...] for e in e_refs]).astype(o_ref.dtype)
        return
    k = pl.program_id(2)

    @pl.when(k == 0)
    def _():
        for acc in acc_refs:
            acc[...] = jnp.zeros_like(acc)

    for acc, (i, j) in zip(acc_refs, pairs):
        acc[...] += _dot(a_refs[i][...], w_refs[j][...])

    @pl.when(k == nk - 1)
    def _():
        o_ref[...] = epilogue([acc[...] for acc in acc_refs], [e[...] for e in e_refs]).astype(o_ref.dtype)


def _matmul(a_list, w_list, pairs, extras, epilogue, out_dtype, tm, tn, tk=None, name="matmul"):
    m, kdim = a_list[0].shape
    n = w_list[0].shape[1]
    tk = kdim if tk is None else tk
    tm = min(tm, m)
    nk = kdim // tk
    assert m % tm == 0 and n % tn == 0 and kdim % tk == 0
    in_specs = [pl.BlockSpec((tm, tk), lambda i, j, k: (i, k)) for _ in a_list]
    in_specs += [pl.BlockSpec((tk, tn), lambda i, j, k: (k, j)) for _ in w_list]
    e_arrays = []
    for kind, arr in extras:
        e_arrays.append(arr)
        if kind == "tile":
            in_specs.append(pl.BlockSpec((tm, tn), lambda i, j, k: (i, j)))
        else:
            in_specs.append(pl.BlockSpec((1, tn), lambda i, j, k: (0, j)))
    scratch = [] if nk == 1 else [pltpu.VMEM((tm, tn), jnp.float32) for _ in pairs]
    kern = functools.partial(_mm_kernel, n_a=len(a_list), n_w=len(w_list), n_e=len(extras),
                             pairs=tuple(pairs), epilogue=epilogue, nk=nk)
    return pl.pallas_call(
        kern,
        out_shape=jax.ShapeDtypeStruct((m, n), out_dtype),
        grid=(m // tm, n // tn, nk),
        in_specs=in_specs,
        out_specs=pl.BlockSpec((tm, tn), lambda i, j, k: (i, j)),
        scratch_shapes=scratch,
        compiler_params=_cparams(("parallel", "parallel", "arbitrary")),
        name=name,
    )(*a_list, *w_list, *e_arrays)


def _ep_plain(accs, ex):
    return accs[0]


def _ep_swiglu(accs, ex):
    g = accs[0]
    return g * _sigmoid(g) * accs[1]


def _ep_resid(accs, ex):
    return ex[0] + accs[0]


def _ep_resid2(accs, ex):
    return ex[0] + (accs[0] + accs[1])


def _ep_glu(accs, ex):
    zg, b = ex
    return zg * _sigmoid(accs[0] + b)


def _rope_tile(x, tc, s1, s2, n_half):
    return x * tc + pltpu.roll(x, n_half, 1) * s2 + pltpu.roll(x, LANES - n_half, 1) * s1


def _dsa_prep_kernel(zq_ref, zs_ref, ta_ref, ti_ref, qt_ref, k_ref, vt_ref, qit_ref, ki_ref, wt_ref):
    ta_c, ta_1, ta_2 = ta_ref[0], ta_ref[1], ta_ref[2]
    ti_c, ti_1, ti_2 = ti_ref[0], ti_ref[1], ti_ref[2]
    q_scale = A_HEAD_DIM ** -0.5
    for h in range(A_HEADS):
        x = zq_ref[:, ZE_Q + h * LANES:ZE_Q + (h + 1) * LANES]
        y = _rope_tile(x, ta_c, ta_1, ta_2, A_ROT_HALF) * q_scale
        qt_ref[h * LANES:(h + 1) * LANES, :] = _bf(y.T)
    for g in range(A_KV_HEADS):
        x = zq_ref[:, ZE_K + g * LANES:ZE_K + (g + 1) * LANES]
        k_ref[:, g * LANES:(g + 1) * LANES] = _bf(_rope_tile(x, ta_c, ta_1, ta_2, A_ROT_HALF))
        v = zq_ref[:, ZE_V + g * LANES:ZE_V + (g + 1) * LANES]
        vt_ref[g * LANES:(g + 1) * LANES, :] = _bf(v.T)
    for c in range(IDX_HEADS * IDX_DIM // LANES):
        x = zq_ref[:, ZE_QI + c * LANES:ZE_QI + (c + 1) * LANES]
        y = _rope_tile(x, ti_c, ti_1, ti_2, IDX_ROT_HALF)
        qit_ref[c * LANES:(c + 1) * LANES, :] = _bf(y.T)
    ki = _rope_tile(zs_ref[:, 0:LANES], ti_c, ti_1, ti_2, IDX_ROT_HALF)
    ki_ref[...] = _bf(ki)
    w_scale = (IDX_DIM ** -0.5) * (IDX_HEADS ** -0.5)
    wt_ref[...] = (zs_ref[:, LANES:2 * LANES] * w_scale).T


def _dsa_prep(z, tab_a, tab_i, tm=256):
    L = z.shape[0]
    n_q = A_HEADS * A_HEAD_DIM
    n_kv = A_KV_HEADS * A_HEAD_DIM
    n_qi = IDX_HEADS * IDX_DIM
    out_shape = (
        jax.ShapeDtypeStruct((n_q, L), MXU_DTYPE),
        jax.ShapeDtypeStruct((L, n_kv), MXU_DTYPE),
        jax.ShapeDtypeStruct((n_kv, L), MXU_DTYPE),
        jax.ShapeDtypeStruct((n_qi, L), MXU_DTYPE),
        jax.ShapeDtypeStruct((L, LANES), MXU_DTYPE),
        jax.ShapeDtypeStruct((LANES, L), jnp.float32),
    )
    return pl.pallas_call(
        _dsa_prep_kernel,
        out_shape=out_shape,
        grid=(L // tm,),
        in_specs=[
            pl.BlockSpec((tm, ZE_R), lambda i: (i, 0)),
            pl.BlockSpec((tm, 2 * LANES), lambda i: (i, ZE_KI2 // (2 * LANES))),
            pl.BlockSpec((3, tm, LANES), lambda i: (0, i, 0)),
            pl.BlockSpec((3, tm, LANES), lambda i: (0, i, 0)),
        ],
        out_specs=(
            pl.BlockSpec((n_q, tm), lambda i: (0, i)),
            pl.BlockSpec((tm, n_kv), lambda i: (i, 0)),
            pl.BlockSpec((n_kv, tm), lambda i: (0, i)),
            pl.BlockSpec((n_qi, tm), lambda i: (0, i)),
            pl.BlockSpec((tm, LANES), lambda i: (i, 0)),
            pl.BlockSpec((LANES, tm), lambda i: (0, i)),
        ),
        compiler_params=_cparams(("parallel",)),
        name="dsa_prep",
    )(z, z, tab_a, tab_i)


DSA_TQ = 128
DSA_KC = 512
DSA_KA = 256


def _dsa_kernel(qt_ref, qit_ref, wt_ref, ki_ref, k_ref, vt_ref, o_ref,
                qi_s, qg_s, sc_s, key_s, m_s, l_s, acc_s, *, n_top):
    qb = pl.program_id(0)
    q0 = qb * DSA_TQ
    n_chunks = (q0 + DSA_TQ + DSA_KC - 1) // DSA_KC
    q_pos = q0 + lax.broadcasted_iota(jnp.int32, (1, DSA_TQ), 1)

    for h in range(IDX_HEADS):
        qi_s[:, h * DSA_TQ:(h + 1) * DSA_TQ] = qit_ref[h * IDX_DIM:(h + 1) * IDX_DIM, :]
    for g in range(A_KV_HEADS):
        for j in range(A_GROUP):
            h = g * A_GROUP + j
            qg_s[g, :, j * DSA_TQ:(j + 1) * DSA_TQ] = qt_ref[h * A_HEAD_DIM:(h + 1) * A_HEAD_DIM, :]

    def score_chunk(c, carry):
        for sub in range(DSA_KC // DSA_KA):
            r0 = pl.multiple_of(c * DSA_KC + sub * DSA_KA, DSA_KA)
            kic = ki_ref[pl.ds(r0, DSA_KA), 0:IDX_DIM]
            logits = _dot(kic, qi_s[...])
            acc = jnp.zeros((DSA_KA, DSA_TQ), jnp.float32)
            for h in range(IDX_HEADS):
                acc = acc + jnp.maximum(logits[:, h * DSA_TQ:(h + 1) * DSA_TQ], 0.0) * wt_ref[h:h + 1, :]
            bits = lax.bitcast_convert_type(acc, jnp.int32)
            okey = bits ^ ((bits >> 31) & jnp.int32(0x7FFFFFFF))
            k_pos = r0 + lax.broadcasted_iota(jnp.int32, (DSA_KA, 1), 0)
            key_s[pl.ds(r0, DSA_KA), :] = jnp.where(k_pos <= q_pos, okey, jnp.int32(INT_MIN))
        return carry

    lax.fori_loop(0, n_chunks, score_chunk, 0)

    def count_ge(cand):
        def body(c, cnt):
            r0 = pl.multiple_of(c * DSA_KC, DSA_KC)
            blk = key_s[pl.ds(r0, DSA_KC), :]
            ge = jnp.where(blk >= cand, jnp.int32(1), jnp.int32(0))
            return cnt + jnp.sum(ge.reshape(DSA_KC // SUBLANES, SUBLANES, DSA_TQ), axis=0)
        cnt8 = lax.fori_loop(0, n_chunks, body, jnp.zeros((SUBLANES, DSA_TQ), jnp.int32))
        return jnp.sum(cnt8, axis=0, keepdims=True)

    def bisect(it, u):
        cand_u = u | lax.shift_left(jnp.int32(1), 31 - it)
        cnt = count_ge(cand_u ^ jnp.int32(INT_MIN))
        return jnp.where(cnt >= n_top, cand_u, u)

    thr = lax.fori_loop(0, 32, bisect, jnp.zeros((1, DSA_TQ), jnp.int32)) ^ jnp.int32(INT_MIN)

    def bias_chunk(c, carry):
        r0 = pl.multiple_of(c * DSA_KC, DSA_KC)
        blk = key_s[pl.ds(r0, DSA_KC), :]
        k_pos = r0 + lax.broadcasted_iota(jnp.int32, (DSA_KC, 1), 0)
        sel = jnp.logical_and(blk >= thr, k_pos <= q_pos)
        sc_s[pl.ds(r0, DSA_KC), :] = jnp.where(sel, 0.0, NEG_BIG)
        return carry

    lax.fori_loop(0, n_chunks, bias_chunk, 0)

    gw = A_GROUP * DSA_TQ
    for g in range(A_KV_HEADS):
        m_s[...] = jnp.full(m_s.shape, NEG_BIG, jnp.float32)
        l_s[...] = jnp.zeros(l_s.shape, jnp.float32)
        acc_s[...] = jnp.zeros(acc_s.shape, jnp.float32)

        def attn_chunk(c, carry, g=g):
            r0 = pl.multiple_of(c * DSA_KC, DSA_KC)
            kc = k_ref[pl.ds(r0, DSA_KC), g * A_HEAD_DIM:(g + 1) * A_HEAD_DIM]
            s = _dot(kc, qg_s[g])
            bias = sc_s[pl.ds(r0, DSA_KC), :]
            s = jnp.concatenate([s[:, j * DSA_TQ:(j + 1) * DSA_TQ] + bias for j in range(A_GROUP)], axis=1)
            m_old = m_s[...]
            m_new = jnp.maximum(m_old, jnp.max(s, axis=0, keepdims=True))
            alpha = jnp.exp(m_old - m_new)
            p = jnp.exp(s - m_new)
            l_s[...] = alpha * l_s[...] + jnp.sum(p, axis=0, keepdims=True)
            vtc = vt_ref[g * A_HEAD_DIM:(g + 1) * A_HEAD_DIM, pl.ds(r0, DSA_KC)]
            acc_s[...] = alpha * acc_s[...] + _dot(vtc, _bf(p))
            m_s[...] = m_new
            return carry

        lax.fori_loop(0, n_chunks, attn_chunk, 0)
        o_t = acc_s[...] / l_s[...]
        for j in range(A_GROUP):
            h = g * A_GROUP + j
            o_ref[:, h * A_HEAD_DIM:(h + 1) * A_HEAD_DIM] = _bf(o_t[:, j * DSA_TQ:(j + 1) * DSA_TQ].T)


def _dsa_attention(q_t, k_r, v_t, qi_t, ki_r, w_t, n_top):
    L = k_r.shape[0]
    n_q = A_HEADS * A_HEAD_DIM
    n_kv = A_KV_HEADS * A_HEAD_DIM
    n_qi = IDX_HEADS * IDX_DIM
    assert L % DSA_KC == 0
    resident = dict(pipeline_mode=pl.Buffered(1))
    return pl.pallas_call(
        functools.partial(_dsa_kernel, n_top=n_top),
        out_shape=jax.ShapeDtypeStruct((L, n_q), MXU_DTYPE),
        grid=(L // DSA_TQ,),
        in_specs=[
            pl.BlockSpec((n_q, DSA_TQ), lambda i: (0, i)),
            pl.BlockSpec((n_qi, DSA_TQ), lambda i: (0, i)),
            pl.BlockSpec((LANES, DSA_TQ), lambda i: (0, i)),
            pl.BlockSpec((L, LANES), lambda i: (0, 0), **resident),
            pl.BlockSpec((L, n_kv), lambda i: (0, 0), **resident),
            pl.BlockSpec((n_kv, L), lambda i: (0, 0), **resident),
        ],
        out_specs=pl.BlockSpec((DSA_TQ, n_q), lambda i: (i, 0)),
        scratch_shapes=[
            pltpu.VMEM((IDX_DIM, IDX_HEADS * DSA_TQ), MXU_DTYPE),
            pltpu.VMEM((A_KV_HEADS, A_HEAD_DIM, A_GROUP * DSA_TQ), MXU_DTYPE),
            pltpu.VMEM((L, DSA_TQ), jnp.float32),
            pltpu.VMEM((L, DSA_TQ), jnp.int32),
            pltpu.VMEM((1, A_GROUP * DSA_TQ), jnp.float32),
            pltpu.VMEM((1, A_GROUP * DSA_TQ), jnp.float32),
            pltpu.VMEM((A_HEAD_DIM, A_GROUP * DSA_TQ), jnp.float32),
        ],
        compiler_params=_cparams(("arbitrary",)),
        name="dsa_attention",
    )(q_t, qi_t, w_t, ki_r, k_r, v_t)


def _rope_tables(L, n_half, period, theta_pow_dim):
    inv = ROPE_THETA ** (-jnp.arange(0, theta_pow_dim, 2, dtype=jnp.float32) / theta_pow_dim)
    ang = jnp.arange(L, dtype=jnp.float32)[:, None] * inv[None, :]
    cos, sin = jnp.cos(ang), jnp.sin(ang)
    pad = period - 2 * n_half
    tc = jnp.concatenate([cos, cos, jnp.ones((L, pad), jnp.float32)], axis=1)
    s1 = jnp.concatenate([-sin, jnp.zeros((L, period - n_half), jnp.float32)], axis=1)
    s2 = jnp.concatenate([jnp.zeros((L, n_half), jnp.float32), sin, jnp.zeros((L, pad), jnp.float32)], axis=1)
    reps = LANES // period
    return jnp.stack([jnp.tile(t, (1, reps)) for t in (tc, s1, s2)])


RWKV_CHUNK = 64
RWKV_SLAB = 512
RWKV_TB = 256
P_MU_R, P_MU_K, P_MU_V, P_W0, P_A0, P_KK, P_KA, P_RK, P_LNW, P_LNB = range(10)


def _head_ones():
    r = lax.broadcasted_iota(jnp.int32, (LANES, LANES), 0) // RWKV_HEAD
    c = lax.broadcasted_iota(jnp.int32, (LANES, LANES), 1) // RWKV_HEAD
    return jnp.where(r == c, 1.0, 0.0).astype(MXU_DTYPE)


def _head_sum(x, ones_bd):
    tiles = [_dot_exact_rhs(x[:, t * LANES:(t + 1) * LANES], ones_bd) for t in range(x.shape[1] // LANES)]
    return jnp.concatenate(tiles, axis=1)


def _pair_stack(x):
    lane = lax.broadcasted_iota(jnp.int32, x.shape, 1)
    return jnp.concatenate([jnp.where(lane < RWKV_HEAD, x, 0.0), jnp.where(lane >= RWKV_HEAD, x, 0.0)], axis=0)


def _rwkv_kernel(zr_ref, zk_ref, zv_ref, zs_ref, hr_ref, hk_ref, hv_ref, hs_ref, p_ref, mus_ref, lora_ref,
                 o_ref, h_s, r_s, lw_s, k2_s, v_s, kk_s, a_s, g_s, bon_s, oo_s):
    C = RWKV_CHUNK
    tb = zr_ref.shape[0]
    i = pl.program_id(1)

    @pl.when(i == 0)
    def _():
        h_s[...] = jnp.zeros_like(h_s)

    row = lax.broadcasted_iota(jnp.int32, (tb, 1), 0)
    has_prev = jnp.where(i > 0, 1.0, 0.0)

    def mix(z_ref, halo_ref, mu):
        z = z_ref[...]
        prev = halo_ref[SUBLANES - 1:SUBLANES, :] * has_prev
        shifted = jnp.where(row == 0, prev, pltpu.roll(z, 1, 0))
        return z + (shifted - z) * mu

    r = mix(zr_ref, hr_ref, p_ref[P_MU_R:P_MU_R + 1, :])
    k = mix(zk_ref, hk_ref, p_ref[P_MU_K:P_MU_K + 1, :])
    v = mix(zv_ref, hv_ref, p_ref[P_MU_V:P_MU_V + 1, :])
    sm = mix(zs_ref, hs_ref, mus_ref[...])
    wd, ad, gd = sm[:, 0:LANES], sm[:, LANES:2 * LANES], sm[:, 2 * LANES:4 * LANES]
    x = -(p_ref[P_W0:P_W0 + 1, :] + _dot_x3(jnp.tanh(wd), lora_ref[0:LANES, :]))
    softplus = jnp.maximum(x, 0.0) + jnp.log(1.0 + jnp.exp(-jnp.abs(x)))
    lw = -jnp.exp(-softplus - 0.5)
    a = _sigmoid(p_ref[P_A0:P_A0 + 1, :] + _dot_x3(ad, lora_ref[LANES:2 * LANES, :]))
    g_s[...] = _dot_x3(_sigmoid(gd), lora_ref[2 * LANES:4 * LANES, :])
    ones_bd = _head_ones()
    kk = k * p_ref[P_KK:P_KK + 1, :]
    kk = kk * lax.rsqrt(jnp.maximum(_head_sum(kk * kk, ones_bd), 1e-24))
    k2 = k * (1.0 + (a - 1.0) * p_ref[P_KA:P_KA + 1, :])
    bon_s[...] = _head_sum(r * k2 * p_ref[P_RK:P_RK + 1, :], ones_bd)
    r_s[...] = r
    lw_s[...] = lw
    k2_s[...] = k2
    v_s[...] = v
    kk_s[...] = kk
    a_s[...] = a

    n2 = 2 * C
    rr = lax.broadcasted_iota(jnp.int32, (n2, n2), 0)
    cc = lax.broadcasted_iota(jnp.int32, (n2, n2), 1)
    strict = (cc % C) < (rr % C)
    incl = (cc % C) <= (rr % C)
    eye = jnp.where(rr == cc, 1.0, 0.0)
    tril_c = jnp.where(lax.broadcasted_iota(jnp.int32, (C, C), 1) <= lax.broadcasted_iota(jnp.int32, (C, C), 0),
                       1.0, 0.0).astype(MXU_DTYPE)

    def chunk(c, carry):
        r0 = pl.multiple_of(c * C, C)
        for p in range(RWKV_SLAB // LANES):
            ls = slice(p * LANES, (p + 1) * LANES)
            rc, lwc, kc, vc = r_s[pl.ds(r0, C), ls], lw_s[pl.ds(r0, C), ls], k2_s[pl.ds(r0, C), ls], v_s[pl.ds(r0, C), ls]
            kkc, ac = kk_s[pl.ds(r0, C), ls], a_s[pl.ds(r0, C), ls]
            l1, l2, l3 = _split3(lwc)
            cum = _dot(tril_c, l1) + _dot(tril_c, l2) + _dot(tril_c, l3)
            cl = cum[C - 1:C, :]
            e_neg = jnp.exp(-cum)
            e_end = jnp.exp(cl - cum)
            beta = kkc * ac
            ae_s = _pair_stack(-kkc * jnp.exp(cum - lwc))
            rp_s = _pair_stack(rc * jnp.exp(cum))
            bm_s = _pair_stack(beta * e_neg)
            km_s = _pair_stack(kc * e_neg)
            bt_s = _pair_stack(beta * e_end)
            kt_s = _pair_stack(kc * e_end)
            v_st = _bf(_pair_stack(vc))
            aa = _dot_nt(_bf(jnp.concatenate([ae_s, rp_s], axis=0)), _bf(jnp.concatenate([bm_s, km_s], axis=0)))
            a_ab = jnp.where(strict, aa[0:n2, 0:n2], 0.0)
            a_ak = jnp.where(strict, aa[0:n2, n2:2 * n2], 0.0)
            a_rb = jnp.where(incl, aa[n2:2 * n2, 0:n2], 0.0)
            a_rk = jnp.where(incl, aa[n2:2 * n2, n2:2 * n2], 0.0)
            t_inv = eye + a_ab
            n_pow = a_ab
            for _ in range(int(math.log2(C)) - 1):
                n_pow = _dot(_bf(n_pow), _bf(n_pow))
                t_inv = t_inv + _dot(_bf(t_inv), _bf(n_pow))
            t_b = _bf(t_inv)
            a_til = _dot(t_b, _bf(ae_s))
            v_til = _dot(t_b, _bf(_dot(_bf(a_ak), v_st)))
            o_intra = _dot(_bf(a_rk), v_st)
            h_kv = _dot(_bf(kt_s.T), v_st)
            h_old = h_s[p]
            h_b = _bf(h_old)
            u = _dot(_bf(a_til), h_b) + v_til
            u_b = _bf(u)
            o_st = _dot(_bf(rp_s), h_b) + _dot(_bf(a_rb), u_b) + o_intra
            decay_col = jnp.exp(jnp.broadcast_to(cl, (LANES, LANES)).T)
            h_s[p] = decay_col * h_old + _dot(_bf(bt_s.T), u_b) + h_kv
            oo_s[pl.ds(r0, C), ls] = o_st[0:C, :] + o_st[C:n2, :]
        return carry

    lax.fori_loop(0, tb // C, chunk, 0)

    o = oo_s[...]
    mean = _head_sum(o, ones_bd) * (1.0 / RWKV_HEAD)
    d = o - mean
    var = _head_sum(d * d, ones_bd) * (1.0 / RWKV_HEAD)
    y = d * lax.rsqrt(var + RWKV_GN_EPS) * p_ref[P_LNW:P_LNW + 1, :] + p_ref[P_LNB:P_LNB + 1, :]
    y = y + bon_s[...] * v_s[...]
    o_ref[...] = (y * g_s[...]).astype(o_ref.dtype)


def _rwkv_params(mu, w0, w_up, a0, a_up, g_up, k_k, k_a, r_k, ln_w, ln_b):
    d = RWKV_DIM
    mu_r, mu_k, mu_v = mu[0:d], mu[d:2 * d], mu[2 * d:3 * d]
    o = 3 * d
    mu_wd, mu_ad, mu_gd = mu[o:o + DECAY_RANK], mu[o + DECAY_RANK:o + DECAY_RANK + AAA_RANK], mu[o + DECAY_RANK + AAA_RANK:]
    rows = [mu_r, mu_k, mu_v, w0, a0, k_k, k_a, r_k.reshape(d), ln_w, ln_b]
    p_rows = jnp.concatenate([jnp.stack(rows), jnp.zeros((16 - len(rows), d), jnp.float32)], axis=0)

    def pad_to(x, n, axis):
        widths = [(0, 0)] * x.ndim
        widths[axis] = (0, n - x.shape[axis])
        return jnp.pad(x, widths)

    mu_small = jnp.concatenate([pad_to(mu_wd, LANES, 0), pad_to(mu_ad, LANES, 0), mu_gd]).reshape(1, 4 * LANES)
    lora = jnp.concatenate([pad_to(w_up, LANES, 0), pad_to(a_up, LANES, 0), g_up], axis=0)
    return p_rows, mu_small, lora


def _rwkv_mix(z, p_rows, mu_small, lora):
    L = z.shape[0]
    tb, w = RWKV_TB, RWKV_SLAB
    assert L % tb == 0
    hb = tb // SUBLANES

    def blk(col0):
        return pl.BlockSpec((tb, w), lambda s, i: (i, col0 // w + s))

    def halo(col0):
        return pl.BlockSpec((SUBLANES, w), lambda s, i: (jnp.maximum(i * hb - 1, 0), col0 // w + s))

    small_w = 4 * LANES
    f32 = jnp.float32
    return pl.pallas_call(
        _rwkv_kernel,
        out_shape=jax.ShapeDtypeStruct((L, RWKV_DIM), MXU_DTYPE),
        grid=(RWKV_DIM // w, L // tb),
        in_specs=[
            blk(ZE_R), blk(ZE_RK), blk(ZE_RV),
            pl.BlockSpec((tb, small_w), lambda s, i: (i, ZE_WD // small_w)),
            halo(ZE_R), halo(ZE_RK), halo(ZE_RV),
            pl.BlockSpec((SUBLANES, small_w), lambda s, i: (jnp.maximum(i * hb - 1, 0), ZE_WD // small_w)),
            pl.BlockSpec((16, w), lambda s, i: (0, s)),
            pl.BlockSpec((1, small_w), lambda s, i: (0, 0)),
            pl.BlockSpec((small_w, w), lambda s, i: (0, s)),
        ],
        out_specs=pl.BlockSpec((tb, w), lambda s, i: (i, s)),
        scratch_shapes=[pltpu.VMEM((w // LANES, LANES, LANES), f32)] + [pltpu.VMEM((tb, w), f32) for _ in range(9)],
        compiler_params=_cparams(("parallel", "arbitrary")),
        name="rwkv7_mix",
    )(z, z, z, z, z, z, z, z, p_rows, mu_small, lora)


S5_TB = 512
S5_SEG = SUBLANES
S5_SLAB_GROUPS = LANES // S5_GROUP
S5_SW = S5_SLAB_GROUPS * S5_STATE


def _cmul_add(ar, ai, xr, xi, br, bi):
    return ar * xr - ai * xi + br, ar * xi + ai * xr + bi


def _s5_kernel(u_ref, w_ref, c_ref, apow_ref, d_ref, zg_ref, zgb_ref, st_s, up_s, bu_s, x_s, y_s):
    tb = u_ref.shape[0]
    ts = tb // S5_SEG
    sw = S5_SW
    i = pl.program_id(1)

    @pl.when(i == 0)
    def _():
        st_s[...] = jnp.zeros_like(st_s)

    for tau in range(ts):
        up_s[tau * S5_SEG:(tau + 1) * S5_SEG, :] = u_ref[pl.ds(tau, S5_SEG, stride=ts), :]
    bu_s[...] = _dot_x3(up_s[...], w_ref[0])
    a1 = apow_ref[0, 0:1, :]
    ar = jnp.broadcast_to(a1[:, 0:sw], (S5_SEG, sw))
    ai = jnp.broadcast_to(a1[:, sw:2 * sw], (S5_SEG, sw))

    def scan(tau, x):
        r0 = pl.multiple_of(tau * S5_SEG, S5_SEG)
        b = bu_s[pl.ds(r0, S5_SEG), :]
        nr, ni = _cmul_add(ar, ai, x[0], x[1], b[:, 0:sw], b[:, sw:2 * sw])
        x_s[pl.ds(r0, S5_SEG), :] = jnp.concatenate([nr, ni], axis=1)
        return nr, ni

    zero = jnp.zeros((S5_SEG, sw), jnp.float32)
    xr, xi = lax.fori_loop(0, ts, scan, (zero, zero))

    a_ts = apow_ref[0, ts - 1:ts, :]
    tr, ti = a_ts[:, 0:sw], a_ts[:, sw:2 * sw]
    cr, ci = st_s[:, 0:sw], st_s[:, sw:2 * sw]
    ent_r, ent_i = [], []
    for s in range(S5_SEG):
        ent_r.append(cr)
        ent_i.append(ci)
        cr, ci = _cmul_add(tr, ti, cr, ci, xr[s:s + 1, :], xi[s:s + 1, :])
    st_s[...] = jnp.concatenate([cr, ci], axis=1)
    er = jnp.concatenate(ent_r, axis=0)
    ei = jnp.concatenate(ent_i, axis=0)

    def fix(tau, carry):
        r0 = pl.multiple_of(tau * S5_SEG, S5_SEG)
        ap = apow_ref[0, pl.ds(tau, 1), :]
        x = x_s[pl.ds(r0, S5_SEG), :]
        nr, ni = _cmul_add(ap[:, 0:sw], ap[:, sw:2 * sw], er, ei, x[:, 0:sw], x[:, sw:2 * sw])
        x_s[pl.ds(r0, S5_SEG), :] = jnp.concatenate([nr, ni], axis=1)
        return carry

    lax.fori_loop(0, ts, fix, 0)
    yp = _dot_x3(x_s[...], c_ref[0])
    for tau in range(ts):
        y_s[pl.ds(tau, S5_SEG, stride=ts), :] = yp[tau * S5_SEG:(tau + 1) * S5_SEG, :]
    y = y_s[...] + d_ref[...] * u_ref[...]
    zg = 0.5 * y * (1.0 + jnp.tanh(math.sqrt(2.0 / math.pi) * (y + 0.044715 * (y * y * y))))
    zg_ref[...] = zg
    zgb_ref[...] = zg.astype(zgb_ref.dtype)


def _s5_params(lam_re, lam_im, log_step, b_re, b_im, c_re, c_im, n_pow):
    lr = jnp.minimum(lam_re, -1e-4)
    li = lam_im
    step = jnp.exp(log_step)[:, None]
    mag = jnp.exp(lr * step)
    abar_r = mag * jnp.cos(li * step)
    abar_i = mag * jnp.sin(li * step)
    den = lr * lr + li * li
    cr = (lr * (abar_r - 1.0) + li * abar_i) / den
    ci = (lr * abar_i - li * (abar_r - 1.0)) / den
    bbar_r = cr[..., None] * b_re - ci[..., None] * b_im
    bbar_i = cr[..., None] * b_im + ci[..., None] * b_re
    ns = S5_GROUPS // S5_SLAB_GROUPS
    eye = jnp.eye(S5_SLAB_GROUPS, dtype=jnp.float32)

    def in_mat(bb):
        t = jnp.einsum('ab,sapi->saibp', eye, bb.reshape(ns, S5_SLAB_GROUPS, S5_STATE, S5_GROUP))
        return t.reshape(ns, LANES, S5_SW)

    def out_mat(cc):
        t = jnp.einsum('ab,saop->sapbo', eye, cc.reshape(ns, S5_SLAB_GROUPS, S5_GROUP, S5_STATE))
        return t.reshape(ns, S5_SW, LANES)

    w_in = jnp.concatenate([in_mat(bbar_r), in_mat(bbar_i)], axis=2)
    w_out = jnp.concatenate([out_mat(c_re), -out_mat(c_im)], axis=1)
    n = jnp.arange(1, n_pow + 1, dtype=jnp.float32)[None, :, None]
    lrs = (lr * step).reshape(ns, 1, S5_SW)
    lis = (li * step).reshape(ns, 1, S5_SW)
    pm = jnp.exp(n * lrs)
    apow = jnp.concatenate([pm * jnp.cos(n * lis), pm * jnp.sin(n * lis)], axis=2)
    return w_in, w_out, apow


def _s5_mix(z, w_in, w_out, apow, d_skip):
    L = z.shape[0]
    tb = S5_TB
    assert L % tb == 0
    ns = S5_DIM // LANES
    f32 = jnp.float32
    return pl.pallas_call(
        _s5_kernel,
        out_shape=(jax.ShapeDtypeStruct((L, S5_DIM), f32), jax.ShapeDtypeStruct((L, S5_DIM), MXU_DTYPE)),
        grid=(ns, L // tb),
        in_specs=[
            pl.BlockSpec((tb, LANES), lambda s, i: (i, s)),
            pl.BlockSpec((1, LANES, 2 * S5_SW), lambda s, i: (s, 0, 0)),
            pl.BlockSpec((1, 2 * S5_SW, LANES), lambda s, i: (s, 0, 0)),
            pl.BlockSpec((1, tb // S5_SEG, 2 * S5_SW), lambda s, i: (s, 0, 0)),
            pl.BlockSpec((1, LANES), lambda s, i: (0, s)),
        ],
        out_specs=(pl.BlockSpec((tb, LANES), lambda s, i: (i, s)), pl.BlockSpec((tb, LANES), lambda s, i: (i, s))),
        scratch_shapes=[
            pltpu.VMEM((1, 2 * S5_SW), f32),
            pltpu.VMEM((tb, LANES), f32),
            pltpu.VMEM((tb, 2 * S5_SW), f32),
            pltpu.VMEM((tb, 2 * S5_SW), f32),
            pltpu.VMEM((tb, LANES), f32),
        ],
        compiler_params=_cparams(("parallel", "arbitrary")),
        name="s5_mix",
    )(z, w_in, w_out, apow, d_skip.reshape(1, S5_DIM))


RET_CHUNK = 256


def _ret_kernel(q_ref, k_ref, v_ref, g_ref, cos_ref, sin_ref, intra_ref, rowdec_ref, o_ref, s_s):
    c = pl.program_id(1)

    @pl.when(c == 0)
    def _():
        s_s[...] = jnp.zeros_like(s_s)

    half = RET_HEAD_DIM // 2
    cos, sin = cos_ref[...], sin_ref[...]

    def rot(x):
        x1, x2 = x[:, 0:half], x[:, half:2 * half]
        return jnp.concatenate([x1 * cos - x2 * sin, x1 * sin + x2 * cos], axis=1)

    q = rot(q_ref[...])
    k = rot(k_ref[...]) * (RET_HEAD_DIM ** -0.5)
    vb = _bf(v_ref[...])
    qb = _bf(q)
    dec = rowdec_ref[0]

    def lanes2(x):
        return jnp.concatenate([x, x], axis=1)

    xi, zeta, g_chunk = lanes2(dec[:, 0:LANES]), lanes2(dec[:, LANES:2 * LANES]), lanes2(dec[:, 2 * LANES:3 * LANES])
    att = _dot_nt(qb, _bf(k)) * intra_ref[0]
    s_old = s_s[...]
    o = _dot(_bf(att), vb) + _dot(qb, _bf(s_old)) * xi
    s_s[...] = s_old * g_chunk + _dot(_bf((k * zeta).T), vb)
    mean = jnp.mean(o, axis=-1, keepdims=True)
    d = o - mean
    var = jnp.mean(d * d, axis=-1, keepdims=True)
    gate = g_ref[...]
    o_ref[...] = (gate * _sigmoid(gate) * (d * lax.rsqrt(var + RET_GN_EPS))).astype(o_ref.dtype)


def _ret_tables(L):
    C = RET_CHUNK
    inv = 1.0 / (RET_ROPE_BASE ** jnp.linspace(0.0, 1.0, RET_HEAD_DIM // 2, dtype=jnp.float32))
    ang = jnp.arange(L, dtype=jnp.float32)[:, None] * inv[None, :]
    log_g = jnp.log(1.0 - 2.0 ** (-5.0 - jnp.arange(RET_HEADS, dtype=jnp.float32)))
    pos = jnp.arange(C, dtype=jnp.float32)
    diff = pos[:, None] - pos[None, :]
    intra = jnp.where(diff >= 0, jnp.exp(jnp.maximum(diff, 0.0)[None] * log_g[:, None, None]), 0.0)
    xi = jnp.exp((pos + 1.0)[None, :] * log_g[:, None])
    zeta = jnp.exp((C - 1.0 - pos)[None, :] * log_g[:, None])
    g_chunk = jnp.broadcast_to(jnp.exp(C * log_g)[:, None], (RET_HEADS, C))
    rowdec = jnp.concatenate([jnp.broadcast_to(t[:, :, None], (RET_HEADS, C, LANES)) for t in (xi, zeta, g_chunk)], axis=2)
    return jnp.cos(ang), jnp.sin(ang), intra, rowdec


def _retention_mix(z, cos, sin, intra, rowdec):
    L = z.shape[0]
    C, hd = RET_CHUNK, RET_HEAD_DIM
    assert L % C == 0
    base = S5_DIM // hd

    def blk(j):
        return pl.BlockSpec((C, hd), lambda h, c: (c, base + j * RET_HEADS + h))

    return pl.pallas_call(
        _ret_kernel,
        out_shape=jax.ShapeDtypeStruct((L, RET_HEADS * hd), MXU_DTYPE),
        grid=(RET_HEADS, L // C),
        in_specs=[
            blk(0), blk(1), blk(2), blk(3),
            pl.BlockSpec((C, hd // 2), lambda h, c: (c, 0)),
            pl.BlockSpec((C, hd // 2), lambda h, c: (c, 0)),
            pl.BlockSpec((1, C, C), lambda h, c: (h, 0, 0)),
            pl.BlockSpec((1, C, 3 * LANES), lambda h, c: (h, 0, 0)),
        ],
        out_specs=pl.BlockSpec((C, hd), lambda h, c: (c, h)),
        scratch_shapes=[pltpu.VMEM((hd, hd), jnp.float32)],
        compiler_params=_cparams(("parallel", "arbitrary")),
        name="retention_mix",
    )(z, z, z, z, cos, sin, intra, rowdec)


def _even_w_in(w):
    d = w.shape[0]
    n_attn = ZE_R
    o_ki = n_attn
    o_wi = o_ki + IDX_DIM
    o_rw = o_wi + IDX_HEADS
    o_wd = o_rw + 3 * RWKV_DIM
    o_ad = o_wd + DECAY_RANK
    o_gd = o_ad + AAA_RANK

    def zeros(n):
        return jnp.zeros((d, n), w.dtype)

    ki = w[:, o_ki:o_wi]
    parts = [
        w[:, 0:n_attn], w[:, o_rw:o_wd],
        ki, ki, w[:, o_wi:o_rw], zeros(ZE_WD - ZE_WI - IDX_HEADS),
        w[:, o_wd:o_ad], zeros(LANES - DECAY_RANK),
        w[:, o_ad:o_gd], zeros(LANES - AAA_RANK),
        w[:, o_gd:o_gd + GATE_RANK],
    ]
    out = jnp.concatenate(parts, axis=1).astype(MXU_DTYPE)
    assert out.shape[1] == ZE_COLS
    return out


def _ffn(h, norm_g, w_gate, w_up, w_down):
    hn = _rmsnorm(h, norm_g, MXU_DTYPE)
    mid = _matmul([hn], [_bf(w_gate), _bf(w_up)], [(0, 0), (0, 1)], [], _ep_swiglu, MXU_DTYPE,
                  tm=1024, tn=256, name="ffn_gate_up")
    return _matmul([mid], [_bf(w_down)], [(0, 0)], [("tile", h)], _ep_resid, jnp.float32,
                   tm=1024, tn=512, tk=w_down.shape[0] // 2, name="ffn_down")


def _out_proj(h, o_first, o_second, w_out):
    half = w_out.shape[0] // 2
    return _matmul([o_first, o_second], [_bf(w_out[:half]), _bf(w_out[half:])], [(0, 0), (1, 1)], [("tile", h)],
                   _ep_resid2, jnp.float32, tm=1024, tn=1024, name="out_proj")


def kernel(x, norm_mix, norm_ffn, ffn_gate, ffn_up, ffn_down, e_w_in, e_w_out, e_mu, e_w0, e_w_up, e_a0, e_a_up, e_g_up, e_k_k, e_k_a, e_r_k, e_ln_w, e_ln_b, o_w_in, o_w_out, o_lam_re, o_lam_im, o_log_step, o_b_re, o_b_im, o_c_re, o_c_im, o_d_skip, o_w_glu, o_b_glu, final_norm):
    assert x.shape[0] == 1
    h = x[0]
    L = h.shape[0]
    n_top = min(TOPK_MAX, L // 4)

    hn = _rmsnorm(h, norm_mix[0], MXU_DTYPE)
    z = _matmul([hn], [_even_w_in(e_w_in[0])], [(0, 0)], [], _ep_plain, jnp.float32, tm=1024, tn=1024, name="even_in_proj")
    tab_a = _rope_tables(L, A_ROT_HALF, A_HEAD_DIM, 2 * A_ROT_HALF)
    tab_i = _rope_tables(L, IDX_ROT_HALF, IDX_DIM, 2 * IDX_ROT_HALF)
    q_t, k_r, v_t, qi_t, ki_r, w_t = _dsa_prep(z, tab_a, tab_i)
    o_a = _dsa_attention(q_t, k_r, v_t, qi_t, ki_r, w_t, n_top)
    p_rows, mu_small, lora = _rwkv_params(e_mu[0], e_w0[0], e_w_up[0], e_a0[0], e_a_up[0], e_g_up[0], e_k_k[0],
                                          e_k_a[0], e_r_k[0], e_ln_w[0], e_ln_b[0])
    o_b = _rwkv_mix(z, p_rows, mu_small, lora)
    h = _out_proj(h, o_a, o_b, e_w_out[0])
    h = _ffn(h, norm_ffn[0], ffn_gate[0], ffn_up[0], ffn_down[0])

    hn = _rmsnorm(h, norm_mix[1], MXU_DTYPE)
    z = _matmul([hn], [_bf(o_w_in[0])], [(0, 0)], [], _ep_plain, jnp.float32, tm=1024, tn=1024, name="odd_in_proj")
    s5_in, s5_out, s5_apow = _s5_params(o_lam_re[0], o_lam_im[0], o_log_step[0], o_b_re[0], o_b_im[0], o_c_re[0],
                                        o_c_im[0], S5_TB // S5_SEG)
    zg, zg_b = _s5_mix(z, s5_in, s5_out, s5_apow, o_d_skip[0])
    o_c = _matmul([zg_b], [_bf(o_w_glu[0])], [(0, 0)], [("tile", zg), ("row", o_b_glu[0].reshape(1, S5_DIM))], _ep_glu,
                  MXU_DTYPE, tm=1024, tn=1024, name="s5_glu")
    o_d = _retention_mix(z, *_ret_tables(L))
    h = _out_proj(h, o_c, o_d, o_w_out[0])
    h = _ffn(h, norm_ffn[1], ffn_gate[1], ffn_up[1], ffn_down[1])

    return _rmsnorm(h, final_norm, jnp.float32)[None]
```

```python
import functools
import math

import jax
import jax.numpy as jnp
from jax import lax
from jax.experimental import pallas as pl
from jax.experimental.pallas import tpu as pltpu

NORM_EPS = 1e-6
A_HEAD_DIM = 128
A_HEADS = 16
A_KV_HEADS = 4
A_GROUP = A_HEADS // A_KV_HEADS
A_ROT_HALF = 16
IDX_HEADS = 32
IDX_DIM = 64
IDX_ROT_HALF = 8
TOPK_MAX = 256
ROPE_THETA = 500000.0
RWKV_HEAD = 64
RWKV_DIM = 2048
RWKV_HEADS = 32
DECAY_RANK = 96
AAA_RANK = 96
GATE_RANK = 256
RWKV_GN_EPS = 1e-5 * RWKV_HEAD
S5_DIM = 2048
S5_GROUP = 16
S5_GROUPS = 128
S5_STATE = 64
RET_HEAD_DIM = 256
RET_HEADS = 8
RET_ROPE_BASE = 10000.0
RET_GN_EPS = 1e-5

LANES = 128
SUBLANES = 8
VMEM_LIMIT_BYTES = 56 * 2**20

MXU_DTYPE = jnp.bfloat16
NEG_BIG = -1e30
INT_MIN = -2**31

ZE_Q, ZE_K, ZE_V, ZE_QI = 0, 2048, 2560, 3072
ZE_R, ZE_RK, ZE_RV = 5120, 7168, 9216
ZE_KI2, ZE_WI, ZE_WD, ZE_AD, ZE_GD = 11264, 11392, 11776, 11904, 12032
ZE_COLS = 12288


def _cparams(sem):
    return pltpu.CompilerParams(dimension_semantics=sem, vmem_limit_bytes=VMEM_LIMIT_BYTES)


def _bf(x):
    return x.astype(MXU_DTYPE)


def _dot(a, b):
    return jnp.dot(a, b, preferred_element_type=jnp.float32)


def _dot_nt(a, b):
    return lax.dot_general(a, b, (((1,), (1,)), ((), ())), preferred_element_type=jnp.float32)


def _split3(x):
    h1 = _bf(x)
    r1 = x - h1.astype(jnp.float32)
    h2 = _bf(r1)
    h3 = _bf(r1 - h2.astype(jnp.float32))
    return h1, h2, h3


def _dot_x3(a, b):
    ah = _bf(a)
    al = _bf(a - ah.astype(jnp.float32))
    bh = _bf(b)
    bl = _bf(b - bh.astype(jnp.float32))
    return _dot(ah, bh) + _dot(ah, bl) + _dot(al, bh)


def _dot_exact_rhs(a, b_bf):
    h1, h2, h3 = _split3(a)
    return _dot(h1, b_bf) + _dot(h2, b_bf) + _dot(h3, b_bf)


def _sigmoid(x):
    return 1.0 / (1.0 + jnp.exp(-x))


def _rmsnorm_kernel(x_ref, g_ref, o_ref):
    x = x_ref[...]
    ms = jnp.mean(x * x, axis=-1, keepdims=True)
    o_ref[...] = (x * lax.rsqrt(ms + NORM_EPS) * g_ref[...]).astype(o_ref.dtype)


def _rmsnorm(x, g, out_dtype, tm=256):
    m, d = x.shape
    return pl.pallas_call(
        _rmsnorm_kernel,
        out_shape=jax.ShapeDtypeStruct((m, d), out_dtype),
        grid=(m // tm,),
        in_specs=[pl.BlockSpec((tm, d), lambda i: (i, 0)), pl.BlockSpec((1, d), lambda i: (0, 0))],
        out_specs=pl.BlockSpec((tm, d), lambda i: (i, 0)),
        compiler_params=_cparams(("parallel",)),
        name="rmsnorm",
    )(x, g.reshape(1, d))


def _mm_kernel(*refs, n_a, n_w, n_e, pairs, epilogue, nk):
    a_refs = refs[:n_a]
    w_refs = refs[n_a:n_a + n_w]
    e_refs = refs[n_a + n_w:n_a + n_w + n_e]
    o_ref = refs[n_a + n_w + n_e]
    acc_refs = refs[n_a + n_w + n_e + 1:]
    if nk == 1:
        accs = [_dot(a_refs[i][...], _bf(w_refs[j][...])) for i, j in pairs]
        o_ref[...] = epilogue(accs, [e[...] for e in e_refs]).astype(o_ref.dtype)
        return
    k = pl.program_id(2)

    @pl.when(k == 0)
    def _():
        for acc in acc_refs:
            acc[...] = jnp.zeros_like(acc)

    for acc, (i, j) in zip(acc_refs, pairs):
        acc[...] += _dot(a_refs[i][...], _bf(w_refs[j][...]))

    @pl.when(k == nk - 1)
    def _():
        o_ref[...] = epilogue([acc[...] for acc in acc_refs], [e[...] for e in e_refs]).astype(o_ref.dtype)


def _weight_spec(w, layer, k_off, tk, tn):
    if w.ndim == 3:
        return pl.BlockSpec((None, tk, tn), lambda i, j, k: (layer, k + k_off, j))
    return pl.BlockSpec((tk, tn), lambda i, j, k: (k + k_off, j))


def _matmul(a_list, w_list, pairs, extras, epilogue, out_dtype, tm, tn, tk=None, name="matmul"):
    m, kdim = a_list[0].shape
    n = w_list[0][0].shape[-1]
    tk = kdim if tk is None else tk
    tm = min(tm, m)
    nk = kdim // tk
    assert m % tm == 0 and n % tn == 0 and kdim % tk == 0
    in_specs = [pl.BlockSpec((tm, tk), lambda i, j, k: (i, k)) for _ in a_list]
    in_specs += [_weight_spec(w, layer, k_off, tk, tn) for w, layer, k_off in w_list]
    w_list = [w for w, _, _ in w_list]
    e_arrays = []
    for kind, arr in extras:
        e_arrays.append(arr)
        if kind == "tile":
            in_specs.append(pl.BlockSpec((tm, tn), lambda i, j, k: (i, j)))
        else:
            in_specs.append(pl.BlockSpec((1, tn), lambda i, j, k: (0, j)))
    scratch = [] if nk == 1 else [pltpu.VMEM((tm, tn), jnp.float32) for _ in pairs]
    kern = functools.partial(_mm_kernel, n_a=len(a_list), n_w=len(w_list), n_e=len(extras),
                             pairs=tuple(pairs), epilogue=epilogue, nk=nk)
    return pl.pallas_call(
        kern,
        out_shape=jax.ShapeDtypeStruct((m, n), out_dtype),
        grid=(m // tm, n // tn, nk),
        in_specs=in_specs,
        out_specs=pl.BlockSpec((tm, tn), lambda i, j, k: (i, j)),
        scratch_shapes=scratch,
        compiler_params=_cparams(("parallel", "parallel", "arbitrary")),
        name=name,
    )(*a_list, *w_list, *e_arrays)


def _ep_plain(accs, ex):
    return accs[0]


def _ep_swiglu(accs, ex):
    g = accs[0]
    return g * _sigmoid(g) * accs[1]


def _ep_resid(accs, ex):
    return ex[0] + accs[0]


def _ep_resid2(accs, ex):
    return ex[0] + (accs[0] + accs[1])


def _ep_glu(accs, ex):
    zg, b = ex
    return zg * _sigmoid(accs[0] + b)


def _rope_tile(x, tc, s1, s2, n_half):
    return x * tc + pltpu.roll(x, n_half, 1) * s2 + pltpu.roll(x, LANES - n_half, 1) * s1


DSA_VPAD = 16
DSA_VROWS = A_HEAD_DIM + DSA_VPAD


def _dsa_prep_kernel(zq_ref, zs_ref, ta_ref, ti_ref, qt_ref, k_ref, vt_ref, qit_ref, ki_ref, wt_ref):
    ta_c, ta_1, ta_2 = ta_ref[0], ta_ref[1], ta_ref[2]
    ti_c, ti_1, ti_2 = ti_ref[0], ti_ref[1], ti_ref[2]
    q_scale = A_HEAD_DIM ** -0.5 * math.log2(math.e)
    for h in range(A_HEADS):
        x = zq_ref[:, ZE_Q + h * LANES:ZE_Q + (h + 1) * LANES]
        y = _rope_tile(x, ta_c, ta_1, ta_2, A_ROT_HALF) * q_scale
        qt_ref[h * LANES:(h + 1) * LANES, :] = _bf(y.T)
    for g in range(A_KV_HEADS):
        x = zq_ref[:, ZE_K + g * LANES:ZE_K + (g + 1) * LANES]
        k_ref[:, g * LANES:(g + 1) * LANES] = _bf(_rope_tile(x, ta_c, ta_1, ta_2, A_ROT_HALF))
        v = zq_ref[:, ZE_V + g * LANES:ZE_V + (g + 1) * LANES]
        vt_ref[g * DSA_VROWS:g * DSA_VROWS + A_HEAD_DIM, :] = _bf(v.T)
        vt_ref[g * DSA_VROWS + A_HEAD_DIM:(g + 1) * DSA_VROWS, :] = jnp.ones((DSA_VPAD, v.shape[0]), MXU_DTYPE)
    for c in range(IDX_HEADS * IDX_DIM // LANES):
        x = zq_ref[:, ZE_QI + c * LANES:ZE_QI + (c + 1) * LANES]
        y = _rope_tile(x, ti_c, ti_1, ti_2, IDX_ROT_HALF)
        qit_ref[c * LANES:(c + 1) * LANES, :] = _bf(y.T)
    ki = _rope_tile(zs_ref[:, 0:LANES], ti_c, ti_1, ti_2, IDX_ROT_HALF)
    ki_ref[...] = _bf(ki)
    w_scale = (IDX_DIM ** -0.5) * (IDX_HEADS ** -0.5)
    wt_ref[...] = (zs_ref[:, LANES:2 * LANES] * w_scale).T


def _dsa_prep(z, tab_a, tab_i, tm=256):
    L = z.shape[0]
    n_q = A_HEADS * A_HEAD_DIM
    n_kv = A_KV_HEADS * A_HEAD_DIM
    n_qi = IDX_HEADS * IDX_DIM
    out_shape = (
        jax.ShapeDtypeStruct((n_q, L), MXU_DTYPE),
        jax.ShapeDtypeStruct((L, n_kv), MXU_DTYPE),
        jax.ShapeDtypeStruct((A_KV_HEADS * DSA_VROWS, L), MXU_DTYPE),
        jax.ShapeDtypeStruct((n_qi, L), MXU_DTYPE),
        jax.ShapeDtypeStruct((L, LANES), MXU_DTYPE),
        jax.ShapeDtypeStruct((LANES, L), jnp.float32),
    )
    return pl.pallas_call(
        _dsa_prep_kernel,
        out_shape=out_shape,
        grid=(L // tm,),
        in_specs=[
            pl.BlockSpec((tm, ZE_R), lambda i: (i, 0)),
            pl.BlockSpec((tm, 2 * LANES), lambda i: (i, ZE_KI2 // (2 * LANES))),
            pl.BlockSpec((3, tm, LANES), lambda i: (0, i, 0)),
            pl.BlockSpec((3, tm, LANES), lambda i: (0, i, 0)),
        ],
        out_specs=(
            pl.BlockSpec((n_q, tm), lambda i: (0, i)),
            pl.BlockSpec((tm, n_kv), lambda i: (i, 0)),
            pl.BlockSpec((A_KV_HEADS * DSA_VROWS, tm), lambda i: (0, i)),
            pl.BlockSpec((n_qi, tm), lambda i: (0, i)),
            pl.BlockSpec((tm, LANES), lambda i: (i, 0)),
            pl.BlockSpec((LANES, tm), lambda i: (0, i)),
        ),
        compiler_params=_cparams(("parallel",)),
        name="dsa_prep",
    )(z, z, tab_a, tab_i)


DSA_TQ = 128
DSA_KC = 512
DSA_KA = 256


def _dsa_kernel(qt_ref, qit_ref, wt_ref, ki_ref, k_ref, vt_ref, o_ref,
                qi_s, qg_s, sc_s, key_s, m_s, acc_s, *, n_top):
    qb = pl.program_id(0)
    q0 = qb * DSA_TQ
    n_chunks = (q0 + DSA_TQ + DSA_KC - 1) // DSA_KC
    q_pos = q0 + lax.broadcasted_iota(jnp.int32, (1, DSA_TQ), 1)

    for h in range(IDX_HEADS):
        qi_s[:, h * DSA_TQ:(h + 1) * DSA_TQ] = qit_ref[h * IDX_DIM:(h + 1) * IDX_DIM, :]
    for g in range(A_KV_HEADS):
        for j in range(A_GROUP):
            h = g * A_GROUP + j
            qg_s[g, :, j * DSA_TQ:(j + 1) * DSA_TQ] = qt_ref[h * A_HEAD_DIM:(h + 1) * A_HEAD_DIM, :]

    def score_chunk(c, carry):
        for sub in range(DSA_KC // DSA_KA):
            r0 = pl.multiple_of(c * DSA_KC + sub * DSA_KA, DSA_KA)
            kic = ki_ref[pl.ds(r0, DSA_KA), 0:IDX_DIM]
            logits = _dot(kic, qi_s[...])
            acc = jnp.zeros((DSA_KA, DSA_TQ), jnp.float32)
            for h in range(IDX_HEADS):
                acc = acc + jnp.maximum(logits[:, h * DSA_TQ:(h + 1) * DSA_TQ], 0.0) * wt_ref[h:h + 1, :]
            bits = lax.bitcast_convert_type(acc, jnp.int32)
            okey = bits ^ ((bits >> 31) & jnp.int32(0x7FFFFFFF))
            k_pos = r0 + lax.broadcasted_iota(jnp.int32, (DSA_KA, 1), 0)
            key_s[pl.ds(r0, DSA_KA), :] = jnp.where(k_pos <= q_pos, okey, jnp.int32(INT_MIN))
        return carry

    lax.fori_loop(0, n_chunks, score_chunk, 0)

    def count_keys(cand, strict):
        def body(c, cnt):
            r0 = pl.multiple_of(c * DSA_KC, DSA_KC)
            blk = key_s[pl.ds(r0, DSA_KC), :]
            hit = jnp.where((blk > cand) if strict else (blk >= cand), jnp.int32(1), jnp.int32(0))
            return cnt + jnp.sum(hit.reshape(DSA_KC // SUBLANES, SUBLANES, DSA_TQ), axis=0)
        cnt8 = lax.fori_loop(0, n_chunks, body, jnp.zeros((SUBLANES, DSA_TQ), jnp.int32))
        return jnp.sum(cnt8, axis=0, keepdims=True)

    def bisect(it, u):
        cand_u = u | lax.shift_left(jnp.int32(1), 31 - it)
        cnt = count_keys(cand_u ^ jnp.int32(INT_MIN), False)
        return jnp.where(cnt >= n_top, cand_u, u)

    thr = lax.fori_loop(0, 32, bisect, jnp.zeros((1, DSA_TQ), jnp.int32)) ^ jnp.int32(INT_MIN)
    thr = jnp.maximum(thr, jnp.int32(INT_MIN + 1))
    n_sel = count_keys(thr, False)

    def bias_plain():
        def body(c, carry):
            r0 = pl.multiple_of(c * DSA_KC, DSA_KC)
            sc_s[pl.ds(r0, DSA_KC), :] = jnp.where(key_s[pl.ds(r0, DSA_KC), :] >= thr, 0.0, NEG_BIG)
            return carry
        lax.fori_loop(0, n_chunks, body, 0)

    def bias_ties():
        need = (n_top - count_keys(thr, True)).astype(jnp.float32)
        rr = lax.broadcasted_iota(jnp.int32, (DSA_KC, DSA_KC), 0)
        cc = lax.broadcasted_iota(jnp.int32, (DSA_KC, DSA_KC), 1)
        before = jnp.where(cc < rr, 1.0, 0.0).astype(MXU_DTYPE)

        def body(c, seen):
            r0 = pl.multiple_of(c * DSA_KC, DSA_KC)
            blk = key_s[pl.ds(r0, DSA_KC), :]
            tie = jnp.where(blk == thr, 1.0, 0.0)
            rank = _dot(before, _bf(tie)) + seen
            sel = jnp.logical_or(blk > thr, jnp.logical_and(blk == thr, rank < need))
            sc_s[pl.ds(r0, DSA_KC), :] = jnp.where(sel, 0.0, NEG_BIG)
            return seen + jnp.sum(tie, axis=0, keepdims=True)
        lax.fori_loop(0, n_chunks, body, jnp.zeros((1, DSA_TQ), jnp.float32))

    lax.cond(jnp.max(n_sel) > n_top, bias_ties, bias_plain)

    groups = range(A_KV_HEADS)
    m_s[...] = jnp.full(m_s.shape, NEG_BIG, jnp.float32)
    acc_s[...] = jnp.zeros(acc_s.shape, jnp.float32)

    def attn_chunk(c, carry):
        r0 = pl.multiple_of(c * DSA_KC, DSA_KC)
        bias = sc_s[pl.ds(r0, DSA_KC), :]
        bias_g = jnp.concatenate([bias] * A_GROUP, axis=1)
        s = [_dot(k_ref[pl.ds(r0, DSA_KC), g * A_HEAD_DIM:(g + 1) * A_HEAD_DIM], qg_s[g]) + bias_g for g in groups]
        m_old = [m_s[g] for g in groups]
        m_new = [jnp.maximum(m_old[g], jnp.max(s[g], axis=0, keepdims=True)) for g in groups]
        alpha = [jnp.exp2(m_old[g] - m_new[g]) for g in groups]
        p = [_bf(jnp.exp2(s[g] - m_new[g])) for g in groups]
        pv = [_dot(vt_ref[g * DSA_VROWS:(g + 1) * DSA_VROWS, pl.ds(r0, DSA_KC)], p[g]) for g in groups]
        acc = [alpha[g] * acc_s[g] + pv[g] for g in groups]
        for g in groups:
            acc_s[g] = acc[g]
            m_s[g] = m_new[g]
        return carry

    lax.fori_loop(0, n_chunks, attn_chunk, 0)
    for g in groups:
        acc = acc_s[g]
        o_t = acc[0:A_HEAD_DIM, :] / acc[A_HEAD_DIM:A_HEAD_DIM + 1, :]
        for j in range(A_GROUP):
            h = g * A_GROUP + j
            o_ref[:, h * A_HEAD_DIM:(h + 1) * A_HEAD_DIM] = _bf(o_t[:, j * DSA_TQ:(j + 1) * DSA_TQ].T)


def _dsa_attention(q_t, k_r, v_t, qi_t, ki_r, w_t, n_top):
    L = k_r.shape[0]
    n_q = A_HEADS * A_HEAD_DIM
    n_kv = A_KV_HEADS * A_HEAD_DIM
    n_qi = IDX_HEADS * IDX_DIM
    assert L % DSA_KC == 0
    resident = dict(pipeline_mode=pl.Buffered(1))
    return pl.pallas_call(
        functools.partial(_dsa_kernel, n_top=n_top),
        out_shape=jax.ShapeDtypeStruct((L, n_q), MXU_DTYPE),
        grid=(L // DSA_TQ,),
        in_specs=[
            pl.BlockSpec((n_q, DSA_TQ), lambda i: (0, i)),
            pl.BlockSpec((n_qi, DSA_TQ), lambda i: (0, i)),
            pl.BlockSpec((LANES, DSA_TQ), lambda i: (0, i)),
            pl.BlockSpec((L, LANES), lambda i: (0, 0), **resident),
            pl.BlockSpec((L, n_kv), lambda i: (0, 0), **resident),
            pl.BlockSpec((A_KV_HEADS * DSA_VROWS, L), lambda i: (0, 0), **resident),
        ],
        out_specs=pl.BlockSpec((DSA_TQ, n_q), lambda i: (i, 0)),
        scratch_shapes=[
            pltpu.VMEM((IDX_DIM, IDX_HEADS * DSA_TQ), MXU_DTYPE),
            pltpu.VMEM((A_KV_HEADS, A_HEAD_DIM, A_GROUP * DSA_TQ), MXU_DTYPE),
            pltpu.VMEM((L, DSA_TQ), jnp.float32),
            pltpu.VMEM((L, DSA_TQ), jnp.int32),
            pltpu.VMEM((A_KV_HEADS, 1, A_GROUP * DSA_TQ), jnp.float32),
            pltpu.VMEM((A_KV_HEADS, DSA_VROWS, A_GROUP * DSA_TQ), jnp.float32),
        ],
        compiler_params=_cparams(("arbitrary",)),
        name="dsa_attention",
    )(q_t, qi_t, w_t, ki_r, k_r, v_t)


def _rope_tables(L, n_half, period, theta_pow_dim):
    inv = ROPE_THETA ** (-jnp.arange(0, theta_pow_dim, 2, dtype=jnp.float32) / theta_pow_dim)
    ang = jnp.arange(L, dtype=jnp.float32)[:, None] * inv[None, :]
    cos, sin = jnp.cos(ang), jnp.sin(ang)
    pad = period - 2 * n_half
    tc = jnp.concatenate([cos, cos, jnp.ones((L, pad), jnp.float32)], axis=1)
    s1 = jnp.concatenate([-sin, jnp.zeros((L, period - n_half), jnp.float32)], axis=1)
    s2 = jnp.concatenate([jnp.zeros((L, n_half), jnp.float32), sin, jnp.zeros((L, pad), jnp.float32)], axis=1)
    reps = LANES // period
    return jnp.stack([jnp.tile(t, (1, reps)) for t in (tc, s1, s2)])


RWKV_CHUNK = 64
RWKV_SLAB = 512
RWKV_TB = 256
P_MU_R, P_MU_K, P_MU_V, P_W0, P_A0, P_KK, P_KA, P_RK, P_LNW, P_LNB = range(10)


def _head_ones():
    r = lax.broadcasted_iota(jnp.int32, (LANES, LANES), 0) // RWKV_HEAD
    c = lax.broadcasted_iota(jnp.int32, (LANES, LANES), 1) // RWKV_HEAD
    return jnp.where(r == c, 1.0, 0.0).astype(MXU_DTYPE)


def _head_sum(x, ones_bd):
    tiles = [_dot_exact_rhs(x[:, t * LANES:(t + 1) * LANES], ones_bd) for t in range(x.shape[1] // LANES)]
    return jnp.concatenate(tiles, axis=1)


def _pair_stack(x):
    lane = lax.broadcasted_iota(jnp.int32, x.shape, 1)
    return jnp.concatenate([jnp.where(lane < RWKV_HEAD, x, 0.0), jnp.where(lane >= RWKV_HEAD, x, 0.0)], axis=0)


def _rwkv_kernel(zr_ref, zk_ref, zv_ref, zs_ref, hr_ref, hk_ref, hv_ref, hs_ref, p_ref, mus_ref, lora_ref,
                 o_ref, h_s, r_s, lw_s, k2_s, v_s, kk_s, a_s, g_s, bon_s, oo_s):
    C = RWKV_CHUNK
    tb = zr_ref.shape[0]
    i = pl.program_id(1)

    @pl.when(i == 0)
    def _():
        h_s[...] = jnp.zeros_like(h_s)

    row = lax.broadcasted_iota(jnp.int32, (tb, 1), 0)
    has_prev = jnp.where(i > 0, 1.0, 0.0)

    def mix(z_ref, halo_ref, mu):
        z = z_ref[...]
        prev = halo_ref[SUBLANES - 1:SUBLANES, :] * has_prev
        shifted = jnp.where(row == 0, prev, pltpu.roll(z, 1, 0))
        return z + (shifted - z) * mu

    r = mix(zr_ref, hr_ref, p_ref[P_MU_R:P_MU_R + 1, :])
    k = mix(zk_ref, hk_ref, p_ref[P_MU_K:P_MU_K + 1, :])
    v = mix(zv_ref, hv_ref, p_ref[P_MU_V:P_MU_V + 1, :])
    sm = mix(zs_ref, hs_ref, mus_ref[...])
    wd, ad, gd = sm[:, 0:LANES], sm[:, LANES:2 * LANES], sm[:, 2 * LANES:4 * LANES]
    x = -(p_ref[P_W0:P_W0 + 1, :] + _dot_x3(jnp.tanh(wd), lora_ref[0:LANES, :]))
    softplus = jnp.maximum(x, 0.0) + jnp.log(1.0 + jnp.exp(-jnp.abs(x)))
    lw = -jnp.exp(-softplus - 0.5)
    a = _sigmoid(p_ref[P_A0:P_A0 + 1, :] + _dot_x3(ad, lora_ref[LANES:2 * LANES, :]))
    g_s[...] = _dot_x3(_sigmoid(gd), lora_ref[2 * LANES:4 * LANES, :])
    ones_bd = _head_ones()
    kk = k * p_ref[P_KK:P_KK + 1, :]
    kk = kk * lax.rsqrt(jnp.maximum(_head_sum(kk * kk, ones_bd), 1e-24))
    k2 = k * (1.0 + (a - 1.0) * p_ref[P_KA:P_KA + 1, :])
    bon_s[...] = _head_sum(r * k2 * p_ref[P_RK:P_RK + 1, :], ones_bd)
    r_s[...] = r
    lw_s[...] = lw
    k2_s[...] = k2
    v_s[...] = v
    kk_s[...] = kk
    a_s[...] = a

    n2 = 2 * C
    rr = lax.broadcasted_iota(jnp.int32, (n2, n2), 0)
    cc = lax.broadcasted_iota(jnp.int32, (n2, n2), 1)
    strict = (cc % C) < (rr % C)
    incl = (cc % C) <= (rr % C)
    eye = jnp.where(rr == cc, 1.0, 0.0)
    tril_c = jnp.where(lax.broadcasted_iota(jnp.int32, (C, C), 1) <= lax.broadcasted_iota(jnp.int32, (C, C), 0),
                       1.0, 0.0).astype(MXU_DTYPE)

    n_pairs = RWKV_SLAB // LANES
    pairs = range(n_pairs)

    def chunk(c, carry):
        r0 = pl.multiple_of(c * C, C)

        def ld(ref):
            return [ref[pl.ds(r0, C), p * LANES:(p + 1) * LANES] for p in pairs]

        rc, lwc, kc, vc, kkc, ac = ld(r_s), ld(lw_s), ld(k2_s), ld(v_s), ld(kk_s), ld(a_s)
        h_old = [h_s[p] for p in pairs]
        cum = []
        for p in pairs:
            l1, l2, l3 = _split3(lwc[p])
            cum.append(_dot(tril_c, l1) + _dot(tril_c, l2) + _dot(tril_c, l3))
        cl = [cum[p][C - 1:C, :] for p in pairs]
        e_neg = [jnp.exp(-cum[p]) for p in pairs]
        e_end = [jnp.exp(cl[p] - cum[p]) for p in pairs]
        beta = [kkc[p] * ac[p] for p in pairs]
        ae_s = [_pair_stack(-kkc[p] * jnp.exp(cum[p] - lwc[p])) for p in pairs]
        rp_s = [_pair_stack(rc[p] * jnp.exp(cum[p])) for p in pairs]
        bm_s = [_pair_stack(beta[p] * e_neg[p]) for p in pairs]
        km_s = [_pair_stack(kc[p] * e_neg[p]) for p in pairs]
        bt_t = [_bf(_pair_stack(beta[p] * e_end[p]).T) for p in pairs]
        kt_t = [_bf(_pair_stack(kc[p] * e_end[p]).T) for p in pairs]
        v_st = [_bf(_pair_stack(vc[p])) for p in pairs]
        aa = [_dot_nt(_bf(jnp.concatenate([ae_s[p], rp_s[p]], axis=0)),
                      _bf(jnp.concatenate([bm_s[p], km_s[p]], axis=0))) for p in pairs]
        a_ab = [jnp.where(strict, aa[p][0:n2, 0:n2], 0.0) for p in pairs]
        a_ak = [_bf(jnp.where(strict, aa[p][0:n2, n2:2 * n2], 0.0)) for p in pairs]
        a_rb = [_bf(jnp.where(incl, aa[p][n2:2 * n2, 0:n2], 0.0)) for p in pairs]
        a_rk = [_bf(jnp.where(incl, aa[p][n2:2 * n2, n2:2 * n2], 0.0)) for p in pairs]
        t_inv = [eye + a_ab[p] for p in pairs]
        n_pow = a_ab
        for _ in range(int(math.log2(C)) - 1):
            n_b = [_bf(n_pow[p]) for p in pairs]
            n_pow = [_dot(n_b[p], n_b[p]) for p in pairs]
            t_inv = [t_inv[p] + _dot(_bf(t_inv[p]), _bf(n_pow[p])) for p in pairs]
        t_b = [_bf(t_inv[p]) for p in pairs]
        a_til = [_bf(_dot(t_b[p], _bf(ae_s[p]))) for p in pairs]
        w_k = [_bf(_dot(a_ak[p], v_st[p])) for p in pairs]
        v_til = [_dot(t_b[p], w_k[p]) for p in pairs]
        o_intra = [_dot(a_rk[p], v_st[p]) for p in pairs]
        h_kv = [_dot(kt_t[p], v_st[p]) for p in pairs]
        decay_col = [jnp.exp(jnp.broadcast_to(cl[p], (LANES, LANES)).T) for p in pairs]
        h_b = [_bf(h_old[p]) for p in pairs]
        u_b = [_bf(_dot(a_til[p], h_b[p]) + v_til[p]) for p in pairs]
        o_st = [_dot(_bf(rp_s[p]), h_b[p]) + _dot(a_rb[p], u_b[p]) + o_intra[p] for p in pairs]
        h_new = [decay_col[p] * h_old[p] + _dot(bt_t[p], u_b[p]) + h_kv[p] for p in pairs]
        for p in pairs:
            h_s[p] = h_new[p]
            oo_s[pl.ds(r0, C), p * LANES:(p + 1) * LANES] = o_st[p][0:C, :] + o_st[p][C:n2, :]
        return carry

    lax.fori_loop(0, tb // C, chunk, 0)

    o = oo_s[...]
    mean = _head_sum(o, ones_bd) * (1.0 / RWKV_HEAD)
    d = o - mean
    var = _head_sum(d * d, ones_bd) * (1.0 / RWKV_HEAD)
    y = d * lax.rsqrt(var + RWKV_GN_EPS) * p_ref[P_LNW:P_LNW + 1, :] + p_ref[P_LNB:P_LNB + 1, :]
    y = y + bon_s[...] * v_s[...]
    o_ref[...] = (y * g_s[...]).astype(o_ref.dtype)


def _rwkv_params(mu, w0, w_up, a0, a_up, g_up, k_k, k_a, r_k, ln_w, ln_b):
    d = RWKV_DIM
    mu_r, mu_k, mu_v = mu[0:d], mu[d:2 * d], mu[2 * d:3 * d]
    o = 3 * d
    mu_wd, mu_ad, mu_gd = mu[o:o + DECAY_RANK], mu[o + DECAY_RANK:o + DECAY_RANK + AAA_RANK], mu[o + DECAY_RANK + AAA_RANK:]
    rows = [mu_r, mu_k, mu_v, w0, a0, k_k, k_a, r_k.reshape(d), ln_w, ln_b]
    p_rows = jnp.concatenate([jnp.stack(rows), jnp.zeros((16 - len(rows), d), jnp.float32)], axis=0)

    def pad_to(x, n, axis):
        widths = [(0, 0)] * x.ndim
        widths[axis] = (0, n - x.shape[axis])
        return jnp.pad(x, widths)

    mu_small = jnp.concatenate([pad_to(mu_wd, LANES, 0), pad_to(mu_ad, LANES, 0), mu_gd]).reshape(1, 4 * LANES)
    lora = jnp.concatenate([pad_to(w_up, LANES, 0), pad_to(a_up, LANES, 0), g_up], axis=0)
    return p_rows, mu_small, lora


def _rwkv_mix(z, p_rows, mu_small, lora):
    L = z.shape[0]
    tb, w = RWKV_TB, RWKV_SLAB
    assert L % tb == 0
    hb = tb // SUBLANES

    def blk(col0):
        return pl.BlockSpec((tb, w), lambda s, i: (i, col0 // w + s))

    def halo(col0):
        return pl.BlockSpec((SUBLANES, w), lambda s, i: (jnp.maximum(i * hb - 1, 0), col0 // w + s))

    small_w = 4 * LANES
    f32 = jnp.float32
    return pl.pallas_call(
        _rwkv_kernel,
        out_shape=jax.ShapeDtypeStruct((L, RWKV_DIM), MXU_DTYPE),
        grid=(RWKV_DIM // w, L // tb),
        in_specs=[
            blk(ZE_R), blk(ZE_RK), blk(ZE_RV),
            pl.BlockSpec((tb, small_w), lambda s, i: (i, ZE_WD // small_w)),
            halo(ZE_R), halo(ZE_RK), halo(ZE_RV),
            pl.BlockSpec((SUBLANES, small_w), lambda s, i: (jnp.maximum(i * hb - 1, 0), ZE_WD // small_w)),
            pl.BlockSpec((16, w), lambda s, i: (0, s)),
            pl.BlockSpec((1, small_w), lambda s, i: (0, 0)),
            pl.BlockSpec((small_w, w), lambda s, i: (0, s)),
        ],
        out_specs=pl.BlockSpec((tb, w), lambda s, i: (i, s)),
        scratch_shapes=[pltpu.VMEM((w // LANES, LANES, LANES), f32)] + [pltpu.VMEM((tb, w), f32) for _ in range(9)],
        compiler_params=_cparams(("parallel", "arbitrary")),
        name="rwkv7_mix",
    )(z, z, z, z, z, z, z, z, p_rows, mu_small, lora)


S5_TB = 512
S5_SEG = SUBLANES
S5_SLAB_GROUPS = LANES // S5_GROUP
S5_SW = S5_SLAB_GROUPS * S5_STATE


def _cmul_add(ar, ai, xr, xi, br, bi):
    return ar * xr - ai * xi + br, ar * xi + ai * xr + bi


def _s5_kernel(u_ref, w_ref, c_ref, apow_ref, d_ref, zg_ref, zgb_ref, st_s, up_s, bu_s, x_s, y_s):
    tb = u_ref.shape[0]
    ts = tb // S5_SEG
    sw = S5_SW
    i = pl.program_id(1)

    @pl.when(i == 0)
    def _():
        st_s[...] = jnp.zeros_like(st_s)

    for tau in range(ts):
        up_s[tau * S5_SEG:(tau + 1) * S5_SEG, :] = u_ref[pl.ds(tau, S5_SEG, stride=ts), :]
    bu_s[...] = _dot(_bf(up_s[...]), _bf(w_ref[0]))
    a1 = apow_ref[0, 0:1, :]
    ar = jnp.broadcast_to(a1[:, 0:sw], (S5_SEG, sw))
    ai = jnp.broadcast_to(a1[:, sw:2 * sw], (S5_SEG, sw))

    def scan(tau, x):
        r0 = pl.multiple_of(tau * S5_SEG, S5_SEG)
        b = bu_s[pl.ds(r0, S5_SEG), :]
        nr, ni = _cmul_add(ar, ai, x[0], x[1], b[:, 0:sw], b[:, sw:2 * sw])
        x_s[pl.ds(r0, S5_SEG), :] = jnp.concatenate([nr, ni], axis=1)
        return nr, ni

    zero = jnp.zeros((S5_SEG, sw), jnp.float32)
    xr, xi = lax.fori_loop(0, ts, scan, (zero, zero), unroll=4)

    a_ts = apow_ref[0, ts - 1:ts, :]
    tr, ti = a_ts[:, 0:sw], a_ts[:, sw:2 * sw]
    cr, ci = st_s[:, 0:sw], st_s[:, sw:2 * sw]
    ent_r, ent_i = [], []
    for s in range(S5_SEG):
        ent_r.append(cr)
        ent_i.append(ci)
        cr, ci = _cmul_add(tr, ti, cr, ci, xr[s:s + 1, :], xi[s:s + 1, :])
    st_s[...] = jnp.concatenate([cr, ci], axis=1)
    er = jnp.concatenate(ent_r, axis=0)
    ei = jnp.concatenate(ent_i, axis=0)

    def fix(tau, carry):
        r0 = pl.multiple_of(tau * S5_SEG, S5_SEG)
        ap = apow_ref[0, pl.ds(tau, 1), :]
        x = x_s[pl.ds(r0, S5_SEG), :]
        nr, ni = _cmul_add(ap[:, 0:sw], ap[:, sw:2 * sw], er, ei, x[:, 0:sw], x[:, sw:2 * sw])
        x_s[pl.ds(r0, S5_SEG), :] = jnp.concatenate([nr, ni], axis=1)
        return carry

    lax.fori_loop(0, ts, fix, 0, unroll=4)
    yp = _dot(_bf(x_s[...]), _bf(c_ref[0]))
    for tau in range(ts):
        y_s[pl.ds(tau, S5_SEG, stride=ts), :] = yp[tau * S5_SEG:(tau + 1) * S5_SEG, :]
    y = y_s[...] + d_ref[...] * u_ref[...]
    zg = 0.5 * y * (1.0 + jnp.tanh(math.sqrt(2.0 / math.pi) * (y + 0.044715 * (y * y * y))))
    zg_ref[...] = zg
    zgb_ref[...] = zg.astype(zgb_ref.dtype)


def _s5_params(lam_re, lam_im, log_step, b_re, b_im, c_re, c_im, n_pow):
    lr = jnp.minimum(lam_re, -1e-4)
    li = lam_im
    step = jnp.exp(log_step)[:, None]
    mag = jnp.exp(lr * step)
    abar_r = mag * jnp.cos(li * step)
    abar_i = mag * jnp.sin(li * step)
    den = lr * lr + li * li
    cr = (lr * (abar_r - 1.0) + li * abar_i) / den
    ci = (lr * abar_i - li * (abar_r - 1.0)) / den
    bbar_r = cr[..., None] * b_re - ci[..., None] * b_im
    bbar_i = cr[..., None] * b_im + ci[..., None] * b_re
    ns = S5_GROUPS // S5_SLAB_GROUPS
    eye = jnp.eye(S5_SLAB_GROUPS, dtype=jnp.float32)

    def in_mat(bb):
        t = jnp.einsum('ab,sapi->saibp', eye, bb.reshape(ns, S5_SLAB_GROUPS, S5_STATE, S5_GROUP))
        return t.reshape(ns, LANES, S5_SW)

    def out_mat(cc):
        t = jnp.einsum('ab,saop->sapbo', eye, cc.reshape(ns, S5_SLAB_GROUPS, S5_GROUP, S5_STATE))
        return t.reshape(ns, S5_SW, LANES)

    w_in = jnp.concatenate([in_mat(bbar_r), in_mat(bbar_i)], axis=2)
    w_out = jnp.concatenate([out_mat(c_re), -out_mat(c_im)], axis=1)
    n = jnp.arange(1, n_pow + 1, dtype=jnp.float32)[None, :, None]
    lrs = (lr * step).reshape(ns, 1, S5_SW)
    lis = (li * step).reshape(ns, 1, S5_SW)
    pm = jnp.exp(n * lrs)
    apow = jnp.concatenate([pm * jnp.cos(n * lis), pm * jnp.sin(n * lis)], axis=2)
    return _bf(w_in), _bf(w_out), apow


def _s5_mix(z, w_in, w_out, apow, d_skip):
    L = z.shape[0]
    tb = S5_TB
    assert L % tb == 0
    ns = S5_DIM // LANES
    f32 = jnp.float32
    return pl.pallas_call(
        _s5_kernel,
        out_shape=(jax.ShapeDtypeStruct((L, S5_DIM), f32), jax.ShapeDtypeStruct((L, S5_DIM), MXU_DTYPE)),
        grid=(ns, L // tb),
        in_specs=[
            pl.BlockSpec((tb, LANES), lambda s, i: (i, s)),
            pl.BlockSpec((1, LANES, 2 * S5_SW), lambda s, i: (s, 0, 0)),
            pl.BlockSpec((1, 2 * S5_SW, LANES), lambda s, i: (s, 0, 0)),
            pl.BlockSpec((1, tb // S5_SEG, 2 * S5_SW), lambda s, i: (s, 0, 0)),
            pl.BlockSpec((1, LANES), lambda s, i: (0, s)),
        ],
        out_specs=(pl.BlockSpec((tb, LANES), lambda s, i: (i, s)), pl.BlockSpec((tb, LANES), lambda s, i: (i, s))),
        scratch_shapes=[
            pltpu.VMEM((1, 2 * S5_SW), f32),
            pltpu.VMEM((tb, LANES), f32),
            pltpu.VMEM((tb, 2 * S5_SW), f32),
            pltpu.VMEM((tb, 2 * S5_SW), f32),
            pltpu.VMEM((tb, LANES), f32),
        ],
        compiler_params=_cparams(("parallel", "arbitrary")),
        name="s5_mix",
    )(z, w_in, w_out, apow, d_skip.reshape(1, S5_DIM))


RET_CHUNK = 256


def _ret_kernel(q_ref, k_ref, v_ref, g_ref, cos_ref, sin_ref, intra_ref, rowdec_ref, o_ref, s_s):
    c = pl.program_id(1)

    @pl.when(c == 0)
    def _():
        s_s[...] = jnp.zeros_like(s_s)

    half = RET_HEAD_DIM // 2
    cos, sin = cos_ref[...], sin_ref[...]

    def rot(x):
        x1, x2 = x[:, 0:half], x[:, half:2 * half]
        return jnp.concatenate([x1 * cos - x2 * sin, x1 * sin + x2 * cos], axis=1)

    q = rot(q_ref[...])
    k = rot(k_ref[...]) * (RET_HEAD_DIM ** -0.5)
    vb = _bf(v_ref[...])
    qb = _bf(q)
    dec = rowdec_ref[0]

    def lanes2(x):
        return jnp.concatenate([x, x], axis=1)

    xi, zeta, g_chunk = lanes2(dec[:, 0:LANES]), lanes2(dec[:, LANES:2 * LANES]), lanes2(dec[:, 2 * LANES:3 * LANES])
    att = _dot_nt(qb, _bf(k)) * intra_ref[0]
    s_old = s_s[...]
    o = _dot(_bf(att), vb) + _dot(qb, _bf(s_old)) * xi
    s_s[...] = s_old * g_chunk + _dot(_bf((k * zeta).T), vb)
    mean = jnp.mean(o, axis=-1, keepdims=True)
    d = o - mean
    var = jnp.mean(d * d, axis=-1, keepdims=True)
    gate = g_ref[...]
    o_ref[...] = (gate * _sigmoid(gate) * (d * lax.rsqrt(var + RET_GN_EPS))).astype(o_ref.dtype)


def _ret_tables(L):
    C = RET_CHUNK
    inv = 1.0 / (RET_ROPE_BASE ** jnp.linspace(0.0, 1.0, RET_HEAD_DIM // 2, dtype=jnp.float32))
    ang = jnp.arange(L, dtype=jnp.float32)[:, None] * inv[None, :]
    log_g = jnp.log(1.0 - 2.0 ** (-5.0 - jnp.arange(RET_HEADS, dtype=jnp.float32)))
    pos = jnp.arange(C, dtype=jnp.float32)
    diff = pos[:, None] - pos[None, :]
    intra = jnp.where(diff >= 0, jnp.exp(jnp.maximum(diff, 0.0)[None] * log_g[:, None, None]), 0.0)
    xi = jnp.exp((pos + 1.0)[None, :] * log_g[:, None])
    zeta = jnp.exp((C - 1.0 - pos)[None, :] * log_g[:, None])
    g_chunk = jnp.broadcast_to(jnp.exp(C * log_g)[:, None], (RET_HEADS, C))
    rowdec = jnp.concatenate([jnp.broadcast_to(t[:, :, None], (RET_HEADS, C, LANES)) for t in (xi, zeta, g_chunk)], axis=2)
    return jnp.cos(ang), jnp.sin(ang), intra, rowdec


def _retention_mix(z, cos, sin, intra, rowdec):
    L = z.shape[0]
    C, hd = RET_CHUNK, RET_HEAD_DIM
    assert L % C == 0
    base = S5_DIM // hd

    def blk(j):
        return pl.BlockSpec((C, hd), lambda h, c: (c, base + j * RET_HEADS + h))

    return pl.pallas_call(
        _ret_kernel,
        out_shape=jax.ShapeDtypeStruct((L, RET_HEADS * hd), MXU_DTYPE),
        grid=(RET_HEADS, L // C),
        in_specs=[
            blk(0), blk(1), blk(2), blk(3),
            pl.BlockSpec((C, hd // 2), lambda h, c: (c, 0)),
            pl.BlockSpec((C, hd // 2), lambda h, c: (c, 0)),
            pl.BlockSpec((1, C, C), lambda h, c: (h, 0, 0)),
            pl.BlockSpec((1, C, 3 * LANES), lambda h, c: (h, 0, 0)),
        ],
        out_specs=pl.BlockSpec((C, hd), lambda h, c: (c, h)),
        scratch_shapes=[pltpu.VMEM((hd, hd), jnp.float32)],
        compiler_params=_cparams(("parallel", "arbitrary")),
        name="retention_mix",
    )(z, z, z, z, cos, sin, intra, rowdec)


def _even_w_in(w):
    d = w.shape[0]
    n_attn = ZE_R
    o_ki = n_attn
    o_wi = o_ki + IDX_DIM
    o_rw = o_wi + IDX_HEADS
    o_wd = o_rw + 3 * RWKV_DIM
    o_ad = o_wd + DECAY_RANK
    o_gd = o_ad + AAA_RANK

    def zeros(n):
        return jnp.zeros((d, n), w.dtype)

    ki = w[:, o_ki:o_wi]
    parts = [
        w[:, 0:n_attn], w[:, o_rw:o_wd],
        ki, ki, w[:, o_wi:o_rw], zeros(ZE_WD - ZE_WI - IDX_HEADS),
        w[:, o_wd:o_ad], zeros(LANES - DECAY_RANK),
        w[:, o_ad:o_gd], zeros(LANES - AAA_RANK),
        w[:, o_gd:o_gd + GATE_RANK],
    ]
    out = jnp.concatenate(parts, axis=1).astype(MXU_DTYPE)
    assert out.shape[1] == ZE_COLS
    return out


def _ffn(h, norm_g, w_gate, w_up, w_down_b, layer):
    hn = _rmsnorm(h, norm_g, MXU_DTYPE)
    mid = _matmul([hn], [(w_gate, layer, 0), (w_up, layer, 0)], [(0, 0), (0, 1)], [], _ep_swiglu, MXU_DTYPE,
                  tm=1024, tn=256, name="ffn_gate_up")
    return _matmul([mid], [(w_down_b, layer, 0)], [(0, 0)], [("tile", h)], _ep_resid, jnp.float32,
                   tm=1024, tn=512, tk=w_down_b.shape[1] // 2, name="ffn_down")


def _out_proj(h, o_first, o_second, w_out, layer):
    return _matmul([o_first, o_second], [(w_out, layer, 0), (w_out, layer, 1)], [(0, 0), (1, 1)], [("tile", h)],
                   _ep_resid2, jnp.float32, tm=1024, tn=512, name="out_proj")


def kernel(x, norm_mix, norm_ffn, ffn_gate, ffn_up, ffn_down, e_w_in, e_w_out, e_mu, e_w0, e_w_up, e_a0, e_a_up, e_g_up, e_k_k, e_k_a, e_r_k, e_ln_w, e_ln_b, o_w_in, o_w_out, o_lam_re, o_lam_im, o_log_step, o_b_re, o_b_im, o_c_re, o_c_im, o_d_skip, o_w_glu, o_b_glu, final_norm):
    assert x.shape[0] == 1
    h = x[0]
    L = h.shape[0]
    n_top = min(TOPK_MAX, L // 4)

    ffn_down_b = _bf(ffn_down)
    hn = _rmsnorm(h, norm_mix[0], MXU_DTYPE)
    z = _matmul([hn], [(_even_w_in(e_w_in[0]), 0, 0)], [(0, 0)], [], _ep_plain, jnp.float32, tm=1024, tn=1024,
                name="even_in_proj")
    tab_a = _rope_tables(L, A_ROT_HALF, A_HEAD_DIM, 2 * A_ROT_HALF)
    tab_i = _rope_tables(L, IDX_ROT_HALF, IDX_DIM, 2 * IDX_ROT_HALF)
    q_t, k_r, v_t, qi_t, ki_r, w_t = _dsa_prep(z, tab_a, tab_i)
    o_a = _dsa_attention(q_t, k_r, v_t, qi_t, ki_r, w_t, n_top)
    p_rows, mu_small, lora = _rwkv_params(e_mu[0], e_w0[0], e_w_up[0], e_a0[0], e_a_up[0], e_g_up[0], e_k_k[0],
                                          e_k_a[0], e_r_k[0], e_ln_w[0], e_ln_b[0])
    o_b = _rwkv_mix(z, p_rows, mu_small, lora)
    h = _out_proj(h, o_a, o_b, e_w_out, 0)
    h = _ffn(h, norm_ffn[0], ffn_gate, ffn_up, ffn_down_b, 0)

    hn = _rmsnorm(h, norm_mix[1], MXU_DTYPE)
    z = _matmul([hn], [(o_w_in, 0, 0)], [(0, 0)], [], _ep_plain, jnp.float32, tm=1024, tn=512, name="odd_in_proj")
    s5_in, s5_out, s5_apow = _s5_params(o_lam_re[0], o_lam_im[0], o_log_step[0], o_b_re[0], o_b_im[0], o_c_re[0],
                                        o_c_im[0], S5_TB // S5_SEG)
    zg, zg_b = _s5_mix(z, s5_in, s5_out, s5_apow, o_d_skip[0])
    o_c = _matmul([zg_b], [(o_w_glu, 0, 0)], [(0, 0)], [("tile", zg), ("row", o_b_glu[0].reshape(1, S5_DIM))], _ep_glu,
                  MXU_DTYPE, tm=1024, tn=512, name="s5_glu")
    o_d = _retention_mix(z, *_ret_tables(L))
    h = _out_proj(h, o_c, o_d, o_w_out, 0)
    h = _ffn(h, norm_ffn[1], ffn_gate, ffn_up, ffn_down_b, 1)

    return _rmsnorm(h, final_norm, jnp.float32)[None]
```

```python
import functools
import math

import jax
import jax.numpy as jnp
from jax import lax
from jax.experimental import pallas as pl
from jax.experimental.pallas import tpu as pltpu

NORM_EPS = 1e-6
A_HEAD_DIM = 128
A_HEADS = 16
A_KV_HEADS = 4
A_GROUP = A_HEADS // A_KV_HEADS
A_ROT_HALF = 16
IDX_HEADS = 32
IDX_DIM = 64
IDX_ROT_HALF = 8
TOPK_MAX = 256
ROPE_THETA = 500000.0
RWKV_HEAD = 64
RWKV_DIM = 2048
RWKV_HEADS = 32
DECAY_RANK = 96
AAA_RANK = 96
GATE_RANK = 256
RWKV_GN_EPS = 1e-5 * RWKV_HEAD
S5_DIM = 2048
S5_GROUP = 16
S5_GROUPS = 128
S5_STATE = 64
RET_HEAD_DIM = 256
RET_HEADS = 8
RET_ROPE_BASE = 10000.0
RET_GN_EPS = 1e-5

LANES = 128
SUBLANES = 8
VMEM_LIMIT_BYTES = 56 * 2**20

MXU_DTYPE = jnp.bfloat16
NEG_BIG = -1e30
INT_MIN = -2**31

ZE_Q, ZE_K, ZE_V, ZE_QI = 0, 2048, 2560, 3072
ZE_R, ZE_RK, ZE_RV = 5120, 7168, 9216
ZE_KI2, ZE_WI, ZE_WD, ZE_AD, ZE_GD = 11264, 11392, 11776, 11904, 12032
ZE_COLS = 12288


def _cparams(sem):
    return pltpu.CompilerParams(dimension_semantics=sem, vmem_limit_bytes=VMEM_LIMIT_BYTES)


def _bf(x):
    return x.astype(MXU_DTYPE)


def _dot(a, b):
    return jnp.dot(a, b, preferred_element_type=jnp.float32)


def _dot_nt(a, b):
    return lax.dot_general(a, b, (((1,), (1,)), ((), ())), preferred_element_type=jnp.float32)


def _split3(x):
    h1 = _bf(x)
    r1 = x - h1.astype(jnp.float32)
    h2 = _bf(r1)
    h3 = _bf(r1 - h2.astype(jnp.float32))
    return h1, h2, h3


def _dot_x3(a, b):
    ah = _bf(a)
    al = _bf(a - ah.astype(jnp.float32))
    bh = _bf(b)
    bl = _bf(b - bh.astype(jnp.float32))
    return _dot(ah, bh) + _dot(ah, bl) + _dot(al, bh)


def _dot_exact_rhs(a, b_bf):
    h1, h2, h3 = _split3(a)
    return _dot(h1, b_bf) + _dot(h2, b_bf) + _dot(h3, b_bf)


def _sigmoid(x):
    return 1.0 / (1.0 + jnp.exp(-x))


def _rmsnorm_kernel(x_ref, g_ref, o_ref):
    x = x_ref[...]
    ms = jnp.mean(x * x, axis=-1, keepdims=True)
    o_ref[...] = (x * lax.rsqrt(ms + NORM_EPS) * g_ref[...]).astype(o_ref.dtype)


def _rmsnorm(x, g, out_dtype, tm=256):
    m, d = x.shape
    return pl.pallas_call(
        _rmsnorm_kernel,
        out_shape=jax.ShapeDtypeStruct((m, d), out_dtype),
        grid=(m // tm,),
        in_specs=[pl.BlockSpec((tm, d), lambda i: (i, 0)), pl.BlockSpec((1, d), lambda i: (0, 0))],
        out_specs=pl.BlockSpec((tm, d), lambda i: (i, 0)),
        compiler_params=_cparams(("parallel",)),
        name="rmsnorm",
    )(x, g.reshape(1, d))


def _mm_kernel(*refs, n_a, n_w, n_e, pairs, epilogue, nk):
    a_refs = refs[:n_a]
    w_refs = refs[n_a:n_a + n_w]
    e_refs = refs[n_a + n_w:n_a + n_w + n_e]
    o_ref = refs[n_a + n_w + n_e]
    acc_refs = refs[n_a + n_w + n_e + 1:]
    if nk == 1:
        accs = [_dot(a_refs[i][...], _bf(w_refs[j][...])) for i, j in pairs]
        o_ref[...] = epilogue(accs, [e[...] for e in e_refs]).astype(o_ref.dtype)
        return
    k = pl.program_id(2)

    @pl.when(k == 0)
    def _():
        for acc in acc_refs:
            acc[...] = jnp.zeros_like(acc)

    for acc, (i, j) in zip(acc_refs, pairs):
        acc[...] += _dot(a_refs[i][...], _bf(w_refs[j][...]))

    @pl.when(k == nk - 1)
    def _():
        o_ref[...] = epilogue([acc[...] for acc in acc_refs], [e[...] for e in e_refs]).astype(o_ref.dtype)


def _weight_spec(w, layer, k_off, tk, tn):
    if w.ndim == 3:
        return pl.BlockSpec((None, tk, tn), lambda i, j, k: (layer, k + k_off, j))
    return pl.BlockSpec((tk, tn), lambda i, j, k: (k + k_off, j))


def _matmul(a_list, w_list, pairs, extras, epilogue, out_dtype, tm, tn, tk=None, name="matmul"):
    m, kdim = a_list[0].shape
    n = w_list[0][0].shape[-1]
    tk = kdim if tk is None else tk
    tm = min(tm, m)
    nk = kdim // tk
    assert m % tm == 0 and n % tn == 0 and kdim % tk == 0
    in_specs = [pl.BlockSpec((tm, tk), lambda i, j, k: (i, k)) for _ in a_list]
    in_specs += [_weight_spec(w, layer, k_off, tk, tn) for w, layer, k_off in w_list]
    w_list = [w for w, _, _ in w_list]
    e_arrays = []
    for kind, arr in extras:
        e_arrays.append(arr)
        if kind == "tile":
            in_specs.append(pl.BlockSpec((tm, tn), lambda i, j, k: (i, j)))
        else:
            in_specs.append(pl.BlockSpec((1, tn), lambda i, j, k: (0, j)))
    scratch = [] if nk == 1 else [pltpu.VMEM((tm, tn), jnp.float32) for _ in pairs]
    kern = functools.partial(_mm_kernel, n_a=len(a_list), n_w=len(w_list), n_e=len(extras),
                             pairs=tuple(pairs), epilogue=epilogue, nk=nk)
    return pl.pallas_call(
        kern,
        out_shape=jax.ShapeDtypeStruct((m, n), out_dtype),
        grid=(m // tm, n // tn, nk),
        in_specs=in_specs,
        out_specs=pl.BlockSpec((tm, tn), lambda i, j, k: (i, j)),
        scratch_shapes=scratch,
        compiler_params=_cparams(("parallel", "parallel", "arbitrary")),
        name=name,
    )(*a_list, *w_list, *e_arrays)


def _ep_plain(accs, ex):
    return accs[0]


def _ep_swiglu(accs, ex):
    g = accs[0]
    return g * _sigmoid(g) * accs[1]


def _ep_resid(accs, ex):
    return ex[0] + accs[0]


def _ep_resid2(accs, ex):
    return ex[0] + (accs[0] + accs[1])


def _ep_glu(accs, ex):
    zg, b = ex
    return zg * _sigmoid(accs[0] + b)


def _rope_tile(x, tc, s1, s2, n_half):
    return x * tc + pltpu.roll(x, n_half, 1) * s2 + pltpu.roll(x, LANES - n_half, 1) * s1


DSA_VPAD = 16
DSA_VROWS = A_HEAD_DIM + DSA_VPAD


def _dsa_prep_kernel(zq_ref, zs_ref, ta_ref, ti_ref, qt_ref, k_ref, vt_ref, qit_ref, ki_ref, wt_ref):
    ta_c, ta_1, ta_2 = ta_ref[0], ta_ref[1], ta_ref[2]
    ti_c, ti_1, ti_2 = ti_ref[0], ti_ref[1], ti_ref[2]
    q_scale = A_HEAD_DIM ** -0.5 * math.log2(math.e)
    for h in range(A_HEADS):
        x = zq_ref[:, ZE_Q + h * LANES:ZE_Q + (h + 1) * LANES]
        y = _rope_tile(x, ta_c, ta_1, ta_2, A_ROT_HALF) * q_scale
        qt_ref[h * LANES:(h + 1) * LANES, :] = _bf(y.T)
    for g in range(A_KV_HEADS):
        x = zq_ref[:, ZE_K + g * LANES:ZE_K + (g + 1) * LANES]
        k_ref[:, g * LANES:(g + 1) * LANES] = _bf(_rope_tile(x, ta_c, ta_1, ta_2, A_ROT_HALF))
        v = zq_ref[:, ZE_V + g * LANES:ZE_V + (g + 1) * LANES]
        vt_ref[g * DSA_VROWS:g * DSA_VROWS + A_HEAD_DIM, :] = _bf(v.T)
        vt_ref[g * DSA_VROWS + A_HEAD_DIM:(g + 1) * DSA_VROWS, :] = jnp.ones((DSA_VPAD, v.shape[0]), MXU_DTYPE)
    for c in range(IDX_HEADS * IDX_DIM // LANES):
        x = zq_ref[:, ZE_QI + c * LANES:ZE_QI + (c + 1) * LANES]
        y = _rope_tile(x, ti_c, ti_1, ti_2, IDX_ROT_HALF)
        qit_ref[c * LANES:(c + 1) * LANES, :] = _bf(y.T)
    ki = _rope_tile(zs_ref[:, 0:LANES], ti_c, ti_1, ti_2, IDX_ROT_HALF)
    ki_ref[...] = _bf(ki)
    w_scale = (IDX_DIM ** -0.5) * (IDX_HEADS ** -0.5)
    wt_ref[...] = (zs_ref[:, LANES:2 * LANES] * w_scale).T


def _dsa_prep(z, tab_a, tab_i, tm=256):
    L = z.shape[0]
    n_q = A_HEADS * A_HEAD_DIM
    n_kv = A_KV_HEADS * A_HEAD_DIM
    n_qi = IDX_HEADS * IDX_DIM
    out_shape = (
        jax.ShapeDtypeStruct((n_q, L), MXU_DTYPE),
        jax.ShapeDtypeStruct((L, n_kv), MXU_DTYPE),
        jax.ShapeDtypeStruct((A_KV_HEADS * DSA_VROWS, L), MXU_DTYPE),
        jax.ShapeDtypeStruct((n_qi, L), MXU_DTYPE),
        jax.ShapeDtypeStruct((L, LANES), MXU_DTYPE),
        jax.ShapeDtypeStruct((LANES, L), jnp.float32),
    )
    return pl.pallas_call(
        _dsa_prep_kernel,
        out_shape=out_shape,
        grid=(L // tm,),
        in_specs=[
            pl.BlockSpec((tm, ZE_R), lambda i: (i, 0)),
            pl.BlockSpec((tm, 2 * LANES), lambda i: (i, ZE_KI2 // (2 * LANES))),
            pl.BlockSpec((3, tm, LANES), lambda i: (0, i, 0)),
            pl.BlockSpec((3, tm, LANES), lambda i: (0, i, 0)),
        ],
        out_specs=(
            pl.BlockSpec((n_q, tm), lambda i: (0, i)),
            pl.BlockSpec((tm, n_kv), lambda i: (i, 0)),
            pl.BlockSpec((A_KV_HEADS * DSA_VROWS, tm), lambda i: (0, i)),
            pl.BlockSpec((n_qi, tm), lambda i: (0, i)),
            pl.BlockSpec((tm, LANES), lambda i: (i, 0)),
            pl.BlockSpec((LANES, tm), lambda i: (0, i)),
        ),
        compiler_params=_cparams(("parallel",)),
        name="dsa_prep",
    )(z, z, tab_a, tab_i)


DSA_TQ = 128
DSA_KC = 512
DSA_KA = 256
DSA_GROUP_SETS = ((0, 1, 2, 3),)


def _dsa_kernel(qt_ref, qit_ref, wt_ref, ki_ref, k_ref, vt_ref, o_ref,
                qi_s, qg_s, sc_s, key_s, m_s, acc_s, *, n_top):
    qb = pl.program_id(0)
    q0 = qb * DSA_TQ
    n_chunks = (q0 + DSA_TQ + DSA_KC - 1) // DSA_KC
    q_pos = q0 + lax.broadcasted_iota(jnp.int32, (1, DSA_TQ), 1)

    for h in range(IDX_HEADS):
        qi_s[:, h * DSA_TQ:(h + 1) * DSA_TQ] = qit_ref[h * IDX_DIM:(h + 1) * IDX_DIM, :]
    for g in range(A_KV_HEADS):
        for j in range(A_GROUP):
            h = g * A_GROUP + j
            qg_s[g, :, j * DSA_TQ:(j + 1) * DSA_TQ] = qt_ref[h * A_HEAD_DIM:(h + 1) * A_HEAD_DIM, :]

    def score_chunk(c, carry):
        for sub in range(DSA_KC // DSA_KA):
            r0 = pl.multiple_of(c * DSA_KC + sub * DSA_KA, DSA_KA)
            kic = ki_ref[pl.ds(r0, DSA_KA), 0:IDX_DIM]
            logits = _dot(kic, qi_s[...])
            acc = jnp.zeros((DSA_KA, DSA_TQ), jnp.float32)
            for h in range(IDX_HEADS):
                acc = acc + jnp.maximum(logits[:, h * DSA_TQ:(h + 1) * DSA_TQ], 0.0) * wt_ref[h:h + 1, :]
            bits = lax.bitcast_convert_type(acc, jnp.int32)
            okey = bits ^ ((bits >> 31) & jnp.int32(0x7FFFFFFF))
            k_pos = r0 + lax.broadcasted_iota(jnp.int32, (DSA_KA, 1), 0)
            key_s[pl.ds(r0, DSA_KA), :] = jnp.where(k_pos <= q_pos, okey, jnp.int32(INT_MIN))
        return carry

    lax.fori_loop(0, n_chunks, score_chunk, 0)

    def count_keys(cand, strict):
        def body(c, cnt):
            r0 = pl.multiple_of(c * DSA_KC, DSA_KC)
            blk = key_s[pl.ds(r0, DSA_KC), :]
            hit = jnp.where((blk > cand) if strict else (blk >= cand), jnp.int32(1), jnp.int32(0))
            return cnt + jnp.sum(hit.reshape(DSA_KC // SUBLANES, SUBLANES, DSA_TQ), axis=0)
        cnt8 = lax.fori_loop(0, n_chunks, body, jnp.zeros((SUBLANES, DSA_TQ), jnp.int32))
        return jnp.sum(cnt8, axis=0, keepdims=True)

    def bisect(it, u):
        cand_u = u | lax.shift_left(jnp.int32(1), 31 - it)
        cnt = count_keys(cand_u ^ jnp.int32(INT_MIN), False)
        return jnp.where(cnt >= n_top, cand_u, u)

    thr = lax.fori_loop(0, 32, bisect, jnp.zeros((1, DSA_TQ), jnp.int32)) ^ jnp.int32(INT_MIN)
    thr = jnp.maximum(thr, jnp.int32(INT_MIN + 1))
    n_sel = count_keys(thr, False)

    def bias_plain():
        def body(c, carry):
            r0 = pl.multiple_of(c * DSA_KC, DSA_KC)
            sc_s[pl.ds(r0, DSA_KC), :] = jnp.where(key_s[pl.ds(r0, DSA_KC), :] >= thr, 0.0, NEG_BIG)
            return carry
        lax.fori_loop(0, n_chunks, body, 0)

    def bias_ties():
        need = (n_top - count_keys(thr, True)).astype(jnp.float32)
        rr = lax.broadcasted_iota(jnp.int32, (DSA_KC, DSA_KC), 0)
        cc = lax.broadcasted_iota(jnp.int32, (DSA_KC, DSA_KC), 1)
        before = jnp.where(cc < rr, 1.0, 0.0).astype(MXU_DTYPE)

        def body(c, seen):
            r0 = pl.multiple_of(c * DSA_KC, DSA_KC)
            blk = key_s[pl.ds(r0, DSA_KC), :]
            tie = jnp.where(blk == thr, 1.0, 0.0)
            rank = _dot(before, _bf(tie)) + seen
            sel = jnp.logical_or(blk > thr, jnp.logical_and(blk == thr, rank < need))
            sc_s[pl.ds(r0, DSA_KC), :] = jnp.where(sel, 0.0, NEG_BIG)
            return seen + jnp.sum(tie, axis=0, keepdims=True)
        lax.fori_loop(0, n_chunks, body, jnp.zeros((1, DSA_TQ), jnp.float32))

    lax.cond(jnp.max(n_sel) > n_top, bias_ties, bias_plain)

    m_s[...] = jnp.full(m_s.shape, NEG_BIG, jnp.float32)
    acc_s[...] = jnp.zeros(acc_s.shape, jnp.float32)

    def attn_chunk(c, carry, groups):
        r0 = pl.multiple_of(c * DSA_KC, DSA_KC)
        bias = sc_s[pl.ds(r0, DSA_KC), :]
        bias_g = jnp.concatenate([bias] * A_GROUP, axis=1)
        s = {g: _dot(k_ref[pl.ds(r0, DSA_KC), g * A_HEAD_DIM:(g + 1) * A_HEAD_DIM], qg_s[g]) + bias_g for g in groups}
        m_old = {g: m_s[g] for g in groups}
        m_new = {g: jnp.maximum(m_old[g], jnp.max(s[g], axis=0, keepdims=True)) for g in groups}
        alpha = {g: jnp.exp2(m_old[g] - m_new[g]) for g in groups}
        p = {g: _bf(jnp.exp2(s[g] - m_new[g])) for g in groups}
        pv = {g: _dot(vt_ref[g * DSA_VROWS:(g + 1) * DSA_VROWS, pl.ds(r0, DSA_KC)], p[g]) for g in groups}
        acc = {g: alpha[g] * acc_s[g] + pv[g] for g in groups}
        for g in groups:
            acc_s[g] = acc[g]
            m_s[g] = m_new[g]
        return carry

    for groups in DSA_GROUP_SETS:
        lax.fori_loop(0, n_chunks, functools.partial(attn_chunk, groups=groups), 0)
    for g in range(A_KV_HEADS):
        acc = acc_s[g]
        o_t = acc[0:A_HEAD_DIM, :] / acc[A_HEAD_DIM:A_HEAD_DIM + 1, :]
        for j in range(A_GROUP):
            h = g * A_GROUP + j
            o_ref[:, h * A_HEAD_DIM:(h + 1) * A_HEAD_DIM] = _bf(o_t[:, j * DSA_TQ:(j + 1) * DSA_TQ].T)


def _dsa_attention(q_t, k_r, v_t, qi_t, ki_r, w_t, n_top):
    L = k_r.shape[0]
    n_q = A_HEADS * A_HEAD_DIM
    n_kv = A_KV_HEADS * A_HEAD_DIM
    n_qi = IDX_HEADS * IDX_DIM
    assert L % DSA_KC == 0
    resident = dict(pipeline_mode=pl.Buffered(1))
    return pl.pallas_call(
        functools.partial(_dsa_kernel, n_top=n_top),
        out_shape=jax.ShapeDtypeStruct((L, n_q), MXU_DTYPE),
        grid=(L // DSA_TQ,),
        in_specs=[
            pl.BlockSpec((n_q, DSA_TQ), lambda i: (0, i)),
            pl.BlockSpec((n_qi, DSA_TQ), lambda i: (0, i)),
            pl.BlockSpec((LANES, DSA_TQ), lambda i: (0, i)),
            pl.BlockSpec((L, LANES), lambda i: (0, 0), **resident),
            pl.BlockSpec((L, n_kv), lambda i: (0, 0), **resident),
            pl.BlockSpec((A_KV_HEADS * DSA_VROWS, L), lambda i: (0, 0), **resident),
        ],
        out_specs=pl.BlockSpec((DSA_TQ, n_q), lambda i: (i, 0)),
        scratch_shapes=[
            pltpu.VMEM((IDX_DIM, IDX_HEADS * DSA_TQ), MXU_DTYPE),
            pltpu.VMEM((A_KV_HEADS, A_HEAD_DIM, A_GROUP * DSA_TQ), MXU_DTYPE),
            pltpu.VMEM((L, DSA_TQ), jnp.float32),
            pltpu.VMEM((L, DSA_TQ), jnp.int32),
            pltpu.VMEM((A_KV_HEADS, 1, A_GROUP * DSA_TQ), jnp.float32),
            pltpu.VMEM((A_KV_HEADS, DSA_VROWS, A_GROUP * DSA_TQ), jnp.float32),
        ],
        compiler_params=_cparams(("arbitrary",)),
        name="dsa_attention",
    )(q_t, qi_t, w_t, ki_r, k_r, v_t)


def _rope_tables(L, n_half, period, theta_pow_dim):
    inv = ROPE_THETA ** (-jnp.arange(0, theta_pow_dim, 2, dtype=jnp.float32) / theta_pow_dim)
    ang = jnp.arange(L, dtype=jnp.float32)[:, None] * inv[None, :]
    cos, sin = jnp.cos(ang), jnp.sin(ang)
    pad = period - 2 * n_half
    tc = jnp.concatenate([cos, cos, jnp.ones((L, pad), jnp.float32)], axis=1)
    s1 = jnp.concatenate([-sin, jnp.zeros((L, period - n_half), jnp.float32)], axis=1)
    s2 = jnp.concatenate([jnp.zeros((L, n_half), jnp.float32), sin, jnp.zeros((L, pad), jnp.float32)], axis=1)
    reps = LANES // period
    return jnp.stack([jnp.tile(t, (1, reps)) for t in (tc, s1, s2)])


RWKV_CHUNK = 64
RWKV_SLAB = 1024
RWKV_TB = 256
P_MU_R, P_MU_K, P_MU_V, P_W0, P_A0, P_KK, P_KA, P_RK, P_LNW, P_LNB = range(10)


def _head_ones():
    r = lax.broadcasted_iota(jnp.int32, (LANES, LANES), 0) // RWKV_HEAD
    c = lax.broadcasted_iota(jnp.int32, (LANES, LANES), 1) // RWKV_HEAD
    return jnp.where(r == c, 1.0, 0.0).astype(MXU_DTYPE)


def _head_sum(x, ones_bd):
    hi = _bf(x)
    lo = _bf(x - hi.astype(jnp.float32))
    tiles = [_dot(hi[:, t * LANES:(t + 1) * LANES], ones_bd) + _dot(lo[:, t * LANES:(t + 1) * LANES], ones_bd)
             for t in range(x.shape[1] // LANES)]
    return jnp.concatenate(tiles, axis=1)


def _pair_stack(x):
    lane = lax.broadcasted_iota(jnp.int32, x.shape, 1)
    return jnp.concatenate([jnp.where(lane < RWKV_HEAD, x, 0.0), jnp.where(lane >= RWKV_HEAD, x, 0.0)], axis=0)


def _rwkv_kernel(zr_ref, zk_ref, zv_ref, zs_ref, hr_ref, hk_ref, hv_ref, hs_ref, p_ref, mus_ref, lora_ref,
                 o_ref, h_s, r_s, lw_s, k2_s, v_s, kk_s, a_s, g_s, bon_s, oo_s):
    C = RWKV_CHUNK
    tb = zr_ref.shape[0]
    i = pl.program_id(1)

    @pl.when(i == 0)
    def _():
        h_s[...] = jnp.zeros_like(h_s)

    row = lax.broadcasted_iota(jnp.int32, (tb, 1), 0)
    has_prev = jnp.where(i > 0, 1.0, 0.0)

    def mix(z_ref, halo_ref, mu):
        z = z_ref[...]
        prev = halo_ref[SUBLANES - 1:SUBLANES, :] * has_prev
        shifted = jnp.where(row == 0, prev, pltpu.roll(z, 1, 0))
        return z + (shifted - z) * mu

    r = mix(zr_ref, hr_ref, p_ref[P_MU_R:P_MU_R + 1, :])
    k = mix(zk_ref, hk_ref, p_ref[P_MU_K:P_MU_K + 1, :])
    v = mix(zv_ref, hv_ref, p_ref[P_MU_V:P_MU_V + 1, :])
    sm = mix(zs_ref, hs_ref, mus_ref[...])
    wd, ad, gd = sm[:, 0:LANES], sm[:, LANES:2 * LANES], sm[:, 2 * LANES:4 * LANES]
    x = -(p_ref[P_W0:P_W0 + 1, :] + _dot_x3(jnp.tanh(wd), lora_ref[0:LANES, :]))
    softplus = jnp.maximum(x, 0.0) + jnp.log(1.0 + jnp.exp(-jnp.abs(x)))
    lw = -jnp.exp(-softplus - 0.5)
    a = _sigmoid(p_ref[P_A0:P_A0 + 1, :] + _dot(_bf(ad), _bf(lora_ref[LANES:2 * LANES, :])))
    g_s[...] = _dot(_bf(_sigmoid(gd)), _bf(lora_ref[2 * LANES:4 * LANES, :]))
    ones_bd = _head_ones()
    kk = k * p_ref[P_KK:P_KK + 1, :]
    kk = kk * lax.rsqrt(jnp.maximum(_head_sum(kk * kk, ones_bd), 1e-24))
    k2 = k * (1.0 + (a - 1.0) * p_ref[P_KA:P_KA + 1, :])
    bon_s[...] = _head_sum(r * k2 * p_ref[P_RK:P_RK + 1, :], ones_bd)
    r_s[...] = r
    lw_s[...] = lw
    k2_s[...] = k2
    v_s[...] = v
    kk_s[...] = kk
    a_s[...] = a

    n2 = 2 * C
    rr = lax.broadcasted_iota(jnp.int32, (n2, n2), 0)
    cc = lax.broadcasted_iota(jnp.int32, (n2, n2), 1)
    strict = (cc % C) < (rr % C)
    incl = (cc % C) <= (rr % C)
    eye = jnp.where(rr == cc, 1.0, 0.0)
    tril_c = jnp.where(lax.broadcasted_iota(jnp.int32, (C, C), 1) <= lax.broadcasted_iota(jnp.int32, (C, C), 0),
                       1.0, 0.0).astype(MXU_DTYPE)

    n_pairs = RWKV_SLAB // LANES
    pairs = range(n_pairs)

    def chunk(c, carry):
        r0 = pl.multiple_of(c * C, C)

        def ld(ref):
            return [ref[pl.ds(r0, C), p * LANES:(p + 1) * LANES] for p in pairs]

        rc, lwc, kc, vc, kkc, ac = ld(r_s), ld(lw_s), ld(k2_s), ld(v_s), ld(kk_s), ld(a_s)
        h_old = [h_s[p] for p in pairs]
        cum = []
        for p in pairs:
            l1, l2, l3 = _split3(lwc[p])
            cum.append(_dot(tril_c, l1) + _dot(tril_c, l2) + _dot(tril_c, l3))
        cl = [cum[p][C - 1:C, :] for p in pairs]
        e_neg = [jnp.exp(-cum[p]) for p in pairs]
        e_end = [jnp.exp(cl[p] - cum[p]) for p in pairs]
        beta = [kkc[p] * ac[p] for p in pairs]
        ae_s = [_pair_stack(-kkc[p] * jnp.exp(cum[p] - lwc[p])) for p in pairs]
        rp_s = [_pair_stack(rc[p] * jnp.exp(cum[p])) for p in pairs]
        bm_s = [_pair_stack(beta[p] * e_neg[p]) for p in pairs]
        km_s = [_pair_stack(kc[p] * e_neg[p]) for p in pairs]
        bt_t = [_bf(_pair_stack(beta[p] * e_end[p]).T) for p in pairs]
        kt_t = [_bf(_pair_stack(kc[p] * e_end[p]).T) for p in pairs]
        v_st = [_bf(_pair_stack(vc[p])) for p in pairs]
        aa = [_dot_nt(_bf(jnp.concatenate([ae_s[p], rp_s[p]], axis=0)),
                      _bf(jnp.concatenate([bm_s[p], km_s[p]], axis=0))) for p in pairs]
        a_ab = [jnp.where(strict, aa[p][0:n2, 0:n2], 0.0) for p in pairs]
        a_ak = [_bf(jnp.where(strict, aa[p][0:n2, n2:2 * n2], 0.0)) for p in pairs]
        a_rb = [_bf(jnp.where(incl, aa[p][n2:2 * n2, 0:n2], 0.0)) for p in pairs]
        a_rk = [_bf(jnp.where(incl, aa[p][n2:2 * n2, n2:2 * n2], 0.0)) for p in pairs]
        t_inv = [eye + a_ab[p] for p in pairs]
        n_b = [_bf(a_ab[p]) for p in pairs]
        n_pow = [_dot(n_b[p], n_b[p]) for p in pairs]
        for _ in range(int(math.log2(C)) - 2):
            n_b = [_bf(n_pow[p]) for p in pairs]
            both = [_dot(jnp.concatenate([_bf(t_inv[p]), n_b[p]], axis=0), n_b[p]) for p in pairs]
            t_inv = [t_inv[p] + both[p][0:n2, :] for p in pairs]
            n_pow = [both[p][n2:2 * n2, :] for p in pairs]
        t_inv = [t_inv[p] + _dot(_bf(t_inv[p]), _bf(n_pow[p])) for p in pairs]
        t_b = [_bf(t_inv[p]) for p in pairs]
        w_k = [_bf(_dot(a_ak[p], v_st[p])) for p in pairs]
        t_rhs = [_dot(t_b[p], jnp.concatenate([_bf(ae_s[p]), w_k[p]], axis=1)) for p in pairs]
        a_til = [_bf(t_rhs[p][:, 0:LANES]) for p in pairs]
        v_til = [t_rhs[p][:, LANES:2 * LANES] for p in pairs]
        o_intra = [_dot(a_rk[p], v_st[p]) for p in pairs]
        h_kv = [_dot(kt_t[p], v_st[p]) for p in pairs]
        decay_col = [jnp.exp(jnp.broadcast_to(cl[p], (LANES, LANES)).T) for p in pairs]
        h_b = [_bf(h_old[p]) for p in pairs]
        u_b = [_bf(_dot(a_til[p], h_b[p]) + v_til[p]) for p in pairs]
        o_st = [_dot(_bf(rp_s[p]), h_b[p]) + _dot(a_rb[p], u_b[p]) + o_intra[p] for p in pairs]
        h_new = [decay_col[p] * h_old[p] + _dot(bt_t[p], u_b[p]) + h_kv[p] for p in pairs]
        for p in pairs:
            h_s[p] = h_new[p]
            oo_s[pl.ds(r0, C), p * LANES:(p + 1) * LANES] = o_st[p][0:C, :] + o_st[p][C:n2, :]
        return carry

    lax.fori_loop(0, tb // C, chunk, 0)

    o = oo_s[...]
    mean = _head_sum(o, ones_bd) * (1.0 / RWKV_HEAD)
    d = o - mean
    var = _head_sum(d * d, ones_bd) * (1.0 / RWKV_HEAD)
    y = d * lax.rsqrt(var + RWKV_GN_EPS) * p_ref[P_LNW:P_LNW + 1, :] + p_ref[P_LNB:P_LNB + 1, :]
    y = y + bon_s[...] * v_s[...]
    o_ref[...] = (y * g_s[...]).astype(o_ref.dtype)


def _rwkv_params(mu, w0, w_up, a0, a_up, g_up, k_k, k_a, r_k, ln_w, ln_b):
    d = RWKV_DIM
    mu_r, mu_k, mu_v = mu[0:d], mu[d:2 * d], mu[2 * d:3 * d]
    o = 3 * d
    mu_wd, mu_ad, mu_gd = mu[o:o + DECAY_RANK], mu[o + DECAY_RANK:o + DECAY_RANK + AAA_RANK], mu[o + DECAY_RANK + AAA_RANK:]
    rows = [mu_r, mu_k, mu_v, w0, a0, k_k, k_a, r_k.reshape(d), ln_w, ln_b]
    p_rows = jnp.concatenate([jnp.stack(rows), jnp.zeros((16 - len(rows), d), jnp.float32)], axis=0)

    def pad_to(x, n, axis):
        widths = [(0, 0)] * x.ndim
        widths[axis] = (0, n - x.shape[axis])
        return jnp.pad(x, widths)

    mu_small = jnp.concatenate([pad_to(mu_wd, LANES, 0), pad_to(mu_ad, LANES, 0), mu_gd]).reshape(1, 4 * LANES)
    lora = jnp.concatenate([pad_to(w_up, LANES, 0), pad_to(a_up, LANES, 0), g_up], axis=0)
    return p_rows, mu_small, lora


def _rwkv_mix(z, p_rows, mu_small, lora):
    L = z.shape[0]
    tb, w = RWKV_TB, RWKV_SLAB
    assert L % tb == 0
    hb = tb // SUBLANES

    def blk(col0):
        return pl.BlockSpec((tb, w), lambda s, i: (i, col0 // w + s))

    def halo(col0):
        return pl.BlockSpec((SUBLANES, w), lambda s, i: (jnp.maximum(i * hb - 1, 0), col0 // w + s))

    small_w = 4 * LANES
    f32 = jnp.float32
    return pl.pallas_call(
        _rwkv_kernel,
        out_shape=jax.ShapeDtypeStruct((L, RWKV_DIM), MXU_DTYPE),
        grid=(RWKV_DIM // w, L // tb),
        in_specs=[
            blk(ZE_R), blk(ZE_RK), blk(ZE_RV),
            pl.BlockSpec((tb, small_w), lambda s, i: (i, ZE_WD // small_w)),
            halo(ZE_R), halo(ZE_RK), halo(ZE_RV),
            pl.BlockSpec((SUBLANES, small_w), lambda s, i: (jnp.maximum(i * hb - 1, 0), ZE_WD // small_w)),
            pl.BlockSpec((16, w), lambda s, i: (0, s)),
            pl.BlockSpec((1, small_w), lambda s, i: (0, 0)),
            pl.BlockSpec((small_w, w), lambda s, i: (0, s)),
        ],
        out_specs=pl.BlockSpec((tb, w), lambda s, i: (i, s)),
        scratch_shapes=[pltpu.VMEM((w // LANES, LANES, LANES), f32)] + [pltpu.VMEM((tb, w), f32) for _ in range(9)],
        compiler_params=_cparams(("parallel", "arbitrary")),
        name="rwkv7_mix",
    )(z, z, z, z, z, z, z, z, p_rows, mu_small, lora)


S5_TB = 512
S5_SEG = SUBLANES
S5_SLAB_GROUPS = LANES // S5_GROUP
S5_SW = S5_SLAB_GROUPS * S5_STATE


def _cmul_add(ar, ai, xr, xi, br, bi):
    return ar * xr - ai * xi + br, ar * xi + ai * xr + bi


def _s5_kernel(u_ref, w_ref, c_ref, apow_ref, d_ref, zg_ref, zgb_ref, st_s, up_s, bu_s, x_s, y_s):
    tb = u_ref.shape[0]
    ts = tb // S5_SEG
    sw = S5_SW
    i = pl.program_id(1)

    @pl.when(i == 0)
    def _():
        st_s[...] = jnp.zeros_like(st_s)

    for tau in range(ts):
        up_s[tau * S5_SEG:(tau + 1) * S5_SEG, :] = u_ref[pl.ds(tau, S5_SEG, stride=ts), :]
    bu_s[...] = _dot(_bf(up_s[...]), _bf(w_ref[0]))
    a1 = apow_ref[0, 0:1, :]
    ar = jnp.broadcast_to(a1[:, 0:sw], (S5_SEG, sw))
    ai = jnp.broadcast_to(a1[:, sw:2 * sw], (S5_SEG, sw))

    def scan(tau, x):
        r0 = pl.multiple_of(tau * S5_SEG, S5_SEG)
        b = bu_s[pl.ds(r0, S5_SEG), :]
        nr, ni = _cmul_add(ar, ai, x[0], x[1], b[:, 0:sw], b[:, sw:2 * sw])
        x_s[pl.ds(r0, S5_SEG), :] = jnp.concatenate([nr, ni], axis=1)
        return nr, ni

    zero = jnp.zeros((S5_SEG, sw), jnp.float32)
    xr, xi = lax.fori_loop(0, ts, scan, (zero, zero), unroll=4)

    a_ts = apow_ref[0, ts - 1:ts, :]
    tr, ti = a_ts[:, 0:sw], a_ts[:, sw:2 * sw]
    cr, ci = st_s[:, 0:sw], st_s[:, sw:2 * sw]
    ent_r, ent_i = [], []
    for s in range(S5_SEG):
        ent_r.append(cr)
        ent_i.append(ci)
        cr, ci = _cmul_add(tr, ti, cr, ci, xr[s:s + 1, :], xi[s:s + 1, :])
    st_s[...] = jnp.concatenate([cr, ci], axis=1)
    er = jnp.concatenate(ent_r, axis=0)
    ei = jnp.concatenate(ent_i, axis=0)

    def fix(tau, carry):
        r0 = pl.multiple_of(tau * S5_SEG, S5_SEG)
        ap = apow_ref[0, pl.ds(tau, 1), :]
        x = x_s[pl.ds(r0, S5_SEG), :]
        nr, ni = _cmul_add(ap[:, 0:sw], ap[:, sw:2 * sw], er, ei, x[:, 0:sw], x[:, sw:2 * sw])
        x_s[pl.ds(r0, S5_SEG), :] = jnp.concatenate([nr, ni], axis=1)
        return carry

    lax.fori_loop(0, ts, fix, 0, unroll=4)
    yp = _dot(_bf(x_s[...]), _bf(c_ref[0]))
    for tau in range(ts):
        y_s[pl.ds(tau, S5_SEG, stride=ts), :] = yp[tau * S5_SEG:(tau + 1) * S5_SEG, :]
    y = y_s[...] + d_ref[...] * u_ref[...]
    zg = 0.5 * y * (1.0 + jnp.tanh(math.sqrt(2.0 / math.pi) * (y + 0.044715 * (y * y * y))))
    zg_ref[...] = zg
    zgb_ref[...] = zg.astype(zgb_ref.dtype)


def _s5_params(lam_re, lam_im, log_step, b_re, b_im, c_re, c_im, n_pow):
    lr = jnp.minimum(lam_re, -1e-4)
    li = lam_im
    step = jnp.exp(log_step)[:, None]
    mag = jnp.exp(lr * step)
    abar_r = mag * jnp.cos(li * step)
    abar_i = mag * jnp.sin(li * step)
    den = lr * lr + li * li
    cr = (lr * (abar_r - 1.0) + li * abar_i) / den
    ci = (lr * abar_i - li * (abar_r - 1.0)) / den
    bbar_r = cr[..., None] * b_re - ci[..., None] * b_im
    bbar_i = cr[..., None] * b_im + ci[..., None] * b_re
    ns = S5_GROUPS // S5_SLAB_GROUPS
    eye = jnp.eye(S5_SLAB_GROUPS, dtype=jnp.float32)

    def in_mat(bb):
        t = jnp.einsum('ab,sapi->saibp', eye, bb.reshape(ns, S5_SLAB_GROUPS, S5_STATE, S5_GROUP))
        return t.reshape(ns, LANES, S5_SW)

    def out_mat(cc):
        t = jnp.einsum('ab,saop->sapbo', eye, cc.reshape(ns, S5_SLAB_GROUPS, S5_GROUP, S5_STATE))
        return t.reshape(ns, S5_SW, LANES)

    w_in = jnp.concatenate([in_mat(bbar_r), in_mat(bbar_i)], axis=2)
    w_out = jnp.concatenate([out_mat(c_re), -out_mat(c_im)], axis=1)
    n = jnp.arange(1, n_pow + 1, dtype=jnp.float32)[None, :, None]
    lrs = (lr * step).reshape(ns, 1, S5_SW)
    lis = (li * step).reshape(ns, 1, S5_SW)
    pm = jnp.exp(n * lrs)
    apow = jnp.concatenate([pm * jnp.cos(n * lis), pm * jnp.sin(n * lis)], axis=2)
    return _bf(w_in), _bf(w_out), apow


def _s5_mix(z, w_in, w_out, apow, d_skip):
    L = z.shape[0]
    tb = S5_TB
    assert L % tb == 0
    ns = S5_DIM // LANES
    f32 = jnp.float32
    return pl.pallas_call(
        _s5_kernel,
        out_shape=(jax.ShapeDtypeStruct((L, S5_DIM), f32), jax.ShapeDtypeStruct((L, S5_DIM), MXU_DTYPE)),
        grid=(ns, L // tb),
        in_specs=[
            pl.BlockSpec((tb, LANES), lambda s, i: (i, s)),
            pl.BlockSpec((1, LANES, 2 * S5_SW), lambda s, i: (s, 0, 0)),
            pl.BlockSpec((1, 2 * S5_SW, LANES), lambda s, i: (s, 0, 0)),
            pl.BlockSpec((1, tb // S5_SEG, 2 * S5_SW), lambda s, i: (s, 0, 0)),
            pl.BlockSpec((1, LANES), lambda s, i: (0, s)),
        ],
        out_specs=(pl.BlockSpec((tb, LANES), lambda s, i: (i, s)), pl.BlockSpec((tb, LANES), lambda s, i: (i, s))),
        scratch_shapes=[
            pltpu.VMEM((1, 2 * S5_SW), f32),
            pltpu.VMEM((tb, LANES), f32),
            pltpu.VMEM((tb, 2 * S5_SW), f32),
            pltpu.VMEM((tb, 2 * S5_SW), f32),
            pltpu.VMEM((tb, LANES), f32),
        ],
        compiler_params=_cparams(("parallel", "arbitrary")),
        name="s5_mix",
    )(z, w_in, w_out, apow, d_skip.reshape(1, S5_DIM))


RET_CHUNK = 512


def _ret_kernel(q_ref, k_ref, v_ref, g_ref, cos_ref, sin_ref, intra_ref, rowdec_ref, o_ref, s_s):
    c = pl.program_id(1)

    @pl.when(c == 0)
    def _():
        s_s[...] = jnp.zeros_like(s_s)

    half = RET_HEAD_DIM // 2
    cos, sin = cos_ref[...], sin_ref[...]

    def rot(x):
        x1, x2 = x[:, 0:half], x[:, half:2 * half]
        return jnp.concatenate([x1 * cos - x2 * sin, x1 * sin + x2 * cos], axis=1)

    q = rot(q_ref[...])
    k = rot(k_ref[...]) * (RET_HEAD_DIM ** -0.5)
    vb = _bf(v_ref[...])
    qb = _bf(q)
    dec = rowdec_ref[0]

    def lanes2(x):
        return jnp.concatenate([x, x], axis=1)

    xi, zeta, g_chunk = lanes2(dec[:, 0:LANES]), lanes2(dec[:, LANES:2 * LANES]), lanes2(dec[:, 2 * LANES:3 * LANES])
    att = _dot_nt(qb, _bf(k)) * intra_ref[0]
    s_old = s_s[...]
    o = _dot(_bf(att), vb) + _dot(qb, _bf(s_old)) * xi
    s_s[...] = s_old * g_chunk[0:1, :] + _dot(_bf((k * zeta).T), vb)
    mean = jnp.mean(o, axis=-1, keepdims=True)
    d = o - mean
    var = jnp.mean(d * d, axis=-1, keepdims=True)
    gate = g_ref[...]
    o_ref[...] = (gate * _sigmoid(gate) * (d * lax.rsqrt(var + RET_GN_EPS))).astype(o_ref.dtype)


def _ret_tables(L):
    C = RET_CHUNK
    inv = 1.0 / (RET_ROPE_BASE ** jnp.linspace(0.0, 1.0, RET_HEAD_DIM // 2, dtype=jnp.float32))
    ang = jnp.arange(L, dtype=jnp.float32)[:, None] * inv[None, :]
    log_g = jnp.log(1.0 - 2.0 ** (-5.0 - jnp.arange(RET_HEADS, dtype=jnp.float32)))
    pos = jnp.arange(C, dtype=jnp.float32)
    diff = pos[:, None] - pos[None, :]
    intra = jnp.where(diff >= 0, jnp.exp(jnp.maximum(diff, 0.0)[None] * log_g[:, None, None]), 0.0)
    xi = jnp.exp((pos + 1.0)[None, :] * log_g[:, None])
    zeta = jnp.exp((C - 1.0 - pos)[None, :] * log_g[:, None])
    g_chunk = jnp.broadcast_to(jnp.exp(C * log_g)[:, None], (RET_HEADS, C))
    rowdec = jnp.concatenate([jnp.broadcast_to(t[:, :, None], (RET_HEADS, C, LANES)) for t in (xi, zeta, g_chunk)], axis=2)
    return jnp.cos(ang), jnp.sin(ang), intra, rowdec


def _retention_mix(z, cos, sin, intra, rowdec):
    L = z.shape[0]
    C, hd = RET_CHUNK, RET_HEAD_DIM
    assert L % C == 0
    base = S5_DIM // hd

    def blk(j):
        return pl.BlockSpec((C, hd), lambda h, c: (c, base + j * RET_HEADS + h))

    return pl.pallas_call(
        _ret_kernel,
        out_shape=jax.ShapeDtypeStruct((L, RET_HEADS * hd), MXU_DTYPE),
        grid=(RET_HEADS, L // C),
        in_specs=[
            blk(0), blk(1), blk(2), blk(3),
            pl.BlockSpec((C, hd // 2), lambda h, c: (c, 0)),
            pl.BlockSpec((C, hd // 2), lambda h, c: (c, 0)),
            pl.BlockSpec((1, C, C), lambda h, c: (h, 0, 0)),
            pl.BlockSpec((1, C, 3 * LANES), lambda h, c: (h, 0, 0)),
        ],
        out_specs=pl.BlockSpec((C, hd), lambda h, c: (c, h)),
        scratch_shapes=[pltpu.VMEM((hd, hd), jnp.float32)],
        compiler_params=_cparams(("parallel", "arbitrary")),
        name="retention_mix",
    )(z, z, z, z, cos, sin, intra, rowdec)


def _even_w_in(w):
    d = w.shape[0]
    n_attn = ZE_R
    o_ki = n_attn
    o_wi = o_ki + IDX_DIM
    o_rw = o_wi + IDX_HEADS
    o_wd = o_rw + 3 * RWKV_DIM
    o_ad = o_wd + DECAY_RANK
    o_gd = o_ad + AAA_RANK

    def zeros(n):
        return jnp.zeros((d, n), w.dtype)

    ki = w[:, o_ki:o_wi]
    parts = [
        w[:, 0:n_attn], w[:, o_rw:o_wd],
        ki, ki, w[:, o_wi:o_rw], zeros(ZE_WD - ZE_WI - IDX_HEADS),
        w[:, o_wd:o_ad], zeros(LANES - DECAY_RANK),
        w[:, o_ad:o_gd], zeros(LANES - AAA_RANK),
        w[:, o_gd:o_gd + GATE_RANK],
    ]
    out = jnp.concatenate(parts, axis=1).astype(MXU_DTYPE)
    assert out.shape[1] == ZE_COLS
    return out


def _ffn(h, norm_g, w_gate, w_up, w_down_b, layer):
    hn = _rmsnorm(h, norm_g, MXU_DTYPE)
    mid = _matmul([hn], [(w_gate, layer, 0), (w_up, layer, 0)], [(0, 0), (0, 1)], [], _ep_swiglu, MXU_DTYPE,
                  tm=1024, tn=256, name="ffn_gate_up")
    return _matmul([mid], [(w_down_b, layer, 0)], [(0, 0)], [("tile", h)], _ep_resid, jnp.float32,
                   tm=1024, tn=512, tk=w_down_b.shape[1] // 2, name="ffn_down")


def _out_proj(h, o_first, o_second, w_out, layer):
    return _matmul([o_first, o_second], [(w_out, layer, 0), (w_out, layer, 1)], [(0, 0), (1, 1)], [("tile", h)],
                   _ep_resid2, jnp.float32, tm=1024, tn=512, name="out_proj")


def kernel(x, norm_mix, norm_ffn, ffn_gate, ffn_up, ffn_down, e_w_in, e_w_out, e_mu, e_w0, e_w_up, e_a0, e_a_up, e_g_up, e_k_k, e_k_a, e_r_k, e_ln_w, e_ln_b, o_w_in, o_w_out, o_lam_re, o_lam_im, o_log_step, o_b_re, o_b_im, o_c_re, o_c_im, o_d_skip, o_w_glu, o_b_glu, final_norm):
    assert x.shape[0] == 1
    h = x.reshape(x.shape[1], x.shape[2])
    L = h.shape[0]
    n_top = min(TOPK_MAX, L // 4)

    ffn_down_b = _bf(ffn_down)
    hn = _rmsnorm(h, norm_mix[0], MXU_DTYPE)
    z = _matmul([hn], [(_even_w_in(e_w_in[0]), 0, 0)], [(0, 0)], [], _ep_plain, jnp.float32, tm=1024, tn=1024,
                name="even_in_proj")
    tab_a = _rope_tables(L, A_ROT_HALF, A_HEAD_DIM, 2 * A_ROT_HALF)
    tab_i = _rope_tables(L, IDX_ROT_HALF, IDX_DIM, 2 * IDX_ROT_HALF)
    q_t, k_r, v_t, qi_t, ki_r, w_t = _dsa_prep(z, tab_a, tab_i)
    o_a = _dsa_attention(q_t, k_r, v_t, qi_t, ki_r, w_t, n_top)
    p_rows, mu_small, lora = _rwkv_params(e_mu[0], e_w0[0], e_w_up[0], e_a0[0], e_a_up[0], e_g_up[0], e_k_k[0],
                                          e_k_a[0], e_r_k[0], e_ln_w[0], e_ln_b[0])
    o_b = _rwkv_mix(z, p_rows, mu_small, lora)
    h = _out_proj(h, o_a, o_b, e_w_out, 0)
    h = _ffn(h, norm_ffn[0], ffn_gate, ffn_up, ffn_down_b, 0)

    hn = _rmsnorm(h, norm_mix[1], MXU_DTYPE)
    z = _matmul([hn], [(o_w_in, 0, 0)], [(0, 0)], [], _ep_plain, jnp.float32, tm=1024, tn=512, name="odd_in_proj")
    s5_in, s5_out, s5_apow = _s5_params(o_lam_re[0], o_lam_im[0], o_log_step[0], o_b_re[0], o_b_im[0], o_c_re[0],
                                        o_c_im[0], S5_TB // S5_SEG)
    zg, zg_b = _s5_mix(z, s5_in, s5_out, s5_apow, o_d_skip[0])
    o_c = _matmul([zg_b], [(o_w_glu, 0, 0)], [(0, 0)], [("tile", zg), ("row", o_b_glu[0].reshape(1, S5_DIM))], _ep_glu,
                  MXU_DTYPE, tm=1024, tn=512, name="s5_glu")
    o_d = _retention_mix(z, *_ret_tables(L))
    h = _out_proj(h, o_c, o_d, o_w_out, 0)
    h = _ffn(h, norm_ffn[1], ffn_gate, ffn_up, ffn_down_b, 1)

    return _rmsnorm(h, final_norm, jnp.float32).reshape(x.shape)
```

```python
import functools
import math

import jax
import jax.numpy as jnp
from jax import lax
from jax.experimental import pallas as pl
from jax.experimental.pallas import tpu as pltpu

NORM_EPS = 1e-6
A_HEAD_DIM = 128
A_HEADS = 16
A_KV_HEADS = 4
A_GROUP = A_HEADS // A_KV_HEADS
A_ROT_HALF = 16
IDX_HEADS = 32
IDX_DIM = 64
IDX_ROT_HALF = 8
TOPK_MAX = 256
ROPE_THETA = 500000.0
RWKV_HEAD = 64
RWKV_DIM = 2048
RWKV_HEADS = 32
DECAY_RANK = 96
AAA_RANK = 96
GATE_RANK = 256
RWKV_GN_EPS = 1e-5 * RWKV_HEAD
S5_DIM = 2048
S5_GROUP = 16
S5_GROUPS = 128
S5_STATE = 64
RET_HEAD_DIM = 256
RET_HEADS = 8
RET_ROPE_BASE = 10000.0
RET_GN_EPS = 1e-5

LANES = 128
SUBLANES = 8
VMEM_LIMIT_BYTES = 56 * 2**20

MXU_DTYPE = jnp.bfloat16
NEG_BIG = -1e30
INT_MIN = -2**31

ZE_Q, ZE_K, ZE_V, ZE_QI = 0, 2048, 2560, 3072
ZE_KI = 5120
ZE_WI_LANE = IDX_DIM
ZE_R, ZE_RK, ZE_RV = 5216, 7264, 9312
ZE_LORA = 11360
ZE_SKEW = ZE_R % 128
ZE_MAIN_COLS = 11776
ZS_WD, ZS_AD, ZS_GD, ZS_COLS = 0, 128, 256, 512


def _cparams(sem):
    return pltpu.CompilerParams(dimension_semantics=sem, vmem_limit_bytes=VMEM_LIMIT_BYTES)


def _bf(x):
    return x.astype(MXU_DTYPE)


def _dot(a, b):
    return jnp.dot(a, b, preferred_element_type=jnp.float32)


def _dot_nt(a, b):
    return lax.dot_general(a, b, (((1,), (1,)), ((), ())), preferred_element_type=jnp.float32)


def _split3(x):
    h1 = _bf(x)
    r1 = x - h1.astype(jnp.float32)
    h2 = _bf(r1)
    h3 = _bf(r1 - h2.astype(jnp.float32))
    return h1, h2, h3


def _dot_x3(a, b):
    ah = _bf(a)
    al = _bf(a - ah.astype(jnp.float32))
    bh = _bf(b)
    bl = _bf(b - bh.astype(jnp.float32))
    return _dot(ah, bh) + _dot(ah, bl) + _dot(al, bh)


def _dot_exact_rhs(a, b_bf):
    h1, h2, h3 = _split3(a)
    return _dot(h1, b_bf) + _dot(h2, b_bf) + _dot(h3, b_bf)


def _sigmoid(x):
    return 1.0 / (1.0 + jnp.exp(-x))


def _rmsnorm_kernel(x_ref, g_ref, o_ref):
    x = x_ref[...]
    ms = jnp.mean(x * x, axis=-1, keepdims=True)
    o_ref[...] = (x * lax.rsqrt(ms + NORM_EPS) * g_ref[...]).astype(o_ref.dtype)


def _rmsnorm(x, g, out_dtype, tm=256):
    m, d = x.shape
    return pl.pallas_call(
        _rmsnorm_kernel,
        out_shape=jax.ShapeDtypeStruct((m, d), out_dtype),
        grid=(m // tm,),
        in_specs=[pl.BlockSpec((tm, d), lambda i: (i, 0)), pl.BlockSpec((1, d), lambda i: (0, 0))],
        out_specs=pl.BlockSpec((tm, d), lambda i: (i, 0)),
        compiler_params=_cparams(("parallel",)),
        name="rmsnorm",
    )(x, g.reshape(1, d))


def _mm_kernel(*refs, n_a, n_w, n_e, pairs, epilogue, nk):
    a_refs = refs[:n_a]
    w_refs = refs[n_a:n_a + n_w]
    e_refs = refs[n_a + n_w:n_a + n_w + n_e]
    o_ref = refs[n_a + n_w + n_e]
    acc_refs = refs[n_a + n_w + n_e + 1:]
    if nk == 1:
        accs = [_dot(a_refs[i][...], _bf(w_refs[j][...])) for i, j in pairs]
        o_ref[...] = epilogue(accs, [e[...] for e in e_refs]).astype(o_ref.dtype)
        return
    k = pl.program_id(2)

    @pl.when(k == 0)
    def _():
        for acc in acc_refs:
            acc[...] = jnp.zeros_like(acc)

    for acc, (i, j) in zip(acc_refs, pairs):
        acc[...] += _dot(a_refs[i][...], _bf(w_refs[j][...]))

    @pl.when(k == nk - 1)
    def _():
        o_ref[...] = epilogue([acc[...] for acc in acc_refs], [e[...] for e in e_refs]).astype(o_ref.dtype)


def _weight_spec(w, layer, k_off, tk, tn):
    if w.ndim == 3:
        return pl.BlockSpec((None, tk, tn), lambda i, j, k: (layer, k + k_off, j))
    return pl.BlockSpec((tk, tn), lambda i, j, k: (k + k_off, j))


def _matmul(a_list, w_list, pairs, extras, epilogue, out_dtype, tm, tn, tk=None, n_out=None, name="matmul"):
    m, kdim = a_list[0].shape
    n = w_list[0][0].shape[-1] if n_out is None else n_out
    tk = kdim if tk is None else tk
    tm = min(tm, m)
    nk = kdim // tk
    assert m % tm == 0 and n % tn == 0 and kdim % tk == 0
    in_specs = [pl.BlockSpec((tm, tk), lambda i, j, k: (i, k)) for _ in a_list]
    in_specs += [_weight_spec(w, layer, k_off, tk, tn) for w, layer, k_off in w_list]
    w_list = [w for w, _, _ in w_list]
    e_arrays = []
    for kind, arr in extras:
        e_arrays.append(arr)
        if kind == "tile":
            in_specs.append(pl.BlockSpec((tm, tn), lambda i, j, k: (i, j)))
        else:
            in_specs.append(pl.BlockSpec((1, tn), lambda i, j, k: (0, j)))
    scratch = [] if nk == 1 else [pltpu.VMEM((tm, tn), jnp.float32) for _ in pairs]
    kern = functools.partial(_mm_kernel, n_a=len(a_list), n_w=len(w_list), n_e=len(extras),
                             pairs=tuple(pairs), epilogue=epilogue, nk=nk)
    return pl.pallas_call(
        kern,
        out_shape=jax.ShapeDtypeStruct((m, n), out_dtype),
        grid=(m // tm, n // tn, nk),
        in_specs=in_specs,
        out_specs=pl.BlockSpec((tm, tn), lambda i, j, k: (i, j)),
        scratch_shapes=scratch,
        compiler_params=_cparams(("parallel", "parallel", "arbitrary")),
        name=name,
    )(*a_list, *w_list, *e_arrays)


def _ep_plain(accs, ex):
    return accs[0]


def _ep_swiglu(accs, ex):
    g = accs[0]
    return g * _sigmoid(g) * accs[1]


def _ep_resid(accs, ex):
    return ex[0] + accs[0]


def _ep_resid2(accs, ex):
    return ex[0] + (accs[0] + accs[1])


def _ep_glu(accs, ex):
    zg, b = ex
    return zg * _sigmoid(accs[0] + b)


def _rope_tile(x, tc, s1, s2, n_half):
    return x * tc + pltpu.roll(x, n_half, 1) * s2 + pltpu.roll(x, LANES - n_half, 1) * s1


DSA_VPAD = 16
DSA_VROWS = A_HEAD_DIM + DSA_VPAD


def _dsa_prep_kernel(zq_ref, zs_ref, ta_ref, ti_ref, qt_ref, k_ref, vt_ref, qit_ref, ki_ref, wt_ref):
    ta_c, ta_1, ta_2 = ta_ref[0], ta_ref[1], ta_ref[2]
    ti_c, ti_1, ti_2 = ti_ref[0], ti_ref[1], ti_ref[2]
    q_scale = A_HEAD_DIM ** -0.5 * math.log2(math.e)
    for h in range(A_HEADS):
        x = zq_ref[:, ZE_Q + h * LANES:ZE_Q + (h + 1) * LANES]
        y = _rope_tile(x, ta_c, ta_1, ta_2, A_ROT_HALF) * q_scale
        qt_ref[h * LANES:(h + 1) * LANES, :] = _bf(y.T)
    for g in range(A_KV_HEADS):
        x = zq_ref[:, ZE_K + g * LANES:ZE_K + (g + 1) * LANES]
        k_ref[:, g * LANES:(g + 1) * LANES] = _bf(_rope_tile(x, ta_c, ta_1, ta_2, A_ROT_HALF))
        v = zq_ref[:, ZE_V + g * LANES:ZE_V + (g + 1) * LANES]
        vt_ref[g * DSA_VROWS:g * DSA_VROWS + A_HEAD_DIM, :] = _bf(v.T)
        vt_ref[g * DSA_VROWS + A_HEAD_DIM:(g + 1) * DSA_VROWS, :] = jnp.ones((DSA_VPAD, v.shape[0]), MXU_DTYPE)
    for c in range(IDX_HEADS * IDX_DIM // LANES):
        x = zq_ref[:, ZE_QI + c * LANES:ZE_QI + (c + 1) * LANES]
        y = _rope_tile(x, ti_c, ti_1, ti_2, IDX_ROT_HALF)
        qit_ref[c * LANES:(c + 1) * LANES, :] = _bf(y.T)
    ki_ref[...] = _bf(_rope_tile(zs_ref[...], ti_c, ti_1, ti_2, IDX_ROT_HALF))
    w_scale = (IDX_DIM ** -0.5) * (IDX_HEADS ** -0.5)
    wt_ref[...] = (zs_ref[...] * w_scale).T


def _dsa_prep(z, tab_a, tab_i, tm=256):
    L = z.shape[0]
    n_q = A_HEADS * A_HEAD_DIM
    n_kv = A_KV_HEADS * A_HEAD_DIM
    n_qi = IDX_HEADS * IDX_DIM
    out_shape = (
        jax.ShapeDtypeStruct((n_q, L), MXU_DTYPE),
        jax.ShapeDtypeStruct((L, n_kv), MXU_DTYPE),
        jax.ShapeDtypeStruct((A_KV_HEADS * DSA_VROWS, L), MXU_DTYPE),
        jax.ShapeDtypeStruct((n_qi, L), MXU_DTYPE),
        jax.ShapeDtypeStruct((L, LANES), MXU_DTYPE),
        jax.ShapeDtypeStruct((LANES, L), jnp.float32),
    )
    return pl.pallas_call(
        _dsa_prep_kernel,
        out_shape=out_shape,
        grid=(L // tm,),
        in_specs=[
            pl.BlockSpec((tm, ZE_KI), lambda i: (i, 0)),
            pl.BlockSpec((tm, LANES), lambda i: (i, ZE_KI // LANES)),
            pl.BlockSpec((3, tm, LANES), lambda i: (0, i, 0)),
            pl.BlockSpec((3, tm, LANES), lambda i: (0, i, 0)),
        ],
        out_specs=(
            pl.BlockSpec((n_q, tm), lambda i: (0, i)),
            pl.BlockSpec((tm, n_kv), lambda i: (i, 0)),
            pl.BlockSpec((A_KV_HEADS * DSA_VROWS, tm), lambda i: (0, i)),
            pl.BlockSpec((n_qi, tm), lambda i: (0, i)),
            pl.BlockSpec((tm, LANES), lambda i: (i, 0)),
            pl.BlockSpec((LANES, tm), lambda i: (0, i)),
        ),
        compiler_params=_cparams(("parallel",)),
        name="dsa_prep",
    )(z, z, tab_a, tab_i)


DSA_TQ = 128
DSA_KC = 512
DSA_KA = 256
DSA_GROUP_SETS = ((0, 1, 2, 3),)


def _dsa_kernel(qt_ref, qit_ref, wt_ref, ki_ref, k_ref, vt_ref, o_ref,
                qi_s, qg_s, sc_s, key_s, m_s, acc_s, *, n_top):
    qb = pl.program_id(0)
    q0 = qb * DSA_TQ
    n_chunks = (q0 + DSA_TQ + DSA_KC - 1) // DSA_KC
    q_pos = q0 + lax.broadcasted_iota(jnp.int32, (1, DSA_TQ), 1)

    for h in range(IDX_HEADS):
        qi_s[:, h * DSA_TQ:(h + 1) * DSA_TQ] = qit_ref[h * IDX_DIM:(h + 1) * IDX_DIM, :]
    for g in range(A_KV_HEADS):
        for j in range(A_GROUP):
            h = g * A_GROUP + j
            qg_s[g, :, j * DSA_TQ:(j + 1) * DSA_TQ] = qt_ref[h * A_HEAD_DIM:(h + 1) * A_HEAD_DIM, :]

    def score_chunk(c, carry):
        for sub in range(DSA_KC // DSA_KA):
            r0 = pl.multiple_of(c * DSA_KC + sub * DSA_KA, DSA_KA)
            kic = ki_ref[pl.ds(r0, DSA_KA), 0:IDX_DIM]
            logits = _dot(kic, qi_s[...])
            acc = jnp.zeros((DSA_KA, DSA_TQ), jnp.float32)
            for h in range(IDX_HEADS):
                w_h = wt_ref[ZE_WI_LANE + h:ZE_WI_LANE + h + 1, :]
                acc = acc + jnp.maximum(logits[:, h * DSA_TQ:(h + 1) * DSA_TQ], 0.0) * w_h
            bits = lax.bitcast_convert_type(acc, jnp.int32)
            okey = bits ^ ((bits >> 31) & jnp.int32(0x7FFFFFFF))
            k_pos = r0 + lax.broadcasted_iota(jnp.int32, (DSA_KA, 1), 0)
            key_s[pl.ds(r0, DSA_KA), :] = jnp.where(k_pos <= q_pos, okey, jnp.int32(INT_MIN))
        return carry

    lax.fori_loop(0, n_chunks, score_chunk, 0)

    def count_keys(cand, strict):
        def body(c, cnt):
            r0 = pl.multiple_of(c * DSA_KC, DSA_KC)
            blk = key_s[pl.ds(r0, DSA_KC), :]
            hit = jnp.where((blk > cand) if strict else (blk >= cand), jnp.int32(1), jnp.int32(0))
            return cnt + jnp.sum(hit.reshape(DSA_KC // SUBLANES, SUBLANES, DSA_TQ), axis=0)
        cnt8 = lax.fori_loop(0, n_chunks, body, jnp.zeros((SUBLANES, DSA_TQ), jnp.int32))
        return jnp.sum(cnt8, axis=0, keepdims=True)

    def bisect(it, u):
        cand_u = u | lax.shift_left(jnp.int32(1), 31 - it)
        cnt = count_keys(cand_u ^ jnp.int32(INT_MIN), False)
        return jnp.where(cnt >= n_top, cand_u, u)

    thr = lax.fori_loop(0, 32, bisect, jnp.zeros((1, DSA_TQ), jnp.int32)) ^ jnp.int32(INT_MIN)
    thr = jnp.maximum(thr, jnp.int32(INT_MIN + 1))
    n_sel = count_keys(thr, False)

    def bias_plain():
        def body(c, carry):
            r0 = pl.multiple_of(c * DSA_KC, DSA_KC)
            sc_s[pl.ds(r0, DSA_KC), :] = jnp.where(key_s[pl.ds(r0, DSA_KC), :] >= thr, 0.0, NEG_BIG)
            return carry
        lax.fori_loop(0, n_chunks, body, 0)

    def bias_ties():
        need = (n_top - count_keys(thr, True)).astype(jnp.float32)
        rr = lax.broadcasted_iota(jnp.int32, (DSA_KC, DSA_KC), 0)
        cc = lax.broadcasted_iota(jnp.int32, (DSA_KC, DSA_KC), 1)
        before = jnp.where(cc < rr, 1.0, 0.0).astype(MXU_DTYPE)

        def body(c, seen):
            r0 = pl.multiple_of(c * DSA_KC, DSA_KC)
            blk = key_s[pl.ds(r0, DSA_KC), :]
            tie = jnp.where(blk == thr, 1.0, 0.0)
            rank = _dot(before, _bf(tie)) + seen
            sel = jnp.logical_or(blk > thr, jnp.logical_and(blk == thr, rank < need))
            sc_s[pl.ds(r0, DSA_KC), :] = jnp.where(sel, 0.0, NEG_BIG)
            return seen + jnp.sum(tie, axis=0, keepdims=True)
        lax.fori_loop(0, n_chunks, body, jnp.zeros((1, DSA_TQ), jnp.float32))

    lax.cond(jnp.max(n_sel) > n_top, bias_ties, bias_plain)

    m_s[...] = jnp.full(m_s.shape, NEG_BIG, jnp.float32)
    acc_s[...] = jnp.zeros(acc_s.shape, jnp.float32)

    def attn_chunk(c, carry, groups):
        r0 = pl.multiple_of(c * DSA_KC, DSA_KC)
        bias = sc_s[pl.ds(r0, DSA_KC), :]
        bias_g = jnp.concatenate([bias] * A_GROUP, axis=1)
        s = {g: _dot(k_ref[pl.ds(r0, DSA_KC), g * A_HEAD_DIM:(g + 1) * A_HEAD_DIM], qg_s[g]) + bias_g for g in groups}
        m_old = {g: m_s[g] for g in groups}
        m_new = {g: jnp.maximum(m_old[g], jnp.max(s[g], axis=0, keepdims=True)) for g in groups}
        alpha = {g: jnp.exp2(m_old[g] - m_new[g]) for g in groups}
        p = {g: _bf(jnp.exp2(s[g] - m_new[g])) for g in groups}
        pv = {g: _dot(vt_ref[g * DSA_VROWS:(g + 1) * DSA_VROWS, pl.ds(r0, DSA_KC)], p[g]) for g in groups}
        acc = {g: alpha[g] * acc_s[g] + pv[g] for g in groups}
        for g in groups:
            acc_s[g] = acc[g]
            m_s[g] = m_new[g]
        return carry

    for groups in DSA_GROUP_SETS:
        lax.fori_loop(0, n_chunks, functools.partial(attn_chunk, groups=groups), 0)
    for g in range(A_KV_HEADS):
        acc = acc_s[g]
        o_t = acc[0:A_HEAD_DIM, :] / acc[A_HEAD_DIM:A_HEAD_DIM + 1, :]
        for j in range(A_GROUP):
            h = g * A_GROUP + j
            o_ref[:, h * A_HEAD_DIM:(h + 1) * A_HEAD_DIM] = _bf(o_t[:, j * DSA_TQ:(j + 1) * DSA_TQ].T)


def _dsa_attention(q_t, k_r, v_t, qi_t, ki_r, w_t, n_top):
    L = k_r.shape[0]
    n_q = A_HEADS * A_HEAD_DIM
    n_kv = A_KV_HEADS * A_HEAD_DIM
    n_qi = IDX_HEADS * IDX_DIM
    assert L % DSA_KC == 0
    resident = dict(pipeline_mode=pl.Buffered(1))
    return pl.pallas_call(
        functools.partial(_dsa_kernel, n_top=n_top),
        out_shape=jax.ShapeDtypeStruct((L, n_q), MXU_DTYPE),
        grid=(L // DSA_TQ,),
        in_specs=[
            pl.BlockSpec((n_q, DSA_TQ), lambda i: (0, i)),
            pl.BlockSpec((n_qi, DSA_TQ), lambda i: (0, i)),
            pl.BlockSpec((LANES, DSA_TQ), lambda i: (0, i)),
            pl.BlockSpec((L, LANES), lambda i: (0, 0), **resident),
            pl.BlockSpec((L, n_kv), lambda i: (0, 0), **resident),
            pl.BlockSpec((A_KV_HEADS * DSA_VROWS, L), lambda i: (0, 0), **resident),
        ],
        out_specs=pl.BlockSpec((DSA_TQ, n_q), lambda i: (i, 0)),
        scratch_shapes=[
            pltpu.VMEM((IDX_DIM, IDX_HEADS * DSA_TQ), MXU_DTYPE),
            pltpu.VMEM((A_KV_HEADS, A_HEAD_DIM, A_GROUP * DSA_TQ), MXU_DTYPE),
            pltpu.VMEM((L, DSA_TQ), jnp.float32),
            pltpu.VMEM((L, DSA_TQ), jnp.int32),
            pltpu.VMEM((A_KV_HEADS, 1, A_GROUP * DSA_TQ), jnp.float32),
            pltpu.VMEM((A_KV_HEADS, DSA_VROWS, A_GROUP * DSA_TQ), jnp.float32),
        ],
        compiler_params=_cparams(("arbitrary",)),
        name="dsa_attention",
    )(q_t, qi_t, w_t, ki_r, k_r, v_t)


def _rope_tables(L, n_half, period, theta_pow_dim):
    inv = ROPE_THETA ** (-jnp.arange(0, theta_pow_dim, 2, dtype=jnp.float32) / theta_pow_dim)
    ang = jnp.arange(L, dtype=jnp.float32)[:, None] * inv[None, :]
    cos, sin = jnp.cos(ang), jnp.sin(ang)
    pad = period - 2 * n_half
    tc = jnp.concatenate([cos, cos, jnp.ones((L, pad), jnp.float32)], axis=1)
    s1 = jnp.concatenate([-sin, jnp.zeros((L, period - n_half), jnp.float32)], axis=1)
    s2 = jnp.concatenate([jnp.zeros((L, n_half), jnp.float32), sin, jnp.zeros((L, pad), jnp.float32)], axis=1)
    reps = LANES // period
    return jnp.stack([jnp.tile(t, (1, reps)) for t in (tc, s1, s2)])


RWKV_CHUNK = 64
RWKV_SLAB = 1024
RWKV_TB = 256
P_MU_R, P_MU_K, P_MU_V, P_W0, P_A0, P_KK, P_KA, P_RK, P_LNW, P_LNB = range(10)


def _head_ones():
    r = lax.broadcasted_iota(jnp.int32, (LANES, LANES), 0) // RWKV_HEAD
    c = lax.broadcasted_iota(jnp.int32, (LANES, LANES), 1) // RWKV_HEAD
    return jnp.where(r == c, 1.0, 0.0).astype(MXU_DTYPE)


def _head_sum(x, ones_bd):
    hi = _bf(x)
    lo = _bf(x - hi.astype(jnp.float32))
    tiles = [_dot(hi[:, t * LANES:(t + 1) * LANES], ones_bd) + _dot(lo[:, t * LANES:(t + 1) * LANES], ones_bd)
             for t in range(x.shape[1] // LANES)]
    return jnp.concatenate(tiles, axis=1)


def _pair_stack(x):
    lane = lax.broadcasted_iota(jnp.int32, x.shape, 1)
    return jnp.concatenate([jnp.where(lane < RWKV_HEAD, x, 0.0), jnp.where(lane >= RWKV_HEAD, x, 0.0)], axis=0)


def _rwkv_kernel(zr_ref, zrx_ref, zk_ref, zkx_ref, zv_ref, zvx_ref, zs_ref,
                 hr_ref, hrx_ref, hk_ref, hkx_ref, hv_ref, hvx_ref, hs_ref, p_ref, mus_ref, lora_ref,
                 o_ref, h_s, r_s, lw_s, k2_s, v_s, kk_s, a_s, g_s, bon_s, oo_s):
    C = RWKV_CHUNK
    tb = zr_ref.shape[0]
    i = pl.program_id(1)

    @pl.when(i == 0)
    def _():
        h_s[...] = jnp.zeros_like(h_s)

    row = lax.broadcasted_iota(jnp.int32, (tb, 1), 0)
    has_prev = jnp.where(i > 0, 1.0, 0.0)

    def unskew(main_ref, next_ref):
        x = jnp.concatenate([main_ref[...], next_ref[...]], axis=1)
        return pltpu.roll(x, x.shape[1] - ZE_SKEW, 1)[:, 0:main_ref.shape[1]]

    def mix(z, halo, mu):
        prev = halo[SUBLANES - 1:SUBLANES, :] * has_prev
        shifted = jnp.where(row == 0, prev, pltpu.roll(z, 1, 0))
        return z + (shifted - z) * mu

    r = mix(unskew(zr_ref, zrx_ref), unskew(hr_ref, hrx_ref), p_ref[P_MU_R:P_MU_R + 1, :])
    k = mix(unskew(zk_ref, zkx_ref), unskew(hk_ref, hkx_ref), p_ref[P_MU_K:P_MU_K + 1, :])
    v = mix(unskew(zv_ref, zvx_ref), unskew(hv_ref, hvx_ref), p_ref[P_MU_V:P_MU_V + 1, :])
    sm = mix(zs_ref[...], hs_ref[...], mus_ref[...])
    wd, ad, gd = sm[:, 0:LANES], sm[:, LANES:2 * LANES], sm[:, 2 * LANES:4 * LANES]
    x = -(p_ref[P_W0:P_W0 + 1, :] + _dot_x3(jnp.tanh(wd), lora_ref[0:LANES, :]))
    softplus = jnp.maximum(x, 0.0) + jnp.log(1.0 + jnp.exp(-jnp.abs(x)))
    lw = -jnp.exp(-softplus - 0.5)
    a = _sigmoid(p_ref[P_A0:P_A0 + 1, :] + _dot(_bf(ad), _bf(lora_ref[LANES:2 * LANES, :])))
    g_s[...] = _dot(_bf(_sigmoid(gd)), _bf(lora_ref[2 * LANES:4 * LANES, :]))
    ones_bd = _head_ones()
    kk = k * p_ref[P_KK:P_KK + 1, :]
    kk = kk * lax.rsqrt(jnp.maximum(_head_sum(kk * kk, ones_bd), 1e-24))
    k2 = k * (1.0 + (a - 1.0) * p_ref[P_KA:P_KA + 1, :])
    bon_s[...] = _head_sum(r * k2 * p_ref[P_RK:P_RK + 1, :], ones_bd)
    r_s[...] = r
    lw_s[...] = lw
    k2_s[...] = k2
    v_s[...] = v
    kk_s[...] = kk
    a_s[...] = a

    n2 = 2 * C
    rr = lax.broadcasted_iota(jnp.int32, (n2, n2), 0)
    cc = lax.broadcasted_iota(jnp.int32, (n2, n2), 1)
    strict = (cc % C) < (rr % C)
    incl = (cc % C) <= (rr % C)
    eye = jnp.where(rr == cc, 1.0, 0.0)
    tril_c = jnp.where(lax.broadcasted_iota(jnp.int32, (C, C), 1) <= lax.broadcasted_iota(jnp.int32, (C, C), 0),
                       1.0, 0.0).astype(MXU_DTYPE)

    n_pairs = RWKV_SLAB // LANES
    pairs = range(n_pairs)

    def chunk(c, carry):
        r0 = pl.multiple_of(c * C, C)

        def ld(ref):
            return [ref[pl.ds(r0, C), p * LANES:(p + 1) * LANES] for p in pairs]

        rc, lwc, kc, vc, kkc, ac = ld(r_s), ld(lw_s), ld(k2_s), ld(v_s), ld(kk_s), ld(a_s)
        h_old = [h_s[p] for p in pairs]
        cum = []
        for p in pairs:
            l1, l2, l3 = _split3(lwc[p])
            cum.append(_dot(tril_c, l1) + _dot(tril_c, l2) + _dot(tril_c, l3))
        cl = [cum[p][C - 1:C, :] for p in pairs]
        e_neg = [jnp.exp(-cum[p]) for p in pairs]
        e_end = [jnp.exp(cl[p] - cum[p]) for p in pairs]
        beta = [kkc[p] * ac[p] for p in pairs]
        ae_s = [_pair_stack(-kkc[p] * jnp.exp(cum[p] - lwc[p])) for p in pairs]
        rp_s = [_pair_stack(rc[p] * jnp.exp(cum[p])) for p in pairs]
        bm_s = [_pair_stack(beta[p] * e_neg[p]) for p in pairs]
        km_s = [_pair_stack(kc[p] * e_neg[p]) for p in pairs]
        bt_t = [_bf(_pair_stack(beta[p] * e_end[p]).T) for p in pairs]
        kt_t = [_bf(_pair_stack(kc[p] * e_end[p]).T) for p in pairs]
        v_st = [_bf(_pair_stack(vc[p])) for p in pairs]
        aa = [_dot_nt(_bf(jnp.concatenate([ae_s[p], rp_s[p]], axis=0)),
                      _bf(jnp.concatenate([bm_s[p], km_s[p]], axis=0))) for p in pairs]
        a_ab = [jnp.where(strict, aa[p][0:n2, 0:n2], 0.0) for p in pairs]
        a_ak = [_bf(jnp.where(strict, aa[p][0:n2, n2:2 * n2], 0.0)) for p in pairs]
        a_rb = [_bf(jnp.where(incl, aa[p][n2:2 * n2, 0:n2], 0.0)) for p in pairs]
        a_rk = [_bf(jnp.where(incl, aa[p][n2:2 * n2, n2:2 * n2], 0.0)) for p in pairs]
        t_inv = [eye + a_ab[p] for p in pairs]
        n_b = [_bf(a_ab[p]) for p in pairs]
        n_pow = [_dot(n_b[p], n_b[p]) for p in pairs]
        for _ in range(int(math.log2(C)) - 2):
            n_b = [_bf(n_pow[p]) for p in pairs]
            both = [_dot(jnp.concatenate([_bf(t_inv[p]), n_b[p]], axis=0), n_b[p]) for p in pairs]
            t_inv = [t_inv[p] + both[p][0:n2, :] for p in pairs]
            n_pow = [both[p][n2:2 * n2, :] for p in pairs]
        t_inv = [t_inv[p] + _dot(_bf(t_inv[p]), _bf(n_pow[p])) for p in pairs]
        t_b = [_bf(t_inv[p]) for p in pairs]
        w_k = [_bf(_dot(a_ak[p], v_st[p])) for p in pairs]
        t_rhs = [_dot(t_b[p], jnp.concatenate([_bf(ae_s[p]), w_k[p]], axis=1)) for p in pairs]
        a_til = [_bf(t_rhs[p][:, 0:LANES]) for p in pairs]
        v_til = [t_rhs[p][:, LANES:2 * LANES] for p in pairs]
        o_intra = [_dot(a_rk[p], v_st[p]) for p in pairs]
        h_kv = [_dot(kt_t[p], v_st[p]) for p in pairs]
        decay_col = [jnp.exp(jnp.broadcast_to(cl[p], (LANES, LANES)).T) for p in pairs]
        h_b = [_bf(h_old[p]) for p in pairs]
        u_b = [_bf(_dot(a_til[p], h_b[p]) + v_til[p]) for p in pairs]
        o_st = [_dot(_bf(rp_s[p]), h_b[p]) + _dot(a_rb[p], u_b[p]) + o_intra[p] for p in pairs]
        h_new = [decay_col[p] * h_old[p] + _dot(bt_t[p], u_b[p]) + h_kv[p] for p in pairs]
        for p in pairs:
            h_s[p] = h_new[p]
            oo_s[pl.ds(r0, C), p * LANES:(p + 1) * LANES] = o_st[p][0:C, :] + o_st[p][C:n2, :]
        return carry

    lax.fori_loop(0, tb // C, chunk, 0)

    o = oo_s[...]
    mean = _head_sum(o, ones_bd) * (1.0 / RWKV_HEAD)
    d = o - mean
    var = _head_sum(d * d, ones_bd) * (1.0 / RWKV_HEAD)
    y = d * lax.rsqrt(var + RWKV_GN_EPS) * p_ref[P_LNW:P_LNW + 1, :] + p_ref[P_LNB:P_LNB + 1, :]
    y = y + bon_s[...] * v_s[...]
    o_ref[...] = (y * g_s[...]).astype(o_ref.dtype)


def _rwkv_params(mu, w0, w_up, a0, a_up, g_up, k_k, k_a, r_k, ln_w, ln_b):
    d = RWKV_DIM
    mu_r, mu_k, mu_v = mu[0:d], mu[d:2 * d], mu[2 * d:3 * d]
    o = 3 * d
    mu_wd, mu_ad, mu_gd = mu[o:o + DECAY_RANK], mu[o + DECAY_RANK:o + DECAY_RANK + AAA_RANK], mu[o + DECAY_RANK + AAA_RANK:]
    rows = [mu_r, mu_k, mu_v, w0, a0, k_k, k_a, r_k.reshape(d), ln_w, ln_b]
    p_rows = jnp.concatenate([jnp.stack(rows), jnp.zeros((16 - len(rows), d), jnp.float32)], axis=0)

    def pad_to(x, n, axis):
        widths = [(0, 0)] * x.ndim
        widths[axis] = (0, n - x.shape[axis])
        return jnp.pad(x, widths)

    mu_small = jnp.concatenate([pad_to(mu_wd, LANES, 0), pad_to(mu_ad, LANES, 0), mu_gd]).reshape(1, 4 * LANES)
    lora = jnp.concatenate([pad_to(w_up, LANES, 0), pad_to(a_up, LANES, 0), g_up], axis=0)
    return p_rows, mu_small, lora


def _rwkv_mix(z, z_lora, p_rows, mu_small, lora):
    L = z.shape[0]
    tb, w = RWKV_TB, RWKV_SLAB
    assert L % tb == 0
    hb = tb // SUBLANES
    tiles = w // LANES

    def prev_rows(i):
        return jnp.maximum(i * hb - 1, 0)

    def windows(col0, rows, row_map):
        base = col0 - ZE_SKEW
        assert base % w == 0
        return [pl.BlockSpec((rows, w), lambda s, i: (row_map(i), base // w + s)),
                pl.BlockSpec((rows, LANES), lambda s, i: (row_map(i), base // LANES + (s + 1) * tiles))]

    body = [spec for c in (ZE_R, ZE_RK, ZE_RV) for spec in windows(c, tb, lambda i: i)]
    halo = [spec for c in (ZE_R, ZE_RK, ZE_RV) for spec in windows(c, SUBLANES, prev_rows)]
    f32 = jnp.float32
    return pl.pallas_call(
        _rwkv_kernel,
        out_shape=jax.ShapeDtypeStruct((L, RWKV_DIM), MXU_DTYPE),
        grid=(RWKV_DIM // w, L // tb),
        in_specs=body + [pl.BlockSpec((tb, ZS_COLS), lambda s, i: (i, 0))]
        + halo + [pl.BlockSpec((SUBLANES, ZS_COLS), lambda s, i: (prev_rows(i), 0))]
        + [
            pl.BlockSpec((16, w), lambda s, i: (0, s)),
            pl.BlockSpec((1, ZS_COLS), lambda s, i: (0, 0)),
            pl.BlockSpec((ZS_COLS, w), lambda s, i: (0, s)),
        ],
        out_specs=pl.BlockSpec((tb, w), lambda s, i: (i, s)),
        scratch_shapes=[pltpu.VMEM((w // LANES, LANES, LANES), f32)] + [pltpu.VMEM((tb, w), f32) for _ in range(9)],
        compiler_params=_cparams(("parallel", "arbitrary")),
        name="rwkv7_mix",
    )(*([z] * 6), z_lora, *([z] * 6), z_lora, p_rows, mu_small, lora)


S5_TB = 512
S5_SEG = SUBLANES
S5_SLAB_GROUPS = LANES // S5_GROUP
S5_SW = S5_SLAB_GROUPS * S5_STATE


def _cmul_add(ar, ai, xr, xi, br, bi):
    return ar * xr - ai * xi + br, ar * xi + ai * xr + bi


def _s5_kernel(u_ref, w_ref, c_ref, apow_ref, d_ref, zg_ref, zgb_ref, st_s, up_s, bu_s, x_s, y_s):
    tb = u_ref.shape[0]
    ts = tb // S5_SEG
    sw = S5_SW
    i = pl.program_id(1)

    @pl.when(i == 0)
    def _():
        st_s[...] = jnp.zeros_like(st_s)

    for tau in range(ts):
        up_s[tau * S5_SEG:(tau + 1) * S5_SEG, :] = u_ref[pl.ds(tau, S5_SEG, stride=ts), :]
    bu_s[...] = _dot(_bf(up_s[...]), _bf(w_ref[0]))
    a1 = apow_ref[0, 0:1, :]
    ar = jnp.broadcast_to(a1[:, 0:sw], (S5_SEG, sw))
    ai = jnp.broadcast_to(a1[:, sw:2 * sw], (S5_SEG, sw))

    def scan(tau, x):
        r0 = pl.multiple_of(tau * S5_SEG, S5_SEG)
        b = bu_s[pl.ds(r0, S5_SEG), :]
        nr, ni = _cmul_add(ar, ai, x[0], x[1], b[:, 0:sw], b[:, sw:2 * sw])
        x_s[pl.ds(r0, S5_SEG), :] = jnp.concatenate([nr, ni], axis=1)
        return nr, ni

    zero = jnp.zeros((S5_SEG, sw), jnp.float32)
    xr, xi = lax.fori_loop(0, ts, scan, (zero, zero), unroll=4)

    a_ts = apow_ref[0, ts - 1:ts, :]
    tr, ti = a_ts[:, 0:sw], a_ts[:, sw:2 * sw]
    cr, ci = st_s[:, 0:sw], st_s[:, sw:2 * sw]
    ent_r, ent_i = [], []
    for s in range(S5_SEG):
        ent_r.append(cr)
        ent_i.append(ci)
        cr, ci = _cmul_add(tr, ti, cr, ci, xr[s:s + 1, :], xi[s:s + 1, :])
    st_s[...] = jnp.concatenate([cr, ci], axis=1)
    er = jnp.concatenate(ent_r, axis=0)
    ei = jnp.concatenate(ent_i, axis=0)

    def fix(tau, carry):
        r0 = pl.multiple_of(tau * S5_SEG, S5_SEG)
        ap = apow_ref[0, pl.ds(tau, 1), :]
        x = x_s[pl.ds(r0, S5_SEG), :]
        nr, ni = _cmul_add(ap[:, 0:sw], ap[:, sw:2 * sw], er, ei, x[:, 0:sw], x[:, sw:2 * sw])
        x_s[pl.ds(r0, S5_SEG), :] = jnp.concatenate([nr, ni], axis=1)
        return carry

    lax.fori_loop(0, ts, fix, 0, unroll=4)
    yp = _dot(_bf(x_s[...]), _bf(c_ref[0]))
    for tau in range(ts):
        y_s[pl.ds(tau, S5_SEG, stride=ts), :] = yp[tau * S5_SEG:(tau + 1) * S5_SEG, :]
    y = y_s[...] + d_ref[...] * u_ref[...]
    zg = 0.5 * y * (1.0 + jnp.tanh(math.sqrt(2.0 / math.pi) * (y + 0.044715 * (y * y * y))))
    zg_ref[...] = zg
    zgb_ref[...] = zg.astype(zgb_ref.dtype)


def _s5_params(lam_re, lam_im, log_step, b_re, b_im, c_re, c_im, n_pow):
    lr = jnp.minimum(lam_re, -1e-4)
    li = lam_im
    step = jnp.exp(log_step)[:, None]
    mag = jnp.exp(lr * step)
    abar_r = mag * jnp.cos(li * step)
    abar_i = mag * jnp.sin(li * step)
    den = lr * lr + li * li
    cr = (lr * (abar_r - 1.0) + li * abar_i) / den
    ci = (lr * abar_i - li * (abar_r - 1.0)) / den
    bbar_r = cr[..., None] * b_re - ci[..., None] * b_im
    bbar_i = cr[..., None] * b_im + ci[..., None] * b_re
    ns = S5_GROUPS // S5_SLAB_GROUPS
    eye = jnp.eye(S5_SLAB_GROUPS, dtype=jnp.float32)

    def in_mat(bb):
        t = jnp.einsum('ab,sapi->saibp', eye, bb.reshape(ns, S5_SLAB_GROUPS, S5_STATE, S5_GROUP))
        return t.reshape(ns, LANES, S5_SW)

    def out_mat(cc):
        t = jnp.einsum('ab,saop->sapbo', eye, cc.reshape(ns, S5_SLAB_GROUPS, S5_GROUP, S5_STATE))
        return t.reshape(ns, S5_SW, LANES)

    w_in = jnp.concatenate([in_mat(bbar_r), in_mat(bbar_i)], axis=2)
    w_out = jnp.concatenate([out_mat(c_re), -out_mat(c_im)], axis=1)
    n = jnp.arange(1, n_pow + 1, dtype=jnp.float32)[None, :, None]
    lrs = (lr * step).reshape(ns, 1, S5_SW)
    lis = (li * step).reshape(ns, 1, S5_SW)
    pm = jnp.exp(n * lrs)
    apow = jnp.concatenate([pm * jnp.cos(n * lis), pm * jnp.sin(n * lis)], axis=2)
    return _bf(w_in), _bf(w_out), apow


def _s5_mix(z, w_in, w_out, apow, d_skip):
    L = z.shape[0]
    tb = S5_TB
    assert L % tb == 0
    ns = S5_DIM // LANES
    f32 = jnp.float32
    return pl.pallas_call(
        _s5_kernel,
        out_shape=(jax.ShapeDtypeStruct((L, S5_DIM), f32), jax.ShapeDtypeStruct((L, S5_DIM), MXU_DTYPE)),
        grid=(ns, L // tb),
        in_specs=[
            pl.BlockSpec((tb, LANES), lambda s, i: (i, s)),
            pl.BlockSpec((1, LANES, 2 * S5_SW), lambda s, i: (s, 0, 0)),
            pl.BlockSpec((1, 2 * S5_SW, LANES), lambda s, i: (s, 0, 0)),
            pl.BlockSpec((1, tb // S5_SEG, 2 * S5_SW), lambda s, i: (s, 0, 0)),
            pl.BlockSpec((1, LANES), lambda s, i: (0, s)),
        ],
        out_specs=(pl.BlockSpec((tb, LANES), lambda s, i: (i, s)), pl.BlockSpec((tb, LANES), lambda s, i: (i, s))),
        scratch_shapes=[
            pltpu.VMEM((1, 2 * S5_SW), f32),
            pltpu.VMEM((tb, LANES), f32),
            pltpu.VMEM((tb, 2 * S5_SW), f32),
            pltpu.VMEM((tb, 2 * S5_SW), f32),
            pltpu.VMEM((tb, LANES), f32),
        ],
        compiler_params=_cparams(("parallel", "arbitrary")),
        name="s5_mix",
    )(z, w_in, w_out, apow, d_skip.reshape(1, S5_DIM))


RET_CHUNK = 512


def _ret_kernel(q_ref, k_ref, v_ref, g_ref, cos_ref, sin_ref, intra_ref, rowdec_ref, o_ref, s_s):
    c = pl.program_id(1)

    @pl.when(c == 0)
    def _():
        s_s[...] = jnp.zeros_like(s_s)

    half = RET_HEAD_DIM // 2
    cos, sin = cos_ref[...], sin_ref[...]

    def rot(x):
        x1, x2 = x[:, 0:half], x[:, half:2 * half]
        return jnp.concatenate([x1 * cos - x2 * sin, x1 * sin + x2 * cos], axis=1)

    q = rot(q_ref[...])
    k = rot(k_ref[...]) * (RET_HEAD_DIM ** -0.5)
    vb = _bf(v_ref[...])
    qb = _bf(q)
    dec = rowdec_ref[0]

    def lanes2(x):
        return jnp.concatenate([x, x], axis=1)

    xi, zeta, g_chunk = lanes2(dec[:, 0:LANES]), lanes2(dec[:, LANES:2 * LANES]), lanes2(dec[:, 2 * LANES:3 * LANES])
    att = _dot_nt(qb, _bf(k)) * intra_ref[0]
    s_old = s_s[...]
    o = _dot(_bf(att), vb) + _dot(qb, _bf(s_old)) * xi
    s_s[...] = s_old * g_chunk[0:1, :] + _dot(_bf((k * zeta).T), vb)
    mean = jnp.mean(o, axis=-1, keepdims=True)
    d = o - mean
    var = jnp.mean(d * d, axis=-1, keepdims=True)
    gate = g_ref[...]
    o_ref[...] = (gate * _sigmoid(gate) * (d * lax.rsqrt(var + RET_GN_EPS))).astype(o_ref.dtype)


def _ret_tables(L):
    C = RET_CHUNK
    inv = 1.0 / (RET_ROPE_BASE ** jnp.linspace(0.0, 1.0, RET_HEAD_DIM // 2, dtype=jnp.float32))
    ang = jnp.arange(L, dtype=jnp.float32)[:, None] * inv[None, :]
    log_g = jnp.log(1.0 - 2.0 ** (-5.0 - jnp.arange(RET_HEADS, dtype=jnp.float32)))
    pos = jnp.arange(C, dtype=jnp.float32)
    diff = pos[:, None] - pos[None, :]
    intra = jnp.where(diff >= 0, jnp.exp(jnp.maximum(diff, 0.0)[None] * log_g[:, None, None]), 0.0)
    xi = jnp.exp((pos + 1.0)[None, :] * log_g[:, None])
    zeta = jnp.exp((C - 1.0 - pos)[None, :] * log_g[:, None])
    g_chunk = jnp.broadcast_to(jnp.exp(C * log_g)[:, None], (RET_HEADS, C))
    rowdec = jnp.concatenate([jnp.broadcast_to(t[:, :, None], (RET_HEADS, C, LANES)) for t in (xi, zeta, g_chunk)], axis=2)
    return jnp.cos(ang), jnp.sin(ang), intra, rowdec


def _retention_mix(z, cos, sin, intra, rowdec):
    L = z.shape[0]
    C, hd = RET_CHUNK, RET_HEAD_DIM
    assert L % C == 0
    base = S5_DIM // hd

    def blk(j):
        return pl.BlockSpec((C, hd), lambda h, c: (c, base + j * RET_HEADS + h))

    return pl.pallas_call(
        _ret_kernel,
        out_shape=jax.ShapeDtypeStruct((L, RET_HEADS * hd), MXU_DTYPE),
        grid=(RET_HEADS, L // C),
        in_specs=[
            blk(0), blk(1), blk(2), blk(3),
            pl.BlockSpec((C, hd // 2), lambda h, c: (c, 0)),
            pl.BlockSpec((C, hd // 2), lambda h, c: (c, 0)),
            pl.BlockSpec((1, C, C), lambda h, c: (h, 0, 0)),
            pl.BlockSpec((1, C, 3 * LANES), lambda h, c: (h, 0, 0)),
        ],
        out_specs=pl.BlockSpec((C, hd), lambda h, c: (c, h)),
        scratch_shapes=[pltpu.VMEM((hd, hd), jnp.float32)],
        compiler_params=_cparams(("parallel", "arbitrary")),
        name="retention_mix",
    )(z, z, z, z, cos, sin, intra, rowdec)


def _lora_w_in(w):
    d = w.shape[0]
    o_ad = ZE_LORA + DECAY_RANK
    o_gd = o_ad + AAA_RANK
    parts = [
        w[:, ZE_LORA:o_ad], jnp.zeros((d, ZS_AD - ZS_WD - DECAY_RANK), w.dtype),
        w[:, o_ad:o_gd], jnp.zeros((d, ZS_GD - ZS_AD - AAA_RANK), w.dtype),
        w[:, o_gd:o_gd + GATE_RANK],
    ]
    out = jnp.concatenate(parts, axis=1)
    assert out.shape[1] == ZS_COLS
    return out


def _ffn(h, norm_g, w_gate, w_up, w_down_b, layer):
    hn = _rmsnorm(h, norm_g, MXU_DTYPE)
    mid = _matmul([hn], [(w_gate, layer, 0), (w_up, layer, 0)], [(0, 0), (0, 1)], [], _ep_swiglu, MXU_DTYPE,
                  tm=2048, tn=256, name="ffn_gate_up")
    return _matmul([mid], [(w_down_b, layer, 0)], [(0, 0)], [("tile", h)], _ep_resid, jnp.float32,
                   tm=1024, tn=512, tk=w_down_b.shape[1] // 2, name="ffn_down")


def _out_proj(h, o_first, o_second, w_out, layer):
    return _matmul([o_first, o_second], [(w_out, layer, 0), (w_out, layer, 1)], [(0, 0), (1, 1)], [("tile", h)],
                   _ep_resid2, jnp.float32, tm=1024, tn=512, name="out_proj")


def kernel(x, norm_mix, norm_ffn, ffn_gate, ffn_up, ffn_down, e_w_in, e_w_out, e_mu, e_w0, e_w_up, e_a0, e_a_up, e_g_up, e_k_k, e_k_a, e_r_k, e_ln_w, e_ln_b, o_w_in, o_w_out, o_lam_re, o_lam_im, o_log_step, o_b_re, o_b_im, o_c_re, o_c_im, o_d_skip, o_w_glu, o_b_glu, final_norm):
    assert x.shape[0] == 1
    h = x.reshape(x.shape[1], x.shape[2])
    L = h.shape[0]
    n_top = min(TOPK_MAX, L // 4)

    ffn_down_b = _bf(ffn_down)
    hn = _rmsnorm(h, norm_mix[0], MXU_DTYPE)
    z = _matmul([hn], [(e_w_in, 0, 0)], [(0, 0)], [], _ep_plain, jnp.float32, tm=1024, tn=512, n_out=ZE_MAIN_COLS,
                name="even_in_proj")
    z_lora = _matmul([hn], [(_lora_w_in(e_w_in[0]), 0, 0)], [(0, 0)], [], _ep_plain, jnp.float32, tm=1024, tn=ZS_COLS,
                     name="even_lora_proj")
    tab_a = _rope_tables(L, A_ROT_HALF, A_HEAD_DIM, 2 * A_ROT_HALF)
    tab_i = _rope_tables(L, IDX_ROT_HALF, IDX_DIM, 2 * IDX_ROT_HALF)
    q_t, k_r, v_t, qi_t, ki_r, w_t = _dsa_prep(z, tab_a, tab_i)
    o_a = _dsa_attention(q_t, k_r, v_t, qi_t, ki_r, w_t, n_top)
    p_rows, mu_small, lora = _rwkv_params(e_mu[0], e_w0[0], e_w_up[0], e_a0[0], e_a_up[0], e_g_up[0], e_k_k[0],
                                          e_k_a[0], e_r_k[0], e_ln_w[0], e_ln_b[0])
    o_b = _rwkv_mix(z, z_lora, p_rows, mu_small, lora)
    h = _out_proj(h, o_a, o_b, e_w_out, 0)
    h = _ffn(h, norm_ffn[0], ffn_gate, ffn_up, ffn_down_b, 0)

    hn = _rmsnorm(h, norm_mix[1], MXU_DTYPE)
    z = _matmul([hn], [(o_w_in, 0, 0)], [(0, 0)], [], _ep_plain, jnp.float32, tm=1024, tn=512, name="odd_in_proj")
    s5_in, s5_out, s5_apow = _s5_params(o_lam_re[0], o_lam_im[0], o_log_step[0], o_b_re[0], o_b_im[0], o_c_re[0],
                                        o_c_im[0], S5_TB // S5_SEG)
    zg, zg_b = _s5_mix(z, s5_in, s5_out, s5_apow, o_d_skip[0])
    o_c = _matmul([zg_b], [(o_w_glu, 0, 0)], [(0, 0)], [("tile", zg), ("row", o_b_glu[0].reshape(1, S5_DIM))], _ep_glu,
                  MXU_DTYPE, tm=1024, tn=512, name="s5_glu")
    o_d = _retention_mix(z, *_ret_tables(L))
    h = _out_proj(h, o_c, o_d, o_w_out, 0)
    h = _ffn(h, norm_ffn[1], ffn_gate, ffn_up, ffn_down_b, 1)

    return _rmsnorm(h, final_norm, jnp.float32).reshape(x.shape)
```

```python
import functools
import math

import jax
import jax.numpy as jnp
from jax import lax
from jax.experimental import pallas as pl
from jax.experimental.pallas import tpu as pltpu

NORM_EPS = 1e-6
A_HEAD_DIM = 128
A_HEADS = 16
A_KV_HEADS = 4
A_GROUP = A_HEADS // A_KV_HEADS
A_ROT_HALF = 16
IDX_HEADS = 32
IDX_DIM = 64
IDX_ROT_HALF = 8
TOPK_MAX = 256
ROPE_THETA = 500000.0
RWKV_HEAD = 64
RWKV_DIM = 2048
DECAY_RANK = 96
AAA_RANK = 96
GATE_RANK = 256
RWKV_GN_EPS = 1e-5 * RWKV_HEAD
S5_DIM = 2048
S5_GROUP = 16
S5_GROUPS = 128
S5_STATE = 64
RET_HEAD_DIM = 256
RET_HEADS = 8
RET_ROPE_BASE = 10000.0
RET_GN_EPS = 1e-5

LANES = 128
SUBLANES = 8
VMEM_LIMIT_BYTES = 56 * 2**20

MXU_DTYPE = jnp.bfloat16
NEG_BIG = -1e30
INT_MIN = -2**31

ZE_Q, ZE_K, ZE_V, ZE_QI = 0, 2048, 2560, 3072
ZE_KI = 5120
ZE_WI_LANE = IDX_DIM
ZE_R, ZE_RK, ZE_RV = 5216, 7264, 9312
ZE_LORA = 11360
ZE_SKEW = ZE_R % LANES
ZE_MAIN_COLS = 11776
ZS_WD, ZS_AD, ZS_GD, ZS_COLS = 0, 128, 256, 512


def _cparams(sem):
    return pltpu.CompilerParams(dimension_semantics=sem, vmem_limit_bytes=VMEM_LIMIT_BYTES)


def _bf(x):
    return x.astype(MXU_DTYPE)


def _dot(a, b):
    return jnp.dot(a, b, preferred_element_type=jnp.float32)


def _dot_nt(a, b):
    return lax.dot_general(a, b, (((1,), (1,)), ((), ())), preferred_element_type=jnp.float32)


def _split3(x):
    h1 = _bf(x)
    r1 = x - h1.astype(jnp.float32)
    h2 = _bf(r1)
    h3 = _bf(r1 - h2.astype(jnp.float32))
    return h1, h2, h3


def _dot_x3(a, b):
    ah = _bf(a)
    al = _bf(a - ah.astype(jnp.float32))
    bh = _bf(b)
    bl = _bf(b - bh.astype(jnp.float32))
    return _dot(ah, bh) + _dot(ah, bl) + _dot(al, bh)


def _sigmoid(x):
    return 1.0 / (1.0 + jnp.exp(-x))


def _rmsnorm_kernel(x_ref, g_ref, o_ref):
    x = x_ref[...]
    ms = jnp.mean(x * x, axis=-1, keepdims=True)
    o_ref[...] = (x * lax.rsqrt(ms + NORM_EPS) * g_ref[...]).astype(o_ref.dtype)


def _rmsnorm(x, g, out_dtype, tm=256):
    m, d = x.shape
    return pl.pallas_call(
        _rmsnorm_kernel,
        out_shape=jax.ShapeDtypeStruct((m, d), out_dtype),
        grid=(m // tm,),
        in_specs=[pl.BlockSpec((tm, d), lambda i: (i, 0)), pl.BlockSpec((1, d), lambda i: (0, 0))],
        out_specs=pl.BlockSpec((tm, d), lambda i: (i, 0)),
        compiler_params=_cparams(("parallel",)),
        name="rmsnorm",
    )(x, g.reshape(1, d))


def _mm_kernel(*refs, n_a, n_w, n_e, pairs, epilogue, nk):
    a_refs = refs[:n_a]
    w_refs = refs[n_a:n_a + n_w]
    e_refs = refs[n_a + n_w:n_a + n_w + n_e]
    o_ref = refs[n_a + n_w + n_e]
    acc_refs = refs[n_a + n_w + n_e + 1:]
    if nk == 1:
        accs = [_dot(a_refs[i][...], _bf(w_refs[j][...])) for i, j in pairs]
        o_ref[...] = epilogue(accs, [e[...] for e in e_refs]).astype(o_ref.dtype)
        return
    k = pl.program_id(2)

    @pl.when(k == 0)
    def _():
        for acc in acc_refs:
            acc[...] = jnp.zeros_like(acc)

    for acc, (i, j) in zip(acc_refs, pairs):
        acc[...] += _dot(a_refs[i][...], _bf(w_refs[j][...]))

    @pl.when(k == nk - 1)
    def _():
        o_ref[...] = epilogue([acc[...] for acc in acc_refs], [e[...] for e in e_refs]).astype(o_ref.dtype)


def _weight_spec(w, layer, k_off, tk, tn):
    if w.ndim == 3:
        return pl.BlockSpec((None, tk, tn), lambda i, j, k: (layer, k + k_off, j))
    return pl.BlockSpec((tk, tn), lambda i, j, k: (k + k_off, j))


MM_TILES = {
    "even_in_proj": (1024, 512), "even_lora_proj": (1024, 512), "odd_in_proj": (1024, 512), "s5_glu": (1024, 512),
    "out_proj": (1024, 512), "ffn_gate_up": (2048, 256), "ffn_down": (1024, 512),
}
FFN_DOWN_K_SPLITS = 2


def _matmul(a_list, w_list, pairs, extras, epilogue, out_dtype, name, tk=None, n_out=None):
    m, kdim = a_list[0].shape
    n = w_list[0][0].shape[-1] if n_out is None else n_out
    tk = kdim if tk is None else tk
    tm, tn = MM_TILES[name]
    tm = min(tm, m)
    nk = kdim // tk
    assert m % tm == 0 and n % tn == 0 and kdim % tk == 0
    in_specs = [pl.BlockSpec((tm, tk), lambda i, j, k: (i, k)) for _ in a_list]
    in_specs += [_weight_spec(w, layer, k_off, tk, tn) for w, layer, k_off in w_list]
    w_list = [w for w, _, _ in w_list]
    e_arrays = []
    for kind, arr in extras:
        e_arrays.append(arr)
        if kind == "tile":
            in_specs.append(pl.BlockSpec((tm, tn), lambda i, j, k: (i, j)))
        else:
            in_specs.append(pl.BlockSpec((1, tn), lambda i, j, k: (0, j)))
    scratch = [] if nk == 1 else [pltpu.VMEM((tm, tn), jnp.float32) for _ in pairs]
    kern = functools.partial(_mm_kernel, n_a=len(a_list), n_w=len(w_list), n_e=len(extras),
                             pairs=tuple(pairs), epilogue=epilogue, nk=nk)
    return pl.pallas_call(
        kern,
        out_shape=jax.ShapeDtypeStruct((m, n), out_dtype),
        grid=(m // tm, n // tn, nk),
        in_specs=in_specs,
        out_specs=pl.BlockSpec((tm, tn), lambda i, j, k: (i, j)),
        scratch_shapes=scratch,
        compiler_params=_cparams(("parallel", "parallel", "arbitrary")),
        name=name,
    )(*a_list, *w_list, *e_arrays)


def _ep_plain(accs, ex):
    return accs[0]


def _ep_swiglu(accs, ex):
    g = accs[0]
    return g * _sigmoid(g) * accs[1]


def _ep_resid(accs, ex):
    return ex[0] + accs[0]


def _ep_resid2(accs, ex):
    return ex[0] + (accs[0] + accs[1])


def _ep_glu(accs, ex):
    zg, b = ex
    return zg * _sigmoid(accs[0] + b)


def _rope_tile(x, tc, s1, s2, n_half):
    return x * tc + pltpu.roll(x, n_half, 1) * s2 + pltpu.roll(x, LANES - n_half, 1) * s1


DSA_VPAD = 16
DSA_VROWS = A_HEAD_DIM + DSA_VPAD


def _dsa_prep_kernel(zq_ref, zs_ref, ta_ref, ti_ref, qt_ref, k_ref, vt_ref, qit_ref, ki_ref, wt_ref):
    ta_c, ta_1, ta_2 = ta_ref[0], ta_ref[1], ta_ref[2]
    ti_c, ti_1, ti_2 = ti_ref[0], ti_ref[1], ti_ref[2]
    q_scale = A_HEAD_DIM ** -0.5 * math.log2(math.e)
    for h in range(A_HEADS):
        x = zq_ref[:, ZE_Q + h * LANES:ZE_Q + (h + 1) * LANES]
        y = _rope_tile(x, ta_c, ta_1, ta_2, A_ROT_HALF) * q_scale
        qt_ref[h * LANES:(h + 1) * LANES, :] = _bf(y.T)
    for g in range(A_KV_HEADS):
        x = zq_ref[:, ZE_K + g * LANES:ZE_K + (g + 1) * LANES]
        k_ref[:, g * LANES:(g + 1) * LANES] = _bf(_rope_tile(x, ta_c, ta_1, ta_2, A_ROT_HALF))
        v = zq_ref[:, ZE_V + g * LANES:ZE_V + (g + 1) * LANES]
        vt_ref[g * DSA_VROWS:g * DSA_VROWS + A_HEAD_DIM, :] = _bf(v.T)
        vt_ref[g * DSA_VROWS + A_HEAD_DIM:(g + 1) * DSA_VROWS, :] = jnp.ones((DSA_VPAD, v.shape[0]), MXU_DTYPE)
    for c in range(IDX_HEADS * IDX_DIM // LANES):
        x = zq_ref[:, ZE_QI + c * LANES:ZE_QI + (c + 1) * LANES]
        y = _rope_tile(x, ti_c, ti_1, ti_2, IDX_ROT_HALF)
        qit_ref[c * LANES:(c + 1) * LANES, :] = _bf(y.T)
    ki_ref[...] = _bf(_rope_tile(zs_ref[...], ti_c, ti_1, ti_2, IDX_ROT_HALF))
    w_scale = (IDX_DIM ** -0.5) * (IDX_HEADS ** -0.5)
    wt_ref[...] = (zs_ref[...] * w_scale).T


def _dsa_prep(z, tab_a, tab_i, tm=256):
    L = z.shape[0]
    n_q = A_HEADS * A_HEAD_DIM
    n_kv = A_KV_HEADS * A_HEAD_DIM
    n_qi = IDX_HEADS * IDX_DIM
    out_shape = (
        jax.ShapeDtypeStruct((n_q, L), MXU_DTYPE),
        jax.ShapeDtypeStruct((L, n_kv), MXU_DTYPE),
        jax.ShapeDtypeStruct((A_KV_HEADS * DSA_VROWS, L), MXU_DTYPE),
        jax.ShapeDtypeStruct((n_qi, L), MXU_DTYPE),
        jax.ShapeDtypeStruct((L, LANES), MXU_DTYPE),
        jax.ShapeDtypeStruct((LANES, L), jnp.float32),
    )
    return pl.pallas_call(
        _dsa_prep_kernel,
        out_shape=out_shape,
        grid=(L // tm,),
        in_specs=[
            pl.BlockSpec((tm, ZE_KI), lambda i: (i, 0)),
            pl.BlockSpec((tm, LANES), lambda i: (i, ZE_KI // LANES)),
            pl.BlockSpec((3, tm, LANES), lambda i: (0, i, 0)),
            pl.BlockSpec((3, tm, LANES), lambda i: (0, i, 0)),
        ],
        out_specs=(
            pl.BlockSpec((n_q, tm), lambda i: (0, i)),
            pl.BlockSpec((tm, n_kv), lambda i: (i, 0)),
            pl.BlockSpec((A_KV_HEADS * DSA_VROWS, tm), lambda i: (0, i)),
            pl.BlockSpec((n_qi, tm), lambda i: (0, i)),
            pl.BlockSpec((tm, LANES), lambda i: (i, 0)),
            pl.BlockSpec((LANES, tm), lambda i: (0, i)),
        ),
        compiler_params=_cparams(("parallel",)),
        name="dsa_prep",
    )(z, z, tab_a, tab_i)


DSA_TQ = 128
DSA_KC = 512
DSA_KA = 256
DSA_GROUP_SETS = ((0, 1, 2, 3),)


def _dsa_kernel(qt_ref, qit_ref, wt_ref, ki_ref, k_ref, vt_ref, o_ref,
                qi_s, qg_s, sc_s, key_s, m_s, acc_s, *, n_top):
    qb = pl.program_id(0)
    q0 = qb * DSA_TQ
    n_chunks = (q0 + DSA_TQ + DSA_KC - 1) // DSA_KC
    q_pos = q0 + lax.broadcasted_iota(jnp.int32, (1, DSA_TQ), 1)

    for h in range(IDX_HEADS):
        qi_s[:, h * DSA_TQ:(h + 1) * DSA_TQ] = qit_ref[h * IDX_DIM:(h + 1) * IDX_DIM, :]
    for g in range(A_KV_HEADS):
        for j in range(A_GROUP):
            h = g * A_GROUP + j
            qg_s[g, :, j * DSA_TQ:(j + 1) * DSA_TQ] = qt_ref[h * A_HEAD_DIM:(h + 1) * A_HEAD_DIM, :]

    def score_chunk(c, carry):
        for sub in range(DSA_KC // DSA_KA):
            r0 = pl.multiple_of(c * DSA_KC + sub * DSA_KA, DSA_KA)
            kic = ki_ref[pl.ds(r0, DSA_KA), 0:IDX_DIM]
            logits = _dot(kic, qi_s[...])
            acc = jnp.zeros((DSA_KA, DSA_TQ), jnp.float32)
            for h in range(IDX_HEADS):
                w_h = wt_ref[ZE_WI_LANE + h:ZE_WI_LANE + h + 1, :]
                acc = acc + jnp.maximum(logits[:, h * DSA_TQ:(h + 1) * DSA_TQ], 0.0) * w_h
            bits = lax.bitcast_convert_type(acc, jnp.int32)
            okey = bits ^ ((bits >> 31) & jnp.int32(0x7FFFFFFF))
            k_pos = r0 + lax.broadcasted_iota(jnp.int32, (DSA_KA, 1), 0)
            key_s[pl.ds(r0, DSA_KA), :] = jnp.where(k_pos <= q_pos, okey, jnp.int32(INT_MIN))
        return carry

    lax.fori_loop(0, n_chunks, score_chunk, 0)

    def count_keys(cand, strict):
        def body(c, cnt):
            r0 = pl.multiple_of(c * DSA_KC, DSA_KC)
            blk = key_s[pl.ds(r0, DSA_KC), :]
            hit = jnp.where((blk > cand) if strict else (blk >= cand), jnp.int32(1), jnp.int32(0))
            return cnt + jnp.sum(hit.reshape(DSA_KC // SUBLANES, SUBLANES, DSA_TQ), axis=0)
        cnt8 = lax.fori_loop(0, n_chunks, body, jnp.zeros((SUBLANES, DSA_TQ), jnp.int32))
        return jnp.sum(cnt8, axis=0, keepdims=True)

    def bisect(it, u):
        cand_u = u | lax.shift_left(jnp.int32(1), 31 - it)
        cnt = count_keys(cand_u ^ jnp.int32(INT_MIN), False)
        return jnp.where(cnt >= n_top, cand_u, u)

    thr = lax.fori_loop(0, 32, bisect, jnp.zeros((1, DSA_TQ), jnp.int32)) ^ jnp.int32(INT_MIN)
    thr = jnp.maximum(thr, jnp.int32(INT_MIN + 1))
    n_sel = count_keys(thr, False)

    def bias_plain():
        def body(c, carry):
            r0 = pl.multiple_of(c * DSA_KC, DSA_KC)
            sc_s[pl.ds(r0, DSA_KC), :] = jnp.where(key_s[pl.ds(r0, DSA_KC), :] >= thr, 0.0, NEG_BIG)
            return carry
        lax.fori_loop(0, n_chunks, body, 0)

    def bias_ties():
        need = (n_top - count_keys(thr, True)).astype(jnp.float32)
        rr = lax.broadcasted_iota(jnp.int32, (DSA_KC, DSA_KC), 0)
        cc = lax.broadcasted_iota(jnp.int32, (DSA_KC, DSA_KC), 1)
        before = jnp.where(cc < rr, 1.0, 0.0).astype(MXU_DTYPE)

        def body(c, seen):
            r0 = pl.multiple_of(c * DSA_KC, DSA_KC)
            blk = key_s[pl.ds(r0, DSA_KC), :]
            tie = jnp.where(blk == thr, 1.0, 0.0)
            rank = _dot(before, _bf(tie)) + seen
            sel = jnp.logical_or(blk > thr, jnp.logical_and(blk == thr, rank < need))
            sc_s[pl.ds(r0, DSA_KC), :] = jnp.where(sel, 0.0, NEG_BIG)
            return seen + jnp.sum(tie, axis=0, keepdims=True)
        lax.fori_loop(0, n_chunks, body, jnp.zeros((1, DSA_TQ), jnp.float32))

    lax.cond(jnp.max(n_sel) > n_top, bias_ties, bias_plain)

    m_s[...] = jnp.full(m_s.shape, NEG_BIG, jnp.float32)
    acc_s[...] = jnp.zeros(acc_s.shape, jnp.float32)

    def attn_chunk(c, carry, groups):
        r0 = pl.multiple_of(c * DSA_KC, DSA_KC)
        bias = sc_s[pl.ds(r0, DSA_KC), :]
        bias_g = jnp.concatenate([bias] * A_GROUP, axis=1)
        s = {g: _dot(k_ref[pl.ds(r0, DSA_KC), g * A_HEAD_DIM:(g + 1) * A_HEAD_DIM], qg_s[g]) + bias_g for g in groups}
        m_old = {g: m_s[g] for g in groups}
        m_new = {g: jnp.maximum(m_old[g], jnp.max(s[g], axis=0, keepdims=True)) for g in groups}
        alpha = {g: jnp.exp2(m_old[g] - m_new[g]) for g in groups}
        p = {g: _bf(jnp.exp2(s[g] - m_new[g])) for g in groups}
        pv = {g: _dot(vt_ref[g * DSA_VROWS:(g + 1) * DSA_VROWS, pl.ds(r0, DSA_KC)], p[g]) for g in groups}
        acc = {g: alpha[g] * acc_s[g] + pv[g] for g in groups}
        for g in groups:
            acc_s[g] = acc[g]
            m_s[g] = m_new[g]
        return carry

    for groups in DSA_GROUP_SETS:
        lax.fori_loop(0, n_chunks, functools.partial(attn_chunk, groups=groups), 0)
    for g in range(A_KV_HEADS):
        acc = acc_s[g]
        o_t = acc[0:A_HEAD_DIM, :] / acc[A_HEAD_DIM:A_HEAD_DIM + 1, :]
        for j in range(A_GROUP):
            h = g * A_GROUP + j
            o_ref[:, h * A_HEAD_DIM:(h + 1) * A_HEAD_DIM] = _bf(o_t[:, j * DSA_TQ:(j + 1) * DSA_TQ].T)


def _dsa_attention(q_t, k_r, v_t, qi_t, ki_r, w_t, n_top):
    L = k_r.shape[0]
    n_q = A_HEADS * A_HEAD_DIM
    n_kv = A_KV_HEADS * A_HEAD_DIM
    n_qi = IDX_HEADS * IDX_DIM
    assert L % DSA_KC == 0
    resident = dict(pipeline_mode=pl.Buffered(1))
    return pl.pallas_call(
        functools.partial(_dsa_kernel, n_top=n_top),
        out_shape=jax.ShapeDtypeStruct((L, n_q), MXU_DTYPE),
        grid=(L // DSA_TQ,),
        in_specs=[
            pl.BlockSpec((n_q, DSA_TQ), lambda i: (0, i)),
            pl.BlockSpec((n_qi, DSA_TQ), lambda i: (0, i)),
            pl.BlockSpec((LANES, DSA_TQ), lambda i: (0, i)),
            pl.BlockSpec((L, LANES), lambda i: (0, 0), **resident),
            pl.BlockSpec((L, n_kv), lambda i: (0, 0), **resident),
            pl.BlockSpec((A_KV_HEADS * DSA_VROWS, L), lambda i: (0, 0), **resident),
        ],
        out_specs=pl.BlockSpec((DSA_TQ, n_q), lambda i: (i, 0)),
        scratch_shapes=[
            pltpu.VMEM((IDX_DIM, IDX_HEADS * DSA_TQ), MXU_DTYPE),
            pltpu.VMEM((A_KV_HEADS, A_HEAD_DIM, A_GROUP * DSA_TQ), MXU_DTYPE),
            pltpu.VMEM((L, DSA_TQ), jnp.float32),
            pltpu.VMEM((L, DSA_TQ), jnp.int32),
            pltpu.VMEM((A_KV_HEADS, 1, A_GROUP * DSA_TQ), jnp.float32),
            pltpu.VMEM((A_KV_HEADS, DSA_VROWS, A_GROUP * DSA_TQ), jnp.float32),
        ],
        compiler_params=_cparams(("arbitrary",)),
        name="dsa_attention",
    )(q_t, qi_t, w_t, ki_r, k_r, v_t)


def _rope_tables(L, n_half, period, theta_pow_dim):
    inv = ROPE_THETA ** (-jnp.arange(0, theta_pow_dim, 2, dtype=jnp.float32) / theta_pow_dim)
    ang = jnp.arange(L, dtype=jnp.float32)[:, None] * inv[None, :]
    cos, sin = jnp.cos(ang), jnp.sin(ang)
    pad = period - 2 * n_half
    tc = jnp.concatenate([cos, cos, jnp.ones((L, pad), jnp.float32)], axis=1)
    s1 = jnp.concatenate([-sin, jnp.zeros((L, period - n_half), jnp.float32)], axis=1)
    s2 = jnp.concatenate([jnp.zeros((L, n_half), jnp.float32), sin, jnp.zeros((L, pad), jnp.float32)], axis=1)
    reps = LANES // period
    return jnp.stack([jnp.tile(t, (1, reps)) for t in (tc, s1, s2)])


RWKV_CHUNK = 64
RWKV_SLAB = 1024
RWKV_TB = 256
P_MU_R, P_MU_K, P_MU_V, P_W0, P_A0, P_KK, P_KA, P_RK, P_LNW, P_LNB = range(10)


def _head_ones():
    r = lax.broadcasted_iota(jnp.int32, (LANES, LANES), 0) // RWKV_HEAD
    c = lax.broadcasted_iota(jnp.int32, (LANES, LANES), 1) // RWKV_HEAD
    return jnp.where(r == c, 1.0, 0.0).astype(MXU_DTYPE)


def _head_sum(x, ones_bd):
    hi = _bf(x)
    lo = _bf(x - hi.astype(jnp.float32))
    tiles = [_dot(hi[:, t * LANES:(t + 1) * LANES], ones_bd) + _dot(lo[:, t * LANES:(t + 1) * LANES], ones_bd)
             for t in range(x.shape[1] // LANES)]
    return jnp.concatenate(tiles, axis=1)


def _pair_stack(x):
    lane = lax.broadcasted_iota(jnp.int32, x.shape, 1)
    return jnp.concatenate([jnp.where(lane < RWKV_HEAD, x, 0.0), jnp.where(lane >= RWKV_HEAD, x, 0.0)], axis=0)


def _rwkv_kernel(zr_ref, zrx_ref, zk_ref, zkx_ref, zv_ref, zvx_ref, zs_ref,
                 hr_ref, hrx_ref, hk_ref, hkx_ref, hv_ref, hvx_ref, hs_ref, p_ref, mus_ref, lora_ref,
                 o_ref, h_s, r_s, lw_s, cum_s, k2_s, v_s, kk_s, a_s, g_s, bon_s, oo_s):
    C = RWKV_CHUNK
    tb = zr_ref.shape[0]
    i = pl.program_id(1)

    @pl.when(i == 0)
    def _():
        h_s[...] = jnp.zeros_like(h_s)

    row = lax.broadcasted_iota(jnp.int32, (tb, 1), 0)
    has_prev = jnp.where(i > 0, 1.0, 0.0)

    def unskew(main_ref, next_ref):
        x = jnp.concatenate([main_ref[...], next_ref[...]], axis=1)
        return pltpu.roll(x, x.shape[1] - ZE_SKEW, 1)[:, 0:main_ref.shape[1]]

    def mix(z, halo, mu):
        prev = halo[SUBLANES - 1:SUBLANES, :] * has_prev
        shifted = jnp.where(row == 0, prev, pltpu.roll(z, 1, 0))
        return z + (shifted - z) * mu

    r = mix(unskew(zr_ref, zrx_ref), unskew(hr_ref, hrx_ref), p_ref[P_MU_R:P_MU_R + 1, :])
    k = mix(unskew(zk_ref, zkx_ref), unskew(hk_ref, hkx_ref), p_ref[P_MU_K:P_MU_K + 1, :])
    v = mix(unskew(zv_ref, zvx_ref), unskew(hv_ref, hvx_ref), p_ref[P_MU_V:P_MU_V + 1, :])
    sm = mix(zs_ref[...], hs_ref[...], mus_ref[...])
    wd, ad, gd = sm[:, 0:LANES], sm[:, LANES:2 * LANES], sm[:, 2 * LANES:4 * LANES]
    x = -(p_ref[P_W0:P_W0 + 1, :] + _dot_x3(jnp.tanh(wd), lora_ref[0:LANES, :]))
    softplus = jnp.maximum(x, 0.0) + jnp.log(1.0 + jnp.exp(-jnp.abs(x)))
    lw = -jnp.exp(-softplus - 0.5)
    a = _sigmoid(p_ref[P_A0:P_A0 + 1, :] + _dot(_bf(ad), _bf(lora_ref[LANES:2 * LANES, :])))
    g_s[...] = _dot(_bf(_sigmoid(gd)), _bf(lora_ref[2 * LANES:4 * LANES, :]))
    ones_bd = _head_ones()
    kk = k * p_ref[P_KK:P_KK + 1, :]
    kk = kk * lax.rsqrt(jnp.maximum(_head_sum(kk * kk, ones_bd), 1e-24))
    k2 = k * (1.0 + (a - 1.0) * p_ref[P_KA:P_KA + 1, :])
    bon_s[...] = _head_sum(r * k2 * p_ref[P_RK:P_RK + 1, :], ones_bd)
    tr = lax.broadcasted_iota(jnp.int32, (tb, tb), 0)
    tc = lax.broadcasted_iota(jnp.int32, (tb, tb), 1)
    tril_bd = jnp.where(jnp.logical_and(tc <= tr, tc // C == tr // C), 1.0, 0.0).astype(MXU_DTYPE)
    l1, l2, l3 = _split3(lw)
    cum_s[...] = _dot(tril_bd, l1) + _dot(tril_bd, l2) + _dot(tril_bd, l3)
    r_s[...] = r
    lw_s[...] = lw
    k2_s[...] = k2
    v_s[...] = v
    kk_s[...] = kk
    a_s[...] = a

    n2 = 2 * C
    rr = lax.broadcasted_iota(jnp.int32, (n2, n2), 0)
    cc = lax.broadcasted_iota(jnp.int32, (n2, n2), 1)
    strict = (cc % C) < (rr % C)
    incl = (cc % C) <= (rr % C)
    eye = jnp.where(rr == cc, 1.0, 0.0)

    n_pairs = RWKV_SLAB // LANES
    pairs = range(n_pairs)

    def chunk(c, carry):
        r0 = pl.multiple_of(c * C, C)

        def ld(ref):
            return [ref[pl.ds(r0, C), p * LANES:(p + 1) * LANES] for p in pairs]

        rc, lwc, kc, vc, kkc, ac, cum = ld(r_s), ld(lw_s), ld(k2_s), ld(v_s), ld(kk_s), ld(a_s), ld(cum_s)
        h_old = [h_s[p] for p in pairs]
        cl = [cum[p][C - 1:C, :] for p in pairs]
        e_neg = [jnp.exp(-cum[p]) for p in pairs]
        e_end = [jnp.exp(cl[p] - cum[p]) for p in pairs]
        beta = [kkc[p] * ac[p] for p in pairs]
        ae_s = [_pair_stack(-kkc[p] * jnp.exp(cum[p] - lwc[p])) for p in pairs]
        rp_s = [_pair_stack(rc[p] * jnp.exp(cum[p])) for p in pairs]
        bm_s = [_pair_stack(beta[p] * e_neg[p]) for p in pairs]
        km_s = [_pair_stack(kc[p] * e_neg[p]) for p in pairs]
        bt_t = [_bf(_pair_stack(beta[p] * e_end[p]).T) for p in pairs]
        kt_t = [_bf(_pair_stack(kc[p] * e_end[p]).T) for p in pairs]
        v_st = [_bf(_pair_stack(vc[p])) for p in pairs]
        aa = [_dot_nt(_bf(jnp.concatenate([ae_s[p], rp_s[p]], axis=0)),
                      _bf(jnp.concatenate([bm_s[p], km_s[p]], axis=0))) for p in pairs]
        a_ab = [jnp.where(strict, aa[p][0:n2, 0:n2], 0.0) for p in pairs]
        a_ak = [_bf(jnp.where(strict, aa[p][0:n2, n2:2 * n2], 0.0)) for p in pairs]
        a_rb = [_bf(jnp.where(incl, aa[p][n2:2 * n2, 0:n2], 0.0)) for p in pairs]
        a_rk = [_bf(jnp.where(incl, aa[p][n2:2 * n2, n2:2 * n2], 0.0)) for p in pairs]
        t_inv = [eye + a_ab[p] for p in pairs]
        n_b = [_bf(a_ab[p]) for p in pairs]
        n_pow = [_dot(n_b[p], n_b[p]) for p in pairs]
        for _ in range(int(math.log2(C)) - 2):
            n_b = [_bf(n_pow[p]) for p in pairs]
            both = [_dot(jnp.concatenate([_bf(t_inv[p]), n_b[p]], axis=0), n_b[p]) for p in pairs]
            t_inv = [t_inv[p] + both[p][0:n2, :] for p in pairs]
            n_pow = [both[p][n2:2 * n2, :] for p in pairs]
        t_inv = [t_inv[p] + _dot(_bf(t_inv[p]), _bf(n_pow[p])) for p in pairs]
        t_b = [_bf(t_inv[p]) for p in pairs]
        w_k = [_bf(_dot(a_ak[p], v_st[p])) for p in pairs]
        t_rhs = [_dot(t_b[p], jnp.concatenate([_bf(ae_s[p]), w_k[p]], axis=1)) for p in pairs]
        a_til = [_bf(t_rhs[p][:, 0:LANES]) for p in pairs]
        v_til = [t_rhs[p][:, LANES:2 * LANES] for p in pairs]
        o_intra = [_dot(a_rk[p], v_st[p]) for p in pairs]
        h_kv = [_dot(kt_t[p], v_st[p]) for p in pairs]
        decay_col = [jnp.exp(jnp.broadcast_to(cl[p], (LANES, LANES)).T) for p in pairs]
        h_b = [_bf(h_old[p]) for p in pairs]
        u_b = [_bf(_dot(a_til[p], h_b[p]) + v_til[p]) for p in pairs]
        o_st = [_dot(_bf(rp_s[p]), h_b[p]) + _dot(a_rb[p], u_b[p]) + o_intra[p] for p in pairs]
        h_new = [decay_col[p] * h_old[p] + _dot(bt_t[p], u_b[p]) + h_kv[p] for p in pairs]
        for p in pairs:
            h_s[p] = h_new[p]
            oo_s[pl.ds(r0, C), p * LANES:(p + 1) * LANES] = o_st[p][0:C, :] + o_st[p][C:n2, :]
        return carry

    lax.fori_loop(0, tb // C, chunk, 0)

    o = oo_s[...]
    mean = _head_sum(o, ones_bd) * (1.0 / RWKV_HEAD)
    d = o - mean
    var = _head_sum(d * d, ones_bd) * (1.0 / RWKV_HEAD)
    y = d * lax.rsqrt(var + RWKV_GN_EPS) * p_ref[P_LNW:P_LNW + 1, :] + p_ref[P_LNB:P_LNB + 1, :]
    y = y + bon_s[...] * v_s[...]
    o_ref[...] = (y * g_s[...]).astype(o_ref.dtype)


def _rwkv_params(mu, w0, w_up, a0, a_up, g_up, k_k, k_a, r_k, ln_w, ln_b):
    d = RWKV_DIM
    mu_r, mu_k, mu_v = mu[0:d], mu[d:2 * d], mu[2 * d:3 * d]
    o = 3 * d
    mu_wd, mu_ad, mu_gd = mu[o:o + DECAY_RANK], mu[o + DECAY_RANK:o + DECAY_RANK + AAA_RANK], mu[o + DECAY_RANK + AAA_RANK:]
    rows = [mu_r, mu_k, mu_v, w0, a0, k_k, k_a, r_k.reshape(d), ln_w, ln_b]
    p_rows = jnp.concatenate([jnp.stack(rows), jnp.zeros((16 - len(rows), d), jnp.float32)], axis=0)

    def pad_to(x, n, axis):
        widths = [(0, 0)] * x.ndim
        widths[axis] = (0, n - x.shape[axis])
        return jnp.pad(x, widths)

    mu_small = jnp.concatenate([pad_to(mu_wd, LANES, 0), pad_to(mu_ad, LANES, 0), mu_gd]).reshape(1, 4 * LANES)
    lora = jnp.concatenate([pad_to(w_up, LANES, 0), pad_to(a_up, LANES, 0), g_up], axis=0)
    return p_rows, mu_small, lora


def _rwkv_mix(z, z_lora, p_rows, mu_small, lora):
    L = z.shape[0]
    tb, w = RWKV_TB, RWKV_SLAB
    assert L % tb == 0
    hb = tb // SUBLANES
    tiles = w // LANES

    def prev_rows(i):
        return jnp.maximum(i * hb - 1, 0)

    def windows(col0, rows, row_map):
        base = col0 - ZE_SKEW
        assert base % w == 0
        return [pl.BlockSpec((rows, w), lambda s, i: (row_map(i), base // w + s)),
                pl.BlockSpec((rows, LANES), lambda s, i: (row_map(i), base // LANES + (s + 1) * tiles))]

    body = [spec for c in (ZE_R, ZE_RK, ZE_RV) for spec in windows(c, tb, lambda i: i)]
    halo = [spec for c in (ZE_R, ZE_RK, ZE_RV) for spec in windows(c, SUBLANES, prev_rows)]
    f32 = jnp.float32
    return pl.pallas_call(
        _rwkv_kernel,
        out_shape=jax.ShapeDtypeStruct((L, RWKV_DIM), MXU_DTYPE),
        grid=(RWKV_DIM // w, L // tb),
        in_specs=body + [pl.BlockSpec((tb, ZS_COLS), lambda s, i: (i, 0))]
        + halo + [pl.BlockSpec((SUBLANES, ZS_COLS), lambda s, i: (prev_rows(i), 0))]
        + [
            pl.BlockSpec((16, w), lambda s, i: (0, s)),
            pl.BlockSpec((1, ZS_COLS), lambda s, i: (0, 0)),
            pl.BlockSpec((ZS_COLS, w), lambda s, i: (0, s)),
        ],
        out_specs=pl.BlockSpec((tb, w), lambda s, i: (i, s)),
        scratch_shapes=[pltpu.VMEM((w // LANES, LANES, LANES), f32)] + [pltpu.VMEM((tb, w), f32) for _ in range(10)],
        compiler_params=_cparams(("parallel", "arbitrary")),
        name="rwkv7_mix",
    )(*([z] * 6), z_lora, *([z] * 6), z_lora, p_rows, mu_small, lora)


S5_TB = 512
S5_SEG = SUBLANES
S5_SLAB_GROUPS = LANES // S5_GROUP
S5_SW = S5_SLAB_GROUPS * S5_STATE


def _cmul_add(ar, ai, xr, xi, br, bi):
    return ar * xr - ai * xi + br, ar * xi + ai * xr + bi


def _s5_kernel(u_ref, w_ref, c_ref, apow_ref, d_ref, zg_ref, zgb_ref, st_s, up_s, bu_s, x_s, y_s):
    tb = u_ref.shape[0]
    ts = tb // S5_SEG
    sw = S5_SW
    i = pl.program_id(1)

    @pl.when(i == 0)
    def _():
        st_s[...] = jnp.zeros_like(st_s)

    for tau in range(ts):
        up_s[tau * S5_SEG:(tau + 1) * S5_SEG, :] = u_ref[pl.ds(tau, S5_SEG, stride=ts), :]
    bu_s[...] = _dot(_bf(up_s[...]), _bf(w_ref[0]))
    a1 = apow_ref[0, 0:1, :]
    ar = jnp.broadcast_to(a1[:, 0:sw], (S5_SEG, sw))
    ai = jnp.broadcast_to(a1[:, sw:2 * sw], (S5_SEG, sw))

    def scan(tau, x):
        r0 = pl.multiple_of(tau * S5_SEG, S5_SEG)
        b = bu_s[pl.ds(r0, S5_SEG), :]
        nr, ni = _cmul_add(ar, ai, x[0], x[1], b[:, 0:sw], b[:, sw:2 * sw])
        x_s[pl.ds(r0, S5_SEG), :] = jnp.concatenate([nr, ni], axis=1)
        return nr, ni

    zero = jnp.zeros((S5_SEG, sw), jnp.float32)
    xr, xi = lax.fori_loop(0, ts, scan, (zero, zero), unroll=4)

    a_ts = apow_ref[0, ts - 1:ts, :]
    tr, ti = a_ts[:, 0:sw], a_ts[:, sw:2 * sw]
    cr, ci = st_s[:, 0:sw], st_s[:, sw:2 * sw]
    ent_r, ent_i = [], []
    for s in range(S5_SEG):
        ent_r.append(cr)
        ent_i.append(ci)
        cr, ci = _cmul_add(tr, ti, cr, ci, xr[s:s + 1, :], xi[s:s + 1, :])
    st_s[...] = jnp.concatenate([cr, ci], axis=1)
    er = jnp.concatenate(ent_r, axis=0)
    ei = jnp.concatenate(ent_i, axis=0)

    def fix(tau, carry):
        r0 = pl.multiple_of(tau * S5_SEG, S5_SEG)
        ap = apow_ref[0, pl.ds(tau, 1), :]
        x = x_s[pl.ds(r0, S5_SEG), :]
        nr, ni = _cmul_add(ap[:, 0:sw], ap[:, sw:2 * sw], er, ei, x[:, 0:sw], x[:, sw:2 * sw])
        x_s[pl.ds(r0, S5_SEG), :] = jnp.concatenate([nr, ni], axis=1)
        return carry

    lax.fori_loop(0, ts, fix, 0, unroll=4)
    yp = _dot(_bf(x_s[...]), _bf(c_ref[0]))
    for tau in range(ts):
        y_s[pl.ds(tau, S5_SEG, stride=ts), :] = yp[tau * S5_SEG:(tau + 1) * S5_SEG, :]
    y = y_s[...] + d_ref[...] * u_ref[...]
    zg = 0.5 * y * (1.0 + jnp.tanh(math.sqrt(2.0 / math.pi) * (y + 0.044715 * (y * y * y))))
    zg_ref[...] = zg
    zgb_ref[...] = zg.astype(zgb_ref.dtype)


def _s5_params(lam_re, lam_im, log_step, b_re, b_im, c_re, c_im, n_pow):
    lr = jnp.minimum(lam_re, -1e-4)
    li = lam_im
    step = jnp.exp(log_step)[:, None]
    mag = jnp.exp(lr * step)
    abar_r = mag * jnp.cos(li * step)
    abar_i = mag * jnp.sin(li * step)
    den = lr * lr + li * li
    cr = (lr * (abar_r - 1.0) + li * abar_i) / den
    ci = (lr * abar_i - li * (abar_r - 1.0)) / den
    bbar_r = cr[..., None] * b_re - ci[..., None] * b_im
    bbar_i = cr[..., None] * b_im + ci[..., None] * b_re
    ns = S5_GROUPS // S5_SLAB_GROUPS
    eye = jnp.eye(S5_SLAB_GROUPS, dtype=jnp.float32)

    def in_mat(bb):
        t = jnp.einsum('ab,sapi->saibp', eye, bb.reshape(ns, S5_SLAB_GROUPS, S5_STATE, S5_GROUP))
        return t.reshape(ns, LANES, S5_SW)

    def out_mat(cc):
        t = jnp.einsum('ab,saop->sapbo', eye, cc.reshape(ns, S5_SLAB_GROUPS, S5_GROUP, S5_STATE))
        return t.reshape(ns, S5_SW, LANES)

    w_in = jnp.concatenate([in_mat(bbar_r), in_mat(bbar_i)], axis=2)
    w_out = jnp.concatenate([out_mat(c_re), -out_mat(c_im)], axis=1)
    n = jnp.arange(1, n_pow + 1, dtype=jnp.float32)[None, :, None]
    lrs = (lr * step).reshape(ns, 1, S5_SW)
    lis = (li * step).reshape(ns, 1, S5_SW)
    pm = jnp.exp(n * lrs)
    apow = jnp.concatenate([pm * jnp.cos(n * lis), pm * jnp.sin(n * lis)], axis=2)
    return _bf(w_in), _bf(w_out), apow


def _s5_mix(z, w_in, w_out, apow, d_skip):
    L = z.shape[0]
    tb = S5_TB
    assert L % tb == 0
    ns = S5_DIM // LANES
    f32 = jnp.float32
    return pl.pallas_call(
        _s5_kernel,
        out_shape=(jax.ShapeDtypeStruct((L, S5_DIM), f32), jax.ShapeDtypeStruct((L, S5_DIM), MXU_DTYPE)),
        grid=(ns, L // tb),
        in_specs=[
            pl.BlockSpec((tb, LANES), lambda s, i: (i, s)),
            pl.BlockSpec((1, LANES, 2 * S5_SW), lambda s, i: (s, 0, 0)),
            pl.BlockSpec((1, 2 * S5_SW, LANES), lambda s, i: (s, 0, 0)),
            pl.BlockSpec((1, tb // S5_SEG, 2 * S5_SW), lambda s, i: (s, 0, 0)),
            pl.BlockSpec((1, LANES), lambda s, i: (0, s)),
        ],
        out_specs=(pl.BlockSpec((tb, LANES), lambda s, i: (i, s)), pl.BlockSpec((tb, LANES), lambda s, i: (i, s))),
        scratch_shapes=[
            pltpu.VMEM((1, 2 * S5_SW), f32),
            pltpu.VMEM((tb, LANES), f32),
            pltpu.VMEM((tb, 2 * S5_SW), f32),
            pltpu.VMEM((tb, 2 * S5_SW), f32),
            pltpu.VMEM((tb, LANES), f32),
        ],
        compiler_params=_cparams(("parallel", "arbitrary")),
        name="s5_mix",
    )(z, w_in, w_out, apow, d_skip.reshape(1, S5_DIM))


RET_CHUNK = 512


def _ret_kernel(q_ref, k_ref, v_ref, g_ref, cos_ref, sin_ref, intra_ref, rowdec_ref, o_ref, s_s):
    c = pl.program_id(1)

    @pl.when(c == 0)
    def _():
        s_s[...] = jnp.zeros_like(s_s)

    half = RET_HEAD_DIM // 2
    cos, sin = cos_ref[...], sin_ref[...]

    def rot(x):
        x1, x2 = x[:, 0:half], x[:, half:2 * half]
        return jnp.concatenate([x1 * cos - x2 * sin, x1 * sin + x2 * cos], axis=1)

    q = rot(q_ref[...])
    k = rot(k_ref[...]) * (RET_HEAD_DIM ** -0.5)
    vb = _bf(v_ref[...])
    qb = _bf(q)
    dec = rowdec_ref[0]

    def lanes2(x):
        return jnp.concatenate([x, x], axis=1)

    xi, zeta, g_chunk = lanes2(dec[:, 0:LANES]), lanes2(dec[:, LANES:2 * LANES]), lanes2(dec[:, 2 * LANES:3 * LANES])
    att = _dot_nt(qb, _bf(k)) * intra_ref[0]
    s_old = s_s[...]
    o = _dot(_bf(att), vb) + _dot(qb, _bf(s_old)) * xi
    s_s[...] = s_old * g_chunk[0:1, :] + _dot(_bf((k * zeta).T), vb)
    mean = jnp.mean(o, axis=-1, keepdims=True)
    d = o - mean
    var = jnp.mean(d * d, axis=-1, keepdims=True)
    gate = g_ref[...]
    o_ref[...] = (gate * _sigmoid(gate) * (d * lax.rsqrt(var + RET_GN_EPS))).astype(o_ref.dtype)


def _ret_tables(L):
    C = RET_CHUNK
    inv = 1.0 / (RET_ROPE_BASE ** jnp.linspace(0.0, 1.0, RET_HEAD_DIM // 2, dtype=jnp.float32))
    ang = jnp.arange(L, dtype=jnp.float32)[:, None] * inv[None, :]
    log_g = jnp.log(1.0 - 2.0 ** (-5.0 - jnp.arange(RET_HEADS, dtype=jnp.float32)))
    pos = jnp.arange(C, dtype=jnp.float32)
    diff = pos[:, None] - pos[None, :]
    intra = jnp.where(diff >= 0, jnp.exp(jnp.maximum(diff, 0.0)[None] * log_g[:, None, None]), 0.0)
    xi = jnp.exp((pos + 1.0)[None, :] * log_g[:, None])
    zeta = jnp.exp((C - 1.0 - pos)[None, :] * log_g[:, None])
    g_chunk = jnp.broadcast_to(jnp.exp(C * log_g)[:, None], (RET_HEADS, C))
    rowdec = jnp.concatenate([jnp.broadcast_to(t[:, :, None], (RET_HEADS, C, LANES)) for t in (xi, zeta, g_chunk)], axis=2)
    return jnp.cos(ang), jnp.sin(ang), intra, rowdec


def _retention_mix(z, cos, sin, intra, rowdec):
    L = z.shape[0]
    C, hd = RET_CHUNK, RET_HEAD_DIM
    assert L % C == 0
    base = S5_DIM // hd

    def blk(j):
        return pl.BlockSpec((C, hd), lambda h, c: (c, base + j * RET_HEADS + h))

    return pl.pallas_call(
        _ret_kernel,
        out_shape=jax.ShapeDtypeStruct((L, RET_HEADS * hd), MXU_DTYPE),
        grid=(RET_HEADS, L // C),
        in_specs=[
            blk(0), blk(1), blk(2), blk(3),
            pl.BlockSpec((C, hd // 2), lambda h, c: (c, 0)),
            pl.BlockSpec((C, hd // 2), lambda h, c: (c, 0)),
            pl.BlockSpec((1, C, C), lambda h, c: (h, 0, 0)),
            pl.BlockSpec((1, C, 3 * LANES), lambda h, c: (h, 0, 0)),
        ],
        out_specs=pl.BlockSpec((C, hd), lambda h, c: (c, h)),
        scratch_shapes=[pltpu.VMEM((hd, hd), jnp.float32)],
        compiler_params=_cparams(("parallel", "arbitrary")),
        name="retention_mix",
    )(z, z, z, z, cos, sin, intra, rowdec)


def _lora_w_in(w):
    d = w.shape[0]
    o_ad = ZE_LORA + DECAY_RANK
    o_gd = o_ad + AAA_RANK
    parts = [
        w[:, ZE_LORA:o_ad], jnp.zeros((d, ZS_AD - ZS_WD - DECAY_RANK), w.dtype),
        w[:, o_ad:o_gd], jnp.zeros((d, ZS_GD - ZS_AD - AAA_RANK), w.dtype),
        w[:, o_gd:o_gd + GATE_RANK],
    ]
    out = jnp.concatenate(parts, axis=1)
    assert out.shape[1] == ZS_COLS
    return out


def _ffn(h, norm_g, w_gate, w_up, w_down_b, layer):
    hn = _rmsnorm(h, norm_g, MXU_DTYPE)
    mid = _matmul([hn], [(w_gate, layer, 0), (w_up, layer, 0)], [(0, 0), (0, 1)], [], _ep_swiglu, MXU_DTYPE,
                  "ffn_gate_up")
    return _matmul([mid], [(w_down_b, layer, 0)], [(0, 0)], [("tile", h)], _ep_resid, jnp.float32,
                   "ffn_down", tk=w_down_b.shape[1] // FFN_DOWN_K_SPLITS)


def _out_proj(h, o_first, o_second, w_out, layer):
    return _matmul([o_first, o_second], [(w_out, layer, 0), (w_out, layer, 1)], [(0, 0), (1, 1)], [("tile", h)],
                   _ep_resid2, jnp.float32, "out_proj")


def kernel(x, norm_mix, norm_ffn, ffn_gate, ffn_up, ffn_down, e_w_in, e_w_out, e_mu, e_w0, e_w_up, e_a0, e_a_up, e_g_up, e_k_k, e_k_a, e_r_k, e_ln_w, e_ln_b, o_w_in, o_w_out, o_lam_re, o_lam_im, o_log_step, o_b_re, o_b_im, o_c_re, o_c_im, o_d_skip, o_w_glu, o_b_glu, final_norm):
    assert x.shape[0] == 1
    h = x.reshape(x.shape[1], x.shape[2])
    L = h.shape[0]
    n_top = min(TOPK_MAX, L // 4)

    ffn_down_b = _bf(ffn_down)
    hn = _rmsnorm(h, norm_mix[0], MXU_DTYPE)
    z = _matmul([hn], [(e_w_in, 0, 0)], [(0, 0)], [], _ep_plain, jnp.float32, "even_in_proj", n_out=ZE_MAIN_COLS)
    z_lora = _matmul([hn], [(_lora_w_in(e_w_in[0]), 0, 0)], [(0, 0)], [], _ep_plain, jnp.float32, "even_lora_proj")
    tab_a = _rope_tables(L, A_ROT_HALF, A_HEAD_DIM, 2 * A_ROT_HALF)
    tab_i = _rope_tables(L, IDX_ROT_HALF, IDX_DIM, 2 * IDX_ROT_HALF)
    q_t, k_r, v_t, qi_t, ki_r, w_t = _dsa_prep(z, tab_a, tab_i)
    o_a = _dsa_attention(q_t, k_r, v_t, qi_t, ki_r, w_t, n_top)
    p_rows, mu_small, lora = _rwkv_params(e_mu[0], e_w0[0], e_w_up[0], e_a0[0], e_a_up[0], e_g_up[0], e_k_k[0],
                                          e_k_a[0], e_r_k[0], e_ln_w[0], e_ln_b[0])
    o_b = _rwkv_mix(z, z_lora, p_rows, mu_small, lora)
    h = _out_proj(h, o_a, o_b, e_w_out, 0)
    h = _ffn(h, norm_ffn[0], ffn_gate, ffn_up, ffn_down_b, 0)

    hn = _rmsnorm(h, norm_mix[1], MXU_DTYPE)
    z = _matmul([hn], [(o_w_in, 0, 0)], [(0, 0)], [], _ep_plain, jnp.float32, "odd_in_proj")
    s5_in, s5_out, s5_apow = _s5_params(o_lam_re[0], o_lam_im[0], o_log_step[0], o_b_re[0], o_b_im[0], o_c_re[0],
                                        o_c_im[0], S5_TB // S5_SEG)
    zg, zg_b = _s5_mix(z, s5_in, s5_out, s5_apow, o_d_skip[0])
    o_c = _matmul([zg_b], [(o_w_glu, 0, 0)], [(0, 0)], [("tile", zg), ("row", o_b_glu[0].reshape(1, S5_DIM))], _ep_glu,
                  MXU_DTYPE, "s5_glu")
    o_d = _retention_mix(z, *_ret_tables(L))
    h = _out_proj(h, o_c, o_d, o_w_out, 0)
    h = _ffn(h, norm_ffn[1], ffn_gate, ffn_up, ffn_down_b, 1)

    return _rmsnorm(h, final_norm, jnp.float32).reshape(x.shape)
```

```python
import functools
import math

import jax
import jax.numpy as jnp
from jax import lax
from jax.experimental import pallas as pl
from jax.experimental.pallas import tpu as pltpu

NORM_EPS = 1e-6
A_HEAD_DIM = 128
A_HEADS = 16
A_KV_HEADS = 4
A_GROUP = A_HEADS // A_KV_HEADS
A_ROT_HALF = 16
IDX_HEADS = 32
IDX_DIM = 64
IDX_ROT_HALF = 8
TOPK_MAX = 256
ROPE_THETA = 500000.0
RWKV_HEAD = 64
RWKV_DIM = 2048
DECAY_RANK = 96
AAA_RANK = 96
GATE_RANK = 256
RWKV_GN_EPS = 1e-5 * RWKV_HEAD
S5_DIM = 2048
S5_GROUP = 16
S5_GROUPS = 128
S5_STATE = 64
RET_HEAD_DIM = 256
RET_HEADS = 8
RET_ROPE_BASE = 10000.0
RET_GN_EPS = 1e-5

LANES = 128
SUBLANES = 8
VMEM_LIMIT_BYTES = 56 * 2**20

MXU_DTYPE = jnp.bfloat16
NEG_BIG = -1e30
INT_MIN = -2**31

ZE_Q, ZE_K, ZE_V, ZE_QI = 0, 2048, 2560, 3072
ZE_KI = 5120
ZE_WI_LANE = IDX_DIM
ZE_R, ZE_RK, ZE_RV = 5216, 7264, 9312
ZE_LORA = 11360
ZE_SKEW = ZE_R % LANES
ZE_MAIN_COLS = 11776
ZS_WD, ZS_AD, ZS_GD, ZS_COLS = 0, 128, 256, 512


def _cparams(sem):
    return pltpu.CompilerParams(dimension_semantics=sem, vmem_limit_bytes=VMEM_LIMIT_BYTES)


def _bf(x):
    return x.astype(MXU_DTYPE)


def _dot(a, b):
    return jnp.dot(a, b, preferred_element_type=jnp.float32)


def _dot_nt(a, b):
    return lax.dot_general(a, b, (((1,), (1,)), ((), ())), preferred_element_type=jnp.float32)


def _split3(x):
    h1 = _bf(x)
    r1 = x - h1.astype(jnp.float32)
    h2 = _bf(r1)
    h3 = _bf(r1 - h2.astype(jnp.float32))
    return h1, h2, h3


def _dot_x3(a, b):
    ah = _bf(a)
    al = _bf(a - ah.astype(jnp.float32))
    bh = _bf(b)
    bl = _bf(b - bh.astype(jnp.float32))
    return _dot(ah, bh) + _dot(ah, bl) + _dot(al, bh)


def _sigmoid(x):
    return 1.0 / (1.0 + jnp.exp(-x))


def _rmsnorm_kernel(x_ref, g_ref, o_ref):
    x = x_ref[...]
    ms = jnp.mean(x * x, axis=-1, keepdims=True)
    o_ref[...] = (x * lax.rsqrt(ms + NORM_EPS) * g_ref[...]).astype(o_ref.dtype)


def _rmsnorm(x, g, out_dtype, tm=256):
    m, d = x.shape
    return pl.pallas_call(
        _rmsnorm_kernel,
        out_shape=jax.ShapeDtypeStruct((m, d), out_dtype),
        grid=(m // tm,),
        in_specs=[pl.BlockSpec((tm, d), lambda i: (i, 0)), pl.BlockSpec((1, d), lambda i: (0, 0))],
        out_specs=pl.BlockSpec((tm, d), lambda i: (i, 0)),
        compiler_params=_cparams(("parallel",)),
        name="rmsnorm",
    )(x, g.reshape(1, d))


def _mm_kernel(*refs, n_a, n_w, n_e, pairs, epilogue, nk):
    a_refs = refs[:n_a]
    w_refs = refs[n_a:n_a + n_w]
    e_refs = refs[n_a + n_w:n_a + n_w + n_e]
    o_ref = refs[n_a + n_w + n_e]
    acc_refs = refs[n_a + n_w + n_e + 1:]
    if nk == 1:
        accs = [_dot(a_refs[i][...], _bf(w_refs[j][...])) for i, j in pairs]
        o_ref[...] = epilogue(accs, [e[...] for e in e_refs]).astype(o_ref.dtype)
        return
    k = pl.program_id(2)

    @pl.when(k == 0)
    def _():
        for acc in acc_refs:
            acc[...] = jnp.zeros_like(acc)

    for acc, (i, j) in zip(acc_refs, pairs):
        acc[...] += _dot(a_refs[i][...], _bf(w_refs[j][...]))

    @pl.when(k == nk - 1)
    def _():
        o_ref[...] = epilogue([acc[...] for acc in acc_refs], [e[...] for e in e_refs]).astype(o_ref.dtype)


def _weight_spec(w, layer, k_off, tk, tn):
    if w.ndim == 3:
        return pl.BlockSpec((None, tk, tn), lambda i, j, k: (layer, k + k_off, j))
    return pl.BlockSpec((tk, tn), lambda i, j, k: (k + k_off, j))


MM_TILES = {
    "even_in_proj": (1024, 512), "even_lora_proj": (1024, 512), "odd_in_proj": (1024, 512), "s5_glu": (1024, 512),
    "out_proj": (1024, 512), "ffn_gate_up": (2048, 256), "ffn_down": (1024, 512),
}
FFN_DOWN_K_SPLITS = 2


def _matmul(a_list, w_list, pairs, extras, epilogue, out_dtype, name, tk=None, n_out=None):
    m, kdim = a_list[0].shape
    n = w_list[0][0].shape[-1] if n_out is None else n_out
    tk = kdim if tk is None else tk
    tm, tn = MM_TILES[name]
    tm = min(tm, m)
    nk = kdim // tk
    assert m % tm == 0 and n % tn == 0 and kdim % tk == 0
    in_specs = [pl.BlockSpec((tm, tk), lambda i, j, k: (i, k)) for _ in a_list]
    in_specs += [_weight_spec(w, layer, k_off, tk, tn) for w, layer, k_off in w_list]
    w_list = [w for w, _, _ in w_list]
    e_arrays = []
    for kind, arr in extras:
        e_arrays.append(arr)
        if kind == "tile":
            in_specs.append(pl.BlockSpec((tm, tn), lambda i, j, k: (i, j)))
        else:
            in_specs.append(pl.BlockSpec((1, tn), lambda i, j, k: (0, j)))
    scratch = [] if nk == 1 else [pltpu.VMEM((tm, tn), jnp.float32) for _ in pairs]
    kern = functools.partial(_mm_kernel, n_a=len(a_list), n_w=len(w_list), n_e=len(extras),
                             pairs=tuple(pairs), epilogue=epilogue, nk=nk)
    return pl.pallas_call(
        kern,
        out_shape=jax.ShapeDtypeStruct((m, n), out_dtype),
        grid=(m // tm, n // tn, nk),
        in_specs=in_specs,
        out_specs=pl.BlockSpec((tm, tn), lambda i, j, k: (i, j)),
        scratch_shapes=scratch,
        compiler_params=_cparams(("parallel", "parallel", "arbitrary")),
        name=name,
    )(*a_list, *w_list, *e_arrays)


def _ep_plain(accs, ex):
    return accs[0]


def _ep_swiglu(accs, ex):
    g = accs[0]
    return g * _sigmoid(g) * accs[1]


def _ep_resid(accs, ex):
    return ex[0] + accs[0]


def _ep_resid2(accs, ex):
    return ex[0] + (accs[0] + accs[1])


def _ep_glu(accs, ex):
    zg, b = ex
    return zg * _sigmoid(accs[0] + b)


def _rope_tile(x, tc, s1, s2, n_half):
    return x * tc + pltpu.roll(x, n_half, 1) * s2 + pltpu.roll(x, LANES - n_half, 1) * s1


DSA_VPAD = 16
DSA_VROWS = A_HEAD_DIM + DSA_VPAD


def _dsa_prep_kernel(zq_ref, zs_ref, ta_ref, ti_ref, qt_ref, k_ref, vt_ref, qit_ref, ki_ref, wt_ref):
    ta_c, ta_1, ta_2 = ta_ref[0], ta_ref[1], ta_ref[2]
    ti_c, ti_1, ti_2 = ti_ref[0], ti_ref[1], ti_ref[2]
    q_scale = A_HEAD_DIM ** -0.5 * math.log2(math.e)
    for h in range(A_HEADS):
        x = zq_ref[:, ZE_Q + h * LANES:ZE_Q + (h + 1) * LANES]
        y = _rope_tile(x, ta_c, ta_1, ta_2, A_ROT_HALF) * q_scale
        qt_ref[h * LANES:(h + 1) * LANES, :] = _bf(y.T)
    for g in range(A_KV_HEADS):
        x = zq_ref[:, ZE_K + g * LANES:ZE_K + (g + 1) * LANES]
        k_ref[:, g * LANES:(g + 1) * LANES] = _bf(_rope_tile(x, ta_c, ta_1, ta_2, A_ROT_HALF))
        v = zq_ref[:, ZE_V + g * LANES:ZE_V + (g + 1) * LANES]
        vt_ref[g * DSA_VROWS:g * DSA_VROWS + A_HEAD_DIM, :] = _bf(v.T)
        vt_ref[g * DSA_VROWS + A_HEAD_DIM:(g + 1) * DSA_VROWS, :] = jnp.ones((DSA_VPAD, v.shape[0]), MXU_DTYPE)
    for c in range(IDX_HEADS * IDX_DIM // LANES):
        x = zq_ref[:, ZE_QI + c * LANES:ZE_QI + (c + 1) * LANES]
        y = _rope_tile(x, ti_c, ti_1, ti_2, IDX_ROT_HALF)
        qit_ref[c * LANES:(c + 1) * LANES, :] = _bf(y.T)
    ki_ref[...] = _bf(_rope_tile(zs_ref[...], ti_c, ti_1, ti_2, IDX_ROT_HALF))
    w_scale = (IDX_DIM ** -0.5) * (IDX_HEADS ** -0.5)
    wt_ref[...] = (zs_ref[...] * w_scale).T


def _dsa_prep(z, tab_a, tab_i, tm=256):
    L = z.shape[0]
    n_q = A_HEADS * A_HEAD_DIM
    n_kv = A_KV_HEADS * A_HEAD_DIM
    n_qi = IDX_HEADS * IDX_DIM
    out_shape = (
        jax.ShapeDtypeStruct((n_q, L), MXU_DTYPE),
        jax.ShapeDtypeStruct((L, n_kv), MXU_DTYPE),
        jax.ShapeDtypeStruct((A_KV_HEADS * DSA_VROWS, L), MXU_DTYPE),
        jax.ShapeDtypeStruct((n_qi, L), MXU_DTYPE),
        jax.ShapeDtypeStruct((L, LANES), MXU_DTYPE),
        jax.ShapeDtypeStruct((LANES, L), jnp.float32),
    )
    return pl.pallas_call(
        _dsa_prep_kernel,
        out_shape=out_shape,
        grid=(L // tm,),
        in_specs=[
            pl.BlockSpec((tm, ZE_KI), lambda i: (i, 0)),
            pl.BlockSpec((tm, LANES), lambda i: (i, ZE_KI // LANES)),
            pl.BlockSpec((3, tm, LANES), lambda i: (0, i, 0)),
            pl.BlockSpec((3, tm, LANES), lambda i: (0, i, 0)),
        ],
        out_specs=(
            pl.BlockSpec((n_q, tm), lambda i: (0, i)),
            pl.BlockSpec((tm, n_kv), lambda i: (i, 0)),
            pl.BlockSpec((A_KV_HEADS * DSA_VROWS, tm), lambda i: (0, i)),
            pl.BlockSpec((n_qi, tm), lambda i: (0, i)),
            pl.BlockSpec((tm, LANES), lambda i: (i, 0)),
            pl.BlockSpec((LANES, tm), lambda i: (0, i)),
        ),
        compiler_params=_cparams(("parallel",)),
        name="dsa_prep",
    )(z, z, tab_a, tab_i)


DSA_TQ = 128
DSA_KC = 512
DSA_KA = 256
DSA_GROUP_SETS = ((0, 1, 2, 3),)


def _dsa_kernel(qt_ref, qit_ref, wt_ref, ki_ref, k_ref, vt_ref, o_ref,
                qi_s, qg_s, sc_s, key_s, m_s, acc_s, *, n_top):
    qb = pl.program_id(0)
    q0 = qb * DSA_TQ
    n_chunks = (q0 + DSA_TQ + DSA_KC - 1) // DSA_KC
    q_pos = q0 + lax.broadcasted_iota(jnp.int32, (1, DSA_TQ), 1)

    for h in range(IDX_HEADS):
        qi_s[:, h * DSA_TQ:(h + 1) * DSA_TQ] = qit_ref[h * IDX_DIM:(h + 1) * IDX_DIM, :]
    for g in range(A_KV_HEADS):
        for j in range(A_GROUP):
            h = g * A_GROUP + j
            qg_s[g, :, j * DSA_TQ:(j + 1) * DSA_TQ] = qt_ref[h * A_HEAD_DIM:(h + 1) * A_HEAD_DIM, :]

    def score_chunk(c, carry):
        for sub in range(DSA_KC // DSA_KA):
            r0 = pl.multiple_of(c * DSA_KC + sub * DSA_KA, DSA_KA)
            kic = ki_ref[pl.ds(r0, DSA_KA), 0:IDX_DIM]
            logits = _dot(kic, qi_s[...])
            acc = jnp.zeros((DSA_KA, DSA_TQ), jnp.float32)
            for h in range(IDX_HEADS):
                w_h = wt_ref[ZE_WI_LANE + h:ZE_WI_LANE + h + 1, :]
                acc = acc + jnp.maximum(logits[:, h * DSA_TQ:(h + 1) * DSA_TQ], 0.0) * w_h
            bits = lax.bitcast_convert_type(acc, jnp.int32)
            okey = bits ^ ((bits >> 31) & jnp.int32(0x7FFFFFFF))
            k_pos = r0 + lax.broadcasted_iota(jnp.int32, (DSA_KA, 1), 0)
            key_s[pl.ds(r0, DSA_KA), :] = jnp.where(k_pos <= q_pos, okey, jnp.int32(INT_MIN))
        return carry

    lax.fori_loop(0, n_chunks, score_chunk, 0)

    def count_keys(cand, strict):
        def body(c, cnt):
            r0 = pl.multiple_of(c * DSA_KC, DSA_KC)
            blk = key_s[pl.ds(r0, DSA_KC), :]
            hit = jnp.where((blk > cand) if strict else (blk >= cand), jnp.int32(1), jnp.int32(0))
            return cnt + jnp.sum(hit.reshape(DSA_KC // SUBLANES, SUBLANES, DSA_TQ), axis=0)
        cnt8 = lax.fori_loop(0, n_chunks, body, jnp.zeros((SUBLANES, DSA_TQ), jnp.int32))
        return jnp.sum(cnt8, axis=0, keepdims=True)

    def bisect(it, u):
        cand_u = u | lax.shift_left(jnp.int32(1), 31 - it)
        cnt = count_keys(cand_u ^ jnp.int32(INT_MIN), False)
        return jnp.where(cnt >= n_top, cand_u, u)

    thr = lax.fori_loop(0, 32, bisect, jnp.zeros((1, DSA_TQ), jnp.int32)) ^ jnp.int32(INT_MIN)
    thr = jnp.maximum(thr, jnp.int32(INT_MIN + 1))
    n_sel = count_keys(thr, False)

    def bias_plain():
        def body(c, carry):
            r0 = pl.multiple_of(c * DSA_KC, DSA_KC)
            sc_s[pl.ds(r0, DSA_KC), :] = jnp.where(key_s[pl.ds(r0, DSA_KC), :] >= thr, 0.0, NEG_BIG)
            return carry
        lax.fori_loop(0, n_chunks, body, 0)

    def bias_ties():
        need = (n_top - count_keys(thr, True)).astype(jnp.float32)
        rr = lax.broadcasted_iota(jnp.int32, (DSA_KC, DSA_KC), 0)
        cc = lax.broadcasted_iota(jnp.int32, (DSA_KC, DSA_KC), 1)
        before = jnp.where(cc < rr, 1.0, 0.0).astype(MXU_DTYPE)

        def body(c, seen):
            r0 = pl.multiple_of(c * DSA_KC, DSA_KC)
            blk = key_s[pl.ds(r0, DSA_KC), :]
            tie = jnp.where(blk == thr, 1.0, 0.0)
            rank = _dot(before, _bf(tie)) + seen
            sel = jnp.logical_or(blk > thr, jnp.logical_and(blk == thr, rank < need))
            sc_s[pl.ds(r0, DSA_KC), :] = jnp.where(sel, 0.0, NEG_BIG)
            return seen + jnp.sum(tie, axis=0, keepdims=True)
        lax.fori_loop(0, n_chunks, body, jnp.zeros((1, DSA_TQ), jnp.float32))

    lax.cond(jnp.max(n_sel) > n_top, bias_ties, bias_plain)

    m_s[...] = jnp.full(m_s.shape, NEG_BIG, jnp.float32)
    acc_s[...] = jnp.zeros(acc_s.shape, jnp.float32)

    def attn_chunk(c, carry, groups):
        r0 = pl.multiple_of(c * DSA_KC, DSA_KC)
        bias = sc_s[pl.ds(r0, DSA_KC), :]
        bias_g = jnp.concatenate([bias] * A_GROUP, axis=1)
        s = {g: _dot(k_ref[pl.ds(r0, DSA_KC), g * A_HEAD_DIM:(g + 1) * A_HEAD_DIM], qg_s[g]) + bias_g for g in groups}
        m_old = {g: m_s[g] for g in groups}
        m_new = {g: jnp.maximum(m_old[g], jnp.max(s[g], axis=0, keepdims=True)) for g in groups}
        alpha = {g: jnp.exp2(m_old[g] - m_new[g]) for g in groups}
        p = {g: _bf(jnp.exp2(s[g] - m_new[g])) for g in groups}
        pv = {g: _dot(vt_ref[g * DSA_VROWS:(g + 1) * DSA_VROWS, pl.ds(r0, DSA_KC)], p[g]) for g in groups}
        acc = {g: alpha[g] * acc_s[g] + pv[g] for g in groups}
        for g in groups:
            acc_s[g] = acc[g]
            m_s[g] = m_new[g]
        return carry

    for groups in DSA_GROUP_SETS:
        lax.fori_loop(0, n_chunks, functools.partial(attn_chunk, groups=groups), 0)
    for g in range(A_KV_HEADS):
        acc = acc_s[g]
        o_t = acc[0:A_HEAD_DIM, :] / acc[A_HEAD_DIM:A_HEAD_DIM + 1, :]
        for j in range(A_GROUP):
            h = g * A_GROUP + j
            o_ref[:, h * A_HEAD_DIM:(h + 1) * A_HEAD_DIM] = _bf(o_t[:, j * DSA_TQ:(j + 1) * DSA_TQ].T)


def _dsa_attention(q_t, k_r, v_t, qi_t, ki_r, w_t, n_top):
    L = k_r.shape[0]
    n_q = A_HEADS * A_HEAD_DIM
    n_kv = A_KV_HEADS * A_HEAD_DIM
    n_qi = IDX_HEADS * IDX_DIM
    assert L % DSA_KC == 0
    resident = dict(pipeline_mode=pl.Buffered(1))
    return pl.pallas_call(
        functools.partial(_dsa_kernel, n_top=n_top),
        out_shape=jax.ShapeDtypeStruct((L, n_q), MXU_DTYPE),
        grid=(L // DSA_TQ,),
        in_specs=[
            pl.BlockSpec((n_q, DSA_TQ), lambda i: (0, i)),
            pl.BlockSpec((n_qi, DSA_TQ), lambda i: (0, i)),
            pl.BlockSpec((LANES, DSA_TQ), lambda i: (0, i)),
            pl.BlockSpec((L, LANES), lambda i: (0, 0), **resident),
            pl.BlockSpec((L, n_kv), lambda i: (0, 0), **resident),
            pl.BlockSpec((A_KV_HEADS * DSA_VROWS, L), lambda i: (0, 0), **resident),
        ],
        out_specs=pl.BlockSpec((DSA_TQ, n_q), lambda i: (i, 0)),
        scratch_shapes=[
            pltpu.VMEM((IDX_DIM, IDX_HEADS * DSA_TQ), MXU_DTYPE),
            pltpu.VMEM((A_KV_HEADS, A_HEAD_DIM, A_GROUP * DSA_TQ), MXU_DTYPE),
            pltpu.VMEM((L, DSA_TQ), jnp.float32),
            pltpu.VMEM((L, DSA_TQ), jnp.int32),
            pltpu.VMEM((A_KV_HEADS, 1, A_GROUP * DSA_TQ), jnp.float32),
            pltpu.VMEM((A_KV_HEADS, DSA_VROWS, A_GROUP * DSA_TQ), jnp.float32),
        ],
        compiler_params=_cparams(("arbitrary",)),
        name="dsa_attention",
    )(q_t, qi_t, w_t, ki_r, k_r, v_t)


def _rope_tables(L, n_half, period, theta_pow_dim):
    inv = ROPE_THETA ** (-jnp.arange(0, theta_pow_dim, 2, dtype=jnp.float32) / theta_pow_dim)
    ang = jnp.arange(L, dtype=jnp.float32)[:, None] * inv[None, :]
    cos, sin = jnp.cos(ang), jnp.sin(ang)
    pad = period - 2 * n_half
    tc = jnp.concatenate([cos, cos, jnp.ones((L, pad), jnp.float32)], axis=1)
    s1 = jnp.concatenate([-sin, jnp.zeros((L, period - n_half), jnp.float32)], axis=1)
    s2 = jnp.concatenate([jnp.zeros((L, n_half), jnp.float32), sin, jnp.zeros((L, pad), jnp.float32)], axis=1)
    reps = LANES // period
    return jnp.stack([jnp.tile(t, (1, reps)) for t in (tc, s1, s2)])


RWKV_CHUNK = 64
RWKV_SLAB = 1024
RWKV_TB = 256
P_MU_R, P_MU_K, P_MU_V, P_W0, P_A0, P_KK, P_KA, P_RK, P_LNW, P_LNB = range(10)


def _head_ones():
    r = lax.broadcasted_iota(jnp.int32, (LANES, LANES), 0) // RWKV_HEAD
    c = lax.broadcasted_iota(jnp.int32, (LANES, LANES), 1) // RWKV_HEAD
    return jnp.where(r == c, 1.0, 0.0).astype(MXU_DTYPE)


def _head_sum(x, ones_bd):
    hi = _bf(x)
    lo = _bf(x - hi.astype(jnp.float32))
    tiles = [_dot(hi[:, t * LANES:(t + 1) * LANES], ones_bd) + _dot(lo[:, t * LANES:(t + 1) * LANES], ones_bd)
             for t in range(x.shape[1] // LANES)]
    return jnp.concatenate(tiles, axis=1)


def _pair_stack(x):
    lane = lax.broadcasted_iota(jnp.int32, x.shape, 1)
    return jnp.concatenate([jnp.where(lane < RWKV_HEAD, x, 0.0), jnp.where(lane >= RWKV_HEAD, x, 0.0)], axis=0)


def _rwkv_kernel(zr_ref, zrx_ref, zk_ref, zkx_ref, zv_ref, zvx_ref, zs_ref,
                 hr_ref, hrx_ref, hk_ref, hkx_ref, hv_ref, hvx_ref, hs_ref, p_ref, mus_ref, lora_ref,
                 o_ref, h_s, r_s, lw_s, cum_s, k2_s, v_s, kk_s, a_s, g_s, bon_s, oo_s):
    C = RWKV_CHUNK
    tb = zr_ref.shape[0]
    i = pl.program_id(1)

    @pl.when(i == 0)
    def _():
        h_s[...] = jnp.zeros_like(h_s)

    row = lax.broadcasted_iota(jnp.int32, (tb, 1), 0)
    has_prev = jnp.where(i > 0, 1.0, 0.0)

    def unskew(main_ref, next_ref):
        x = jnp.concatenate([main_ref[...], next_ref[...]], axis=1)
        return pltpu.roll(x, x.shape[1] - ZE_SKEW, 1)[:, 0:main_ref.shape[1]]

    def mix(z, halo, mu):
        prev = halo[SUBLANES - 1:SUBLANES, :] * has_prev
        shifted = jnp.where(row == 0, prev, pltpu.roll(z, 1, 0))
        return z + (shifted - z) * mu

    r = mix(unskew(zr_ref, zrx_ref), unskew(hr_ref, hrx_ref), p_ref[P_MU_R:P_MU_R + 1, :])
    k = mix(unskew(zk_ref, zkx_ref), unskew(hk_ref, hkx_ref), p_ref[P_MU_K:P_MU_K + 1, :])
    v = mix(unskew(zv_ref, zvx_ref), unskew(hv_ref, hvx_ref), p_ref[P_MU_V:P_MU_V + 1, :])
    sm = mix(zs_ref[...], hs_ref[...], mus_ref[...])
    wd, ad, gd = sm[:, 0:LANES], sm[:, LANES:2 * LANES], sm[:, 2 * LANES:4 * LANES]
    x = -(p_ref[P_W0:P_W0 + 1, :] + _dot_x3(jnp.tanh(wd), lora_ref[0:LANES, :]))
    softplus = jnp.maximum(x, 0.0) + jnp.log(1.0 + jnp.exp(-jnp.abs(x)))
    lw = -jnp.exp(-softplus - 0.5)
    a = _sigmoid(p_ref[P_A0:P_A0 + 1, :] + _dot(_bf(ad), _bf(lora_ref[LANES:2 * LANES, :])))
    g_s[...] = _dot(_bf(_sigmoid(gd)), _bf(lora_ref[2 * LANES:4 * LANES, :]))
    ones_bd = _head_ones()
    kk = k * p_ref[P_KK:P_KK + 1, :]
    kk = kk * lax.rsqrt(jnp.maximum(_head_sum(kk * kk, ones_bd), 1e-24))
    k2 = k * (1.0 + (a - 1.0) * p_ref[P_KA:P_KA + 1, :])
    bon_s[...] = _head_sum(r * k2 * p_ref[P_RK:P_RK + 1, :], ones_bd)
    tr = lax.broadcasted_iota(jnp.int32, (tb, tb), 0)
    tc = lax.broadcasted_iota(jnp.int32, (tb, tb), 1)
    tril_bd = jnp.where(jnp.logical_and(tc <= tr, tc // C == tr // C), 1.0, 0.0).astype(MXU_DTYPE)
    l1, l2, l3 = _split3(lw)
    cum_s[...] = _dot(tril_bd, l1) + _dot(tril_bd, l2) + _dot(tril_bd, l3)
    r_s[...] = r
    lw_s[...] = lw
    k2_s[...] = k2
    v_s[...] = v
    kk_s[...] = kk
    a_s[...] = a

    n2 = 2 * C
    rr = lax.broadcasted_iota(jnp.int32, (n2, n2), 0)
    cc = lax.broadcasted_iota(jnp.int32, (n2, n2), 1)
    strict = (cc % C) < (rr % C)
    incl = (cc % C) <= (rr % C)
    eye = jnp.where(rr == cc, 1.0, 0.0)

    n_pairs = RWKV_SLAB // LANES
    pairs = range(n_pairs)

    def chunk(c, carry):
        r0 = pl.multiple_of(c * C, C)

        def ld(ref):
            return [ref[pl.ds(r0, C), p * LANES:(p + 1) * LANES] for p in pairs]

        rc, lwc, kc, vc, kkc, ac, cum = ld(r_s), ld(lw_s), ld(k2_s), ld(v_s), ld(kk_s), ld(a_s), ld(cum_s)
        h_old = [h_s[p] for p in pairs]
        cl = [cum[p][C - 1:C, :] for p in pairs]
        e_neg = [jnp.exp(-cum[p]) for p in pairs]
        e_end = [jnp.exp(cl[p] - cum[p]) for p in pairs]
        beta = [kkc[p] * ac[p] for p in pairs]
        ae_s = [_pair_stack(-kkc[p] * jnp.exp(cum[p] - lwc[p])) for p in pairs]
        rp_s = [_pair_stack(rc[p] * jnp.exp(cum[p])) for p in pairs]
        bm_s = [_pair_stack(beta[p] * e_neg[p]) for p in pairs]
        km_s = [_pair_stack(kc[p] * e_neg[p]) for p in pairs]
        bt_t = [_bf(_pair_stack(beta[p] * e_end[p]).T) for p in pairs]
        kt_t = [_bf(_pair_stack(kc[p] * e_end[p]).T) for p in pairs]
        v_st = [_bf(_pair_stack(vc[p])) for p in pairs]
        aa = [_dot_nt(_bf(jnp.concatenate([ae_s[p], rp_s[p]], axis=0)),
                      _bf(jnp.concatenate([bm_s[p], km_s[p]], axis=0))) for p in pairs]
        a_ab = [jnp.where(strict, aa[p][0:n2, 0:n2], 0.0) for p in pairs]
        a_ak = [_bf(jnp.where(strict, aa[p][0:n2, n2:2 * n2], 0.0)) for p in pairs]
        a_rb = [_bf(jnp.where(incl, aa[p][n2:2 * n2, 0:n2], 0.0)) for p in pairs]
        a_rk = [_bf(jnp.where(incl, aa[p][n2:2 * n2, n2:2 * n2], 0.0)) for p in pairs]
        t_inv = [eye + a_ab[p] for p in pairs]
        n_b = [_bf(a_ab[p]) for p in pairs]
        n_pow = [_dot(n_b[p], n_b[p]) for p in pairs]
        for _ in range(int(math.log2(C)) - 2):
            n_b = [_bf(n_pow[p]) for p in pairs]
            both = [_dot(jnp.concatenate([_bf(t_inv[p]), n_b[p]], axis=0), n_b[p]) for p in pairs]
            t_inv = [t_inv[p] + both[p][0:n2, :] for p in pairs]
            n_pow = [both[p][n2:2 * n2, :] for p in pairs]
        t_inv = [t_inv[p] + _dot(_bf(t_inv[p]), _bf(n_pow[p])) for p in pairs]
        t_b = [_bf(t_inv[p]) for p in pairs]
        w_k = [_bf(_dot(a_ak[p], v_st[p])) for p in pairs]
        t_rhs = [_dot(t_b[p], jnp.concatenate([_bf(ae_s[p]), w_k[p]], axis=1)) for p in pairs]
        a_til = [_bf(t_rhs[p][:, 0:LANES]) for p in pairs]
        v_til = [t_rhs[p][:, LANES:2 * LANES] for p in pairs]
        o_intra = [_dot(a_rk[p], v_st[p]) for p in pairs]
        h_kv = [_dot(kt_t[p], v_st[p]) for p in pairs]
        decay_col = [jnp.exp(jnp.broadcast_to(cl[p], (LANES, LANES)).T) for p in pairs]
        h_b = [_bf(h_old[p]) for p in pairs]
        u_b = [_bf(_dot(a_til[p], h_b[p]) + v_til[p]) for p in pairs]
        o_st = [_dot(_bf(rp_s[p]), h_b[p]) + _dot(a_rb[p], u_b[p]) + o_intra[p] for p in pairs]
        h_new = [decay_col[p] * h_old[p] + _dot(bt_t[p], u_b[p]) + h_kv[p] for p in pairs]
        for p in pairs:
            h_s[p] = h_new[p]
            oo_s[pl.ds(r0, C), p * LANES:(p + 1) * LANES] = o_st[p][0:C, :] + o_st[p][C:n2, :]
        return carry

    lax.fori_loop(0, tb // C, chunk, 0)

    o = oo_s[...]
    mean = _head_sum(o, ones_bd) * (1.0 / RWKV_HEAD)
    d = o - mean
    var = _head_sum(d * d, ones_bd) * (1.0 / RWKV_HEAD)
    y = d * lax.rsqrt(var + RWKV_GN_EPS) * p_ref[P_LNW:P_LNW + 1, :] + p_ref[P_LNB:P_LNB + 1, :]
    y = y + bon_s[...] * v_s[...]
    o_ref[...] = (y * g_s[...]).astype(o_ref.dtype)


def _rwkv_params(mu, w0, w_up, a0, a_up, g_up, k_k, k_a, r_k, ln_w, ln_b):
    d = RWKV_DIM
    mu_r, mu_k, mu_v = mu[0:d], mu[d:2 * d], mu[2 * d:3 * d]
    o = 3 * d
    mu_wd, mu_ad, mu_gd = mu[o:o + DECAY_RANK], mu[o + DECAY_RANK:o + DECAY_RANK + AAA_RANK], mu[o + DECAY_RANK + AAA_RANK:]
    rows = [mu_r, mu_k, mu_v, w0, a0, k_k, k_a, r_k.reshape(d), ln_w, ln_b]
    p_rows = jnp.concatenate([jnp.stack(rows), jnp.zeros((16 - len(rows), d), jnp.float32)], axis=0)

    def pad_to(x, n, axis):
        widths = [(0, 0)] * x.ndim
        widths[axis] = (0, n - x.shape[axis])
        return jnp.pad(x, widths)

    mu_small = jnp.concatenate([pad_to(mu_wd, LANES, 0), pad_to(mu_ad, LANES, 0), mu_gd]).reshape(1, 4 * LANES)
    lora = jnp.concatenate([pad_to(w_up, LANES, 0), pad_to(a_up, LANES, 0), g_up], axis=0)
    return p_rows, mu_small, lora


def _rwkv_mix(z, z_lora, p_rows, mu_small, lora):
    L = z.shape[0]
    tb, w = RWKV_TB, RWKV_SLAB
    assert L % tb == 0
    hb = tb // SUBLANES
    tiles = w // LANES

    def prev_rows(i):
        return jnp.maximum(i * hb - 1, 0)

    def windows(col0, rows, row_map):
        base = col0 - ZE_SKEW
        assert base % w == 0
        return [pl.BlockSpec((rows, w), lambda s, i: (row_map(i), base // w + s)),
                pl.BlockSpec((rows, LANES), lambda s, i: (row_map(i), base // LANES + (s + 1) * tiles))]

    body = [spec for c in (ZE_R, ZE_RK, ZE_RV) for spec in windows(c, tb, lambda i: i)]
    halo = [spec for c in (ZE_R, ZE_RK, ZE_RV) for spec in windows(c, SUBLANES, prev_rows)]
    f32 = jnp.float32
    return pl.pallas_call(
        _rwkv_kernel,
        out_shape=jax.ShapeDtypeStruct((L, RWKV_DIM), MXU_DTYPE),
        grid=(RWKV_DIM // w, L // tb),
        in_specs=body + [pl.BlockSpec((tb, ZS_COLS), lambda s, i: (i, 0))]
        + halo + [pl.BlockSpec((SUBLANES, ZS_COLS), lambda s, i: (prev_rows(i), 0))]
        + [
            pl.BlockSpec((16, w), lambda s, i: (0, s)),
            pl.BlockSpec((1, ZS_COLS), lambda s, i: (0, 0)),
            pl.BlockSpec((ZS_COLS, w), lambda s, i: (0, s)),
        ],
        out_specs=pl.BlockSpec((tb, w), lambda s, i: (i, s)),
        scratch_shapes=[pltpu.VMEM((w // LANES, LANES, LANES), f32)] + [pltpu.VMEM((tb, w), f32) for _ in range(10)],
        compiler_params=_cparams(("parallel", "arbitrary")),
        name="rwkv7_mix",
    )(*([z] * 6), z_lora, *([z] * 6), z_lora, p_rows, mu_small, lora)


S5_TB = 1024
S5_SEG = SUBLANES
S5_SLAB_GROUPS = LANES // S5_GROUP
S5_SW = S5_SLAB_GROUPS * S5_STATE


def _cmul_add(ar, ai, xr, xi, br, bi):
    return ar * xr - ai * xi + br, ar * xi + ai * xr + bi


def _s5_kernel(u_ref, w_ref, c_ref, apow_ref, d_ref, zg_ref, zgb_ref, st_s, up_s, bu_s, x_s, y_s):
    tb = u_ref.shape[0]
    ts = tb // S5_SEG
    sw = S5_SW
    i = pl.program_id(1)

    @pl.when(i == 0)
    def _():
        st_s[...] = jnp.zeros_like(st_s)

    for tau in range(ts):
        up_s[tau * S5_SEG:(tau + 1) * S5_SEG, :] = u_ref[pl.ds(tau, S5_SEG, stride=ts), :]
    bu_s[...] = _dot(_bf(up_s[...]), _bf(w_ref[0]))
    a1 = apow_ref[0, 0:1, :]
    ar = jnp.broadcast_to(a1[:, 0:sw], (S5_SEG, sw))
    ai = jnp.broadcast_to(a1[:, sw:2 * sw], (S5_SEG, sw))

    def scan(tau, x):
        r0 = pl.multiple_of(tau * S5_SEG, S5_SEG)
        b = bu_s[pl.ds(r0, S5_SEG), :]
        nr, ni = _cmul_add(ar, ai, x[0], x[1], b[:, 0:sw], b[:, sw:2 * sw])
        x_s[pl.ds(r0, S5_SEG), :] = jnp.concatenate([nr, ni], axis=1)
        return nr, ni

    zero = jnp.zeros((S5_SEG, sw), jnp.float32)
    xr, xi = lax.fori_loop(0, ts, scan, (zero, zero), unroll=4)

    a_ts = apow_ref[0, ts - 1:ts, :]
    tr, ti = a_ts[:, 0:sw], a_ts[:, sw:2 * sw]
    cr, ci = st_s[:, 0:sw], st_s[:, sw:2 * sw]
    ent_r, ent_i = [], []
    for s in range(S5_SEG):
        ent_r.append(cr)
        ent_i.append(ci)
        cr, ci = _cmul_add(tr, ti, cr, ci, xr[s:s + 1, :], xi[s:s + 1, :])
    st_s[...] = jnp.concatenate([cr, ci], axis=1)
    er = jnp.concatenate(ent_r, axis=0)
    ei = jnp.concatenate(ent_i, axis=0)

    def fix(tau, carry):
        r0 = pl.multiple_of(tau * S5_SEG, S5_SEG)
        ap = apow_ref[0, pl.ds(tau, 1), :]
        x = x_s[pl.ds(r0, S5_SEG), :]
        nr, ni = _cmul_add(ap[:, 0:sw], ap[:, sw:2 * sw], er, ei, x[:, 0:sw], x[:, sw:2 * sw])
        x_s[pl.ds(r0, S5_SEG), :] = jnp.concatenate([nr, ni], axis=1)
        return carry

    lax.fori_loop(0, ts, fix, 0, unroll=4)
    c_out = _bf(c_ref[0])
    half = tb // 2
    yp = jnp.concatenate([_dot(_bf(x_s[0:half, :]), c_out), _dot(_bf(x_s[half:tb, :]), c_out)], axis=0)
    for tau in range(ts):
        y_s[pl.ds(tau, S5_SEG, stride=ts), :] = yp[tau * S5_SEG:(tau + 1) * S5_SEG, :]
    y = y_s[...] + d_ref[...] * u_ref[...]
    zg = 0.5 * y * (1.0 + jnp.tanh(math.sqrt(2.0 / math.pi) * (y + 0.044715 * (y * y * y))))
    zg_ref[...] = zg
    zgb_ref[...] = zg.astype(zgb_ref.dtype)


def _s5_params(lam_re, lam_im, log_step, b_re, b_im, c_re, c_im, n_pow):
    lr = jnp.minimum(lam_re, -1e-4)
    li = lam_im
    step = jnp.exp(log_step)[:, None]
    mag = jnp.exp(lr * step)
    abar_r = mag * jnp.cos(li * step)
    abar_i = mag * jnp.sin(li * step)
    den = lr * lr + li * li
    cr = (lr * (abar_r - 1.0) + li * abar_i) / den
    ci = (lr * abar_i - li * (abar_r - 1.0)) / den
    bbar_r = cr[..., None] * b_re - ci[..., None] * b_im
    bbar_i = cr[..., None] * b_im + ci[..., None] * b_re
    ns = S5_GROUPS // S5_SLAB_GROUPS
    eye = jnp.eye(S5_SLAB_GROUPS, dtype=jnp.float32)

    def in_mat(bb):
        t = jnp.einsum('ab,sapi->saibp', eye, bb.reshape(ns, S5_SLAB_GROUPS, S5_STATE, S5_GROUP))
        return t.reshape(ns, LANES, S5_SW)

    def out_mat(cc):
        t = jnp.einsum('ab,saop->sapbo', eye, cc.reshape(ns, S5_SLAB_GROUPS, S5_GROUP, S5_STATE))
        return t.reshape(ns, S5_SW, LANES)

    w_in = jnp.concatenate([in_mat(bbar_r), in_mat(bbar_i)], axis=2)
    w_out = jnp.concatenate([out_mat(c_re), -out_mat(c_im)], axis=1)
    n = jnp.arange(1, n_pow + 1, dtype=jnp.float32)[None, :, None]
    lrs = (lr * step).reshape(ns, 1, S5_SW)
    lis = (li * step).reshape(ns, 1, S5_SW)
    pm = jnp.exp(n * lrs)
    apow = jnp.concatenate([pm * jnp.cos(n * lis), pm * jnp.sin(n * lis)], axis=2)
    return _bf(w_in), _bf(w_out), apow


def _s5_mix(z, w_in, w_out, apow, d_skip):
    L = z.shape[0]
    tb = S5_TB
    assert L % tb == 0
    ns = S5_DIM // LANES
    f32 = jnp.float32
    return pl.pallas_call(
        _s5_kernel,
        out_shape=(jax.ShapeDtypeStruct((L, S5_DIM), f32), jax.ShapeDtypeStruct((L, S5_DIM), MXU_DTYPE)),
        grid=(ns, L // tb),
        in_specs=[
            pl.BlockSpec((tb, LANES), lambda s, i: (i, s)),
            pl.BlockSpec((1, LANES, 2 * S5_SW), lambda s, i: (s, 0, 0)),
            pl.BlockSpec((1, 2 * S5_SW, LANES), lambda s, i: (s, 0, 0)),
            pl.BlockSpec((1, tb // S5_SEG, 2 * S5_SW), lambda s, i: (s, 0, 0)),
            pl.BlockSpec((1, LANES), lambda s, i: (0, s)),
        ],
        out_specs=(pl.BlockSpec((tb, LANES), lambda s, i: (i, s)), pl.BlockSpec((tb, LANES), lambda s, i: (i, s))),
        scratch_shapes=[
            pltpu.VMEM((1, 2 * S5_SW), f32),
            pltpu.VMEM((tb, LANES), f32),
            pltpu.VMEM((tb, 2 * S5_SW), f32),
            pltpu.VMEM((tb, 2 * S5_SW), f32),
            pltpu.VMEM((tb, LANES), f32),
        ],
        compiler_params=_cparams(("parallel", "arbitrary")),
        name="s5_mix",
    )(z, w_in, w_out, apow, d_skip.reshape(1, S5_DIM))


RET_CHUNK = 512


def _ret_kernel(q_ref, k_ref, v_ref, g_ref, cos_ref, sin_ref, intra_ref, rowdec_ref, o_ref, s_s):
    c = pl.program_id(1)

    @pl.when(c == 0)
    def _():
        s_s[...] = jnp.zeros_like(s_s)

    half = RET_HEAD_DIM // 2
    cos, sin = cos_ref[...], sin_ref[...]

    def rot(x):
        x1, x2 = x[:, 0:half], x[:, half:2 * half]
        return jnp.concatenate([x1 * cos - x2 * sin, x1 * sin + x2 * cos], axis=1)

    q = rot(q_ref[...])
    k = rot(k_ref[...]) * (RET_HEAD_DIM ** -0.5)
    vb = _bf(v_ref[...])
    qb = _bf(q)
    dec = rowdec_ref[0]

    def lanes2(x):
        return jnp.concatenate([x, x], axis=1)

    xi, zeta, g_chunk = lanes2(dec[:, 0:LANES]), lanes2(dec[:, LANES:2 * LANES]), lanes2(dec[:, 2 * LANES:3 * LANES])
    att = _dot_nt(qb, _bf(k)) * intra_ref[0]
    s_old = s_s[...]
    o = _dot(_bf(att), vb) + _dot(qb, _bf(s_old)) * xi
    s_s[...] = s_old * g_chunk[0:1, :] + _dot(_bf((k * zeta).T), vb)
    mean = jnp.mean(o, axis=-1, keepdims=True)
    d = o - mean
    var = jnp.mean(d * d, axis=-1, keepdims=True)
    gate = g_ref[...]
    o_ref[...] = (gate * _sigmoid(gate) * (d * lax.rsqrt(var + RET_GN_EPS))).astype(o_ref.dtype)


def _ret_tables(L):
    C = RET_CHUNK
    inv = 1.0 / (RET_ROPE_BASE ** jnp.linspace(0.0, 1.0, RET_HEAD_DIM // 2, dtype=jnp.float32))
    ang = jnp.arange(L, dtype=jnp.float32)[:, None] * inv[None, :]
    log_g = jnp.log(1.0 - 2.0 ** (-5.0 - jnp.arange(RET_HEADS, dtype=jnp.float32)))
    pos = jnp.arange(C, dtype=jnp.float32)
    diff = pos[:, None] - pos[None, :]
    intra = jnp.where(diff >= 0, jnp.exp(jnp.maximum(diff, 0.0)[None] * log_g[:, None, None]), 0.0)
    xi = jnp.exp((pos + 1.0)[None, :] * log_g[:, None])
    zeta = jnp.exp((C - 1.0 - pos)[None, :] * log_g[:, None])
    g_chunk = jnp.broadcast_to(jnp.exp(C * log_g)[:, None], (RET_HEADS, C))
    rowdec = jnp.concatenate([jnp.broadcast_to(t[:, :, None], (RET_HEADS, C, LANES)) for t in (xi, zeta, g_chunk)], axis=2)
    return jnp.cos(ang), jnp.sin(ang), intra, rowdec


def _retention_mix(z, cos, sin, intra, rowdec):
    L = z.shape[0]
    C, hd = RET_CHUNK, RET_HEAD_DIM
    assert L % C == 0
    base = S5_DIM // hd

    def blk(j):
        return pl.BlockSpec((C, hd), lambda h, c: (c, base + j * RET_HEADS + h))

    return pl.pallas_call(
        _ret_kernel,
        out_shape=jax.ShapeDtypeStruct((L, RET_HEADS * hd), MXU_DTYPE),
        grid=(RET_HEADS, L // C),
        in_specs=[
            blk(0), blk(1), blk(2), blk(3),
            pl.BlockSpec((C, hd // 2), lambda h, c: (c, 0)),
            pl.BlockSpec((C, hd // 2), lambda h, c: (c, 0)),
            pl.BlockSpec((1, C, C), lambda h, c: (h, 0, 0)),
            pl.BlockSpec((1, C, 3 * LANES), lambda h, c: (h, 0, 0)),
        ],
        out_specs=pl.BlockSpec((C, hd), lambda h, c: (c, h)),
        scratch_shapes=[pltpu.VMEM((hd, hd), jnp.float32)],
        compiler_params=_cparams(("parallel", "arbitrary")),
        name="retention_mix",
    )(z, z, z, z, cos, sin, intra, rowdec)


def _lora_w_in(w):
    d = w.shape[0]
    o_ad = ZE_LORA + DECAY_RANK
    o_gd = o_ad + AAA_RANK
    parts = [
        w[:, ZE_LORA:o_ad], jnp.zeros((d, ZS_AD - ZS_WD - DECAY_RANK), w.dtype),
        w[:, o_ad:o_gd], jnp.zeros((d, ZS_GD - ZS_AD - AAA_RANK), w.dtype),
        w[:, o_gd:o_gd + GATE_RANK],
    ]
    out = jnp.concatenate(parts, axis=1)
    assert out.shape[1] == ZS_COLS
    return out


def _ffn(h, norm_g, w_gate, w_up, w_down_b, layer):
    hn = _rmsnorm(h, norm_g, MXU_DTYPE)
    mid = _matmul([hn], [(w_gate, layer, 0), (w_up, layer, 0)], [(0, 0), (0, 1)], [], _ep_swiglu, MXU_DTYPE,
                  "ffn_gate_up")
    return _matmul([mid], [(w_down_b, layer, 0)], [(0, 0)], [("tile", h)], _ep_resid, jnp.float32,
                   "ffn_down", tk=w_down_b.shape[1] // FFN_DOWN_K_SPLITS)


def _out_proj(h, o_first, o_second, w_out, layer):
    return _matmul([o_first, o_second], [(w_out, layer, 0), (w_out, layer, 1)], [(0, 0), (1, 1)], [("tile", h)],
                   _ep_resid2, jnp.float32, "out_proj")


def kernel(x, norm_mix, norm_ffn, ffn_gate, ffn_up, ffn_down, e_w_in, e_w_out, e_mu, e_w0, e_w_up, e_a0, e_a_up, e_g_up, e_k_k, e_k_a, e_r_k, e_ln_w, e_ln_b, o_w_in, o_w_out, o_lam_re, o_lam_im, o_log_step, o_b_re, o_b_im, o_c_re, o_c_im, o_d_skip, o_w_glu, o_b_glu, final_norm):
    assert x.shape[0] == 1
    h = x.reshape(x.shape[1], x.shape[2])
    L = h.shape[0]
    n_top = min(TOPK_MAX, L // 4)

    ffn_down_b = _bf(ffn_down)
    hn = _rmsnorm(h, norm_mix[0], MXU_DTYPE)
    z = _matmul([hn], [(e_w_in, 0, 0)], [(0, 0)], [], _ep_plain, jnp.float32, "even_in_proj", n_out=ZE_MAIN_COLS)
    z_lora = _matmul([hn], [(_lora_w_in(e_w_in[0]), 0, 0)], [(0, 0)], [], _ep_plain, jnp.float32, "even_lora_proj")
    tab_a = _rope_tables(L, A_ROT_HALF, A_HEAD_DIM, 2 * A_ROT_HALF)
    tab_i = _rope_tables(L, IDX_ROT_HALF, IDX_DIM, 2 * IDX_ROT_HALF)
    q_t, k_r, v_t, qi_t, ki_r, w_t = _dsa_prep(z, tab_a, tab_i)
    o_a = _dsa_attention(q_t, k_r, v_t, qi_t, ki_r, w_t, n_top)
    p_rows, mu_small, lora = _rwkv_params(e_mu[0], e_w0[0], e_w_up[0], e_a0[0], e_a_up[0], e_g_up[0], e_k_k[0],
                                          e_k_a[0], e_r_k[0], e_ln_w[0], e_ln_b[0])
    o_b = _rwkv_mix(z, z_lora, p_rows, mu_small, lora)
    h = _out_proj(h, o_a, o_b, e_w_out, 0)
    h = _ffn(h, norm_ffn[0], ffn_gate, ffn_up, ffn_down_b, 0)

    hn = _rmsnorm(h, norm_mix[1], MXU_DTYPE)
    z = _matmul([hn], [(o_w_in, 0, 0)], [(0, 0)], [], _ep_plain, jnp.float32, "odd_in_proj")
    s5_in, s5_out, s5_apow = _s5_params(o_lam_re[0], o_lam_im[0], o_log_step[0], o_b_re[0], o_b_im[0], o_c_re[0],
                                        o_c_im[0], S5_TB // S5_SEG)
    zg, zg_b = _s5_mix(z, s5_in, s5_out, s5_apow, o_d_skip[0])
    o_c = _matmul([zg_b], [(o_w_glu, 0, 0)], [(0, 0)], [("tile", zg), ("row", o_b_glu[0].reshape(1, S5_DIM))], _ep_glu,
                  MXU_DTYPE, "s5_glu")
    o_d = _retention_mix(z, *_ret_tables(L))
    h = _out_proj(h, o_c, o_d, o_w_out, 0)
    h = _ffn(h, norm_ffn[1], ffn_gate, ffn_up, ffn_down_b, 1)

    return _rmsnorm(h, final_norm, jnp.float32).reshape(x.shape)
```

```python
import functools
import math

import jax
import jax.numpy as jnp
from jax import lax
from jax.experimental import pallas as pl
from jax.experimental.pallas import tpu as pltpu

NORM_EPS = 1e-6
A_HEAD_DIM = 128
A_HEADS = 16
A_KV_HEADS = 4
A_GROUP = A_HEADS // A_KV_HEADS
A_ROT_HALF = 16
IDX_HEADS = 32
IDX_DIM = 64
IDX_ROT_HALF = 8
TOPK_MAX = 256
ROPE_THETA = 500000.0
RWKV_HEAD = 64
RWKV_DIM = 2048
DECAY_RANK = 96
AAA_RANK = 96
GATE_RANK = 256
RWKV_GN_EPS = 1e-5 * RWKV_HEAD
S5_DIM = 2048
S5_GROUP = 16
S5_GROUPS = 128
S5_STATE = 64
RET_HEAD_DIM = 256
RET_HEADS = 8
RET_ROPE_BASE = 10000.0
RET_GN_EPS = 1e-5

LANES = 128
SUBLANES = 8
VMEM_LIMIT_BYTES = 56 * 2**20

MXU_DTYPE = jnp.bfloat16
NEG_BIG = -1e30
INT_MIN = -2**31

ZE_Q, ZE_K, ZE_V, ZE_QI = 0, 2048, 2560, 3072
ZE_KI = 5120
ZE_WI_LANE = IDX_DIM
ZE_R, ZE_RK, ZE_RV = 5216, 7264, 9312
ZE_LORA = 11360
ZE_SKEW = ZE_R % LANES
ZE_MAIN_COLS = 11776
ZS_WD, ZS_AD, ZS_GD, ZS_COLS = 0, 128, 256, 512


def _cparams(sem):
    return pltpu.CompilerParams(dimension_semantics=sem, vmem_limit_bytes=VMEM_LIMIT_BYTES)


def _bf(x):
    return x.astype(MXU_DTYPE)


def _dot(a, b):
    return jnp.dot(a, b, preferred_element_type=jnp.float32)


def _dot_nt(a, b):
    return lax.dot_general(a, b, (((1,), (1,)), ((), ())), preferred_element_type=jnp.float32)


def _split3(x):
    h1 = _bf(x)
    r1 = x - h1.astype(jnp.float32)
    h2 = _bf(r1)
    h3 = _bf(r1 - h2.astype(jnp.float32))
    return h1, h2, h3


def _dot_x3(a, b):
    ah = _bf(a)
    al = _bf(a - ah.astype(jnp.float32))
    bh = _bf(b)
    bl = _bf(b - bh.astype(jnp.float32))
    return _dot(ah, bh) + _dot(ah, bl) + _dot(al, bh)


def _sigmoid(x):
    return 1.0 / (1.0 + jnp.exp(-x))


def _rmsnorm_kernel(x_ref, g_ref, o_ref):
    x = x_ref[...]
    ms = jnp.mean(x * x, axis=-1, keepdims=True)
    o_ref[...] = (x * lax.rsqrt(ms + NORM_EPS) * g_ref[...]).astype(o_ref.dtype)


def _rmsnorm(x, g, out_dtype, tm=512):
    m, d = x.shape
    return pl.pallas_call(
        _rmsnorm_kernel,
        out_shape=jax.ShapeDtypeStruct((m, d), out_dtype),
        grid=(m // tm,),
        in_specs=[pl.BlockSpec((tm, d), lambda i: (i, 0)), pl.BlockSpec((1, d), lambda i: (0, 0))],
        out_specs=pl.BlockSpec((tm, d), lambda i: (i, 0)),
        compiler_params=_cparams(("parallel",)),
        name="rmsnorm",
    )(x, g.reshape(1, d))


def _mm_kernel(*refs, n_a, n_w, n_e, pairs, epilogue, nk):
    a_refs = refs[:n_a]
    w_refs = refs[n_a:n_a + n_w]
    e_refs = refs[n_a + n_w:n_a + n_w + n_e]
    o_ref = refs[n_a + n_w + n_e]
    acc_refs = refs[n_a + n_w + n_e + 1:]
    if nk == 1:
        accs = [_dot(a_refs[i][...], _bf(w_refs[j][...])) for i, j in pairs]
        o_ref[...] = epilogue(accs, [e[...] for e in e_refs]).astype(o_ref.dtype)
        return
    k = pl.program_id(2)

    @pl.when(k == 0)
    def _():
        for acc in acc_refs:
            acc[...] = jnp.zeros_like(acc)

    for acc, (i, j) in zip(acc_refs, pairs):
        acc[...] += _dot(a_refs[i][...], _bf(w_refs[j][...]))

    @pl.when(k == nk - 1)
    def _():
        o_ref[...] = epilogue([acc[...] for acc in acc_refs], [e[...] for e in e_refs]).astype(o_ref.dtype)


def _weight_spec(w, layer, k_off, tk, tn):
    if w.ndim == 3:
        return pl.BlockSpec((None, tk, tn), lambda i, j, k: (layer, k + k_off, j))
    return pl.BlockSpec((tk, tn), lambda i, j, k: (k + k_off, j))


MM_TILES = {
    "even_in_proj": (1024, 512), "even_lora_proj": (1024, 512), "odd_in_proj": (1024, 512), "s5_glu": (1024, 512),
    "out_proj": (1024, 512), "ffn_gate_up": (2048, 256), "ffn_down": (512, 512),
}
FFN_DOWN_K_SPLITS = 1


def _matmul(a_list, w_list, pairs, extras, epilogue, out_dtype, name, tk=None, n_out=None):
    m, kdim = a_list[0].shape
    n = w_list[0][0].shape[-1] if n_out is None else n_out
    tk = kdim if tk is None else tk
    tm, tn = MM_TILES[name]
    tm = min(tm, m)
    nk = kdim // tk
    assert m % tm == 0 and n % tn == 0 and kdim % tk == 0
    in_specs = [pl.BlockSpec((tm, tk), lambda i, j, k: (i, k)) for _ in a_list]
    in_specs += [_weight_spec(w, layer, k_off, tk, tn) for w, layer, k_off in w_list]
    w_list = [w for w, _, _ in w_list]
    e_arrays = []
    for kind, arr in extras:
        e_arrays.append(arr)
        if kind == "tile":
            in_specs.append(pl.BlockSpec((tm, tn), lambda i, j, k: (i, j)))
        else:
            in_specs.append(pl.BlockSpec((1, tn), lambda i, j, k: (0, j)))
    scratch = [] if nk == 1 else [pltpu.VMEM((tm, tn), jnp.float32) for _ in pairs]
    kern = functools.partial(_mm_kernel, n_a=len(a_list), n_w=len(w_list), n_e=len(extras),
                             pairs=tuple(pairs), epilogue=epilogue, nk=nk)
    return pl.pallas_call(
        kern,
        out_shape=jax.ShapeDtypeStruct((m, n), out_dtype),
        grid=(m // tm, n // tn, nk),
        in_specs=in_specs,
        out_specs=pl.BlockSpec((tm, tn), lambda i, j, k: (i, j)),
        scratch_shapes=scratch,
        compiler_params=_cparams(("parallel", "parallel", "arbitrary")),
        name=name,
    )(*a_list, *w_list, *e_arrays)


def _ep_plain(accs, ex):
    return accs[0]


def _ep_swiglu(accs, ex):
    g = accs[0]
    return g * _sigmoid(g) * accs[1]


def _ep_resid(accs, ex):
    return ex[0] + accs[0]


def _ep_resid2(accs, ex):
    return ex[0] + (accs[0] + accs[1])


def _ep_glu(accs, ex):
    zg, b = ex
    return zg * _sigmoid(accs[0] + b)


def _rope_tile(x, tc, s1, s2, n_half):
    return x * tc + pltpu.roll(x, n_half, 1) * s2 + pltpu.roll(x, LANES - n_half, 1) * s1


DSA_VPAD = 16
DSA_VROWS = A_HEAD_DIM + DSA_VPAD


def _dsa_prep_kernel(zq_ref, zs_ref, ta_ref, ti_ref, qt_ref, k_ref, vt_ref, qit_ref, ki_ref, wt_ref):
    ta_c, ta_1, ta_2 = ta_ref[0], ta_ref[1], ta_ref[2]
    ti_c, ti_1, ti_2 = ti_ref[0], ti_ref[1], ti_ref[2]
    q_scale = A_HEAD_DIM ** -0.5 * math.log2(math.e)
    for h in range(A_HEADS):
        x = zq_ref[:, ZE_Q + h * LANES:ZE_Q + (h + 1) * LANES]
        y = _rope_tile(x, ta_c, ta_1, ta_2, A_ROT_HALF) * q_scale
        qt_ref[h * LANES:(h + 1) * LANES, :] = _bf(y.T)
    for g in range(A_KV_HEADS):
        x = zq_ref[:, ZE_K + g * LANES:ZE_K + (g + 1) * LANES]
        k_ref[:, g * LANES:(g + 1) * LANES] = _bf(_rope_tile(x, ta_c, ta_1, ta_2, A_ROT_HALF))
        v = zq_ref[:, ZE_V + g * LANES:ZE_V + (g + 1) * LANES]
        vt_ref[g * DSA_VROWS:g * DSA_VROWS + A_HEAD_DIM, :] = _bf(v.T)
        vt_ref[g * DSA_VROWS + A_HEAD_DIM:(g + 1) * DSA_VROWS, :] = jnp.ones((DSA_VPAD, v.shape[0]), MXU_DTYPE)
    for c in range(IDX_HEADS * IDX_DIM // LANES):
        x = zq_ref[:, ZE_QI + c * LANES:ZE_QI + (c + 1) * LANES]
        y = _rope_tile(x, ti_c, ti_1, ti_2, IDX_ROT_HALF)
        qit_ref[c * LANES:(c + 1) * LANES, :] = _bf(y.T)
    ki_ref[...] = _bf(_rope_tile(zs_ref[...], ti_c, ti_1, ti_2, IDX_ROT_HALF))
    w_scale = (IDX_DIM ** -0.5) * (IDX_HEADS ** -0.5)
    wt_ref[...] = (zs_ref[...] * w_scale).T


def _dsa_prep(z, tab_a, tab_i, tm=512):
    L = z.shape[0]
    n_q = A_HEADS * A_HEAD_DIM
    n_kv = A_KV_HEADS * A_HEAD_DIM
    n_qi = IDX_HEADS * IDX_DIM
    out_shape = (
        jax.ShapeDtypeStruct((n_q, L), MXU_DTYPE),
        jax.ShapeDtypeStruct((L, n_kv), MXU_DTYPE),
        jax.ShapeDtypeStruct((A_KV_HEADS * DSA_VROWS, L), MXU_DTYPE),
        jax.ShapeDtypeStruct((n_qi, L), MXU_DTYPE),
        jax.ShapeDtypeStruct((L, LANES), MXU_DTYPE),
        jax.ShapeDtypeStruct((LANES, L), jnp.float32),
    )
    return pl.pallas_call(
        _dsa_prep_kernel,
        out_shape=out_shape,
        grid=(L // tm,),
        in_specs=[
            pl.BlockSpec((tm, ZE_KI), lambda i: (i, 0)),
            pl.BlockSpec((tm, LANES), lambda i: (i, ZE_KI // LANES)),
            pl.BlockSpec((3, tm, LANES), lambda i: (0, i, 0)),
            pl.BlockSpec((3, tm, LANES), lambda i: (0, i, 0)),
        ],
        out_specs=(
            pl.BlockSpec((n_q, tm), lambda i: (0, i)),
            pl.BlockSpec((tm, n_kv), lambda i: (i, 0)),
            pl.BlockSpec((A_KV_HEADS * DSA_VROWS, tm), lambda i: (0, i)),
            pl.BlockSpec((n_qi, tm), lambda i: (0, i)),
            pl.BlockSpec((tm, LANES), lambda i: (i, 0)),
            pl.BlockSpec((LANES, tm), lambda i: (0, i)),
        ),
        compiler_params=_cparams(("parallel",)),
        name="dsa_prep",
    )(z, z, tab_a, tab_i)


DSA_TQ = 128
DSA_KC = 512
DSA_KA = 256
DSA_GROUP_SETS = ((0, 1, 2, 3),)


def _dsa_kernel(qt_ref, qit_ref, wt_ref, ki_ref, k_ref, vt_ref, o_ref,
                qi_s, qg_s, sc_s, key_s, m_s, acc_s, *, n_top):
    qb = pl.program_id(0)
    q0 = qb * DSA_TQ
    n_chunks = (q0 + DSA_TQ + DSA_KC - 1) // DSA_KC
    q_pos = q0 + lax.broadcasted_iota(jnp.int32, (1, DSA_TQ), 1)

    for h in range(IDX_HEADS):
        qi_s[:, h * DSA_TQ:(h + 1) * DSA_TQ] = qit_ref[h * IDX_DIM:(h + 1) * IDX_DIM, :]
    for g in range(A_KV_HEADS):
        for j in range(A_GROUP):
            h = g * A_GROUP + j
            qg_s[g, :, j * DSA_TQ:(j + 1) * DSA_TQ] = qt_ref[h * A_HEAD_DIM:(h + 1) * A_HEAD_DIM, :]

    def score_chunk(c, carry):
        for sub in range(DSA_KC // DSA_KA):
            r0 = pl.multiple_of(c * DSA_KC + sub * DSA_KA, DSA_KA)
            kic = ki_ref[pl.ds(r0, DSA_KA), 0:IDX_DIM]
            logits = _dot(kic, qi_s[...])
            acc = jnp.zeros((DSA_KA, DSA_TQ), jnp.float32)
            for h in range(IDX_HEADS):
                w_h = wt_ref[ZE_WI_LANE + h:ZE_WI_LANE + h + 1, :]
                acc = acc + jnp.maximum(logits[:, h * DSA_TQ:(h + 1) * DSA_TQ], 0.0) * w_h
            bits = lax.bitcast_convert_type(acc, jnp.int32)
            okey = bits ^ ((bits >> 31) & jnp.int32(0x7FFFFFFF))
            k_pos = r0 + lax.broadcasted_iota(jnp.int32, (DSA_KA, 1), 0)
            key_s[pl.ds(r0, DSA_KA), :] = jnp.where(k_pos <= q_pos, okey, jnp.int32(INT_MIN))
        return carry

    lax.fori_loop(0, n_chunks, score_chunk, 0)

    def count_keys(cand, strict):
        def body(c, cnt):
            r0 = pl.multiple_of(c * DSA_KC, DSA_KC)
            blk = key_s[pl.ds(r0, DSA_KC), :]
            hit = jnp.where((blk > cand) if strict else (blk >= cand), jnp.int32(1), jnp.int32(0))
            return cnt + jnp.sum(hit.reshape(DSA_KC // SUBLANES, SUBLANES, DSA_TQ), axis=0)
        cnt8 = lax.fori_loop(0, n_chunks, body, jnp.zeros((SUBLANES, DSA_TQ), jnp.int32))
        return jnp.sum(cnt8, axis=0, keepdims=True)

    def bisect(it, u):
        cand_u = u | lax.shift_left(jnp.int32(1), 31 - it)
        cnt = count_keys(cand_u ^ jnp.int32(INT_MIN), False)
        return jnp.where(cnt >= n_top, cand_u, u)

    thr = lax.fori_loop(0, 32, bisect, jnp.zeros((1, DSA_TQ), jnp.int32)) ^ jnp.int32(INT_MIN)
    thr = jnp.maximum(thr, jnp.int32(INT_MIN + 1))
    n_sel = count_keys(thr, False)

    def bias_plain():
        def body(c, carry):
            r0 = pl.multiple_of(c * DSA_KC, DSA_KC)
            sc_s[pl.ds(r0, DSA_KC), :] = jnp.where(key_s[pl.ds(r0, DSA_KC), :] >= thr, 0.0, NEG_BIG)
            return carry
        lax.fori_loop(0, n_chunks, body, 0)

    def bias_ties():
        need = (n_top - count_keys(thr, True)).astype(jnp.float32)
        rr = lax.broadcasted_iota(jnp.int32, (DSA_KC, DSA_KC), 0)
        cc = lax.broadcasted_iota(jnp.int32, (DSA_KC, DSA_KC), 1)
        before = jnp.where(cc < rr, 1.0, 0.0).astype(MXU_DTYPE)

        def body(c, seen):
            r0 = pl.multiple_of(c * DSA_KC, DSA_KC)
            blk = key_s[pl.ds(r0, DSA_KC), :]
            tie = jnp.where(blk == thr, 1.0, 0.0)
            rank = _dot(before, _bf(tie)) + seen
            sel = jnp.logical_or(blk > thr, jnp.logical_and(blk == thr, rank < need))
            sc_s[pl.ds(r0, DSA_KC), :] = jnp.where(sel, 0.0, NEG_BIG)
            return seen + jnp.sum(tie, axis=0, keepdims=True)
        lax.fori_loop(0, n_chunks, body, jnp.zeros((1, DSA_TQ), jnp.float32))

    lax.cond(jnp.max(n_sel) > n_top, bias_ties, bias_plain)

    m_s[...] = jnp.full(m_s.shape, NEG_BIG, jnp.float32)
    acc_s[...] = jnp.zeros(acc_s.shape, jnp.float32)

    def attn_chunk(c, carry, groups):
        r0 = pl.multiple_of(c * DSA_KC, DSA_KC)
        bias = sc_s[pl.ds(r0, DSA_KC), :]
        bias_g = jnp.concatenate([bias] * A_GROUP, axis=1)
        s = {g: _dot(k_ref[pl.ds(r0, DSA_KC), g * A_HEAD_DIM:(g + 1) * A_HEAD_DIM], qg_s[g]) + bias_g for g in groups}
        m_old = {g: m_s[g] for g in groups}
        m_new = {g: jnp.maximum(m_old[g], jnp.max(s[g], axis=0, keepdims=True)) for g in groups}
        alpha = {g: jnp.exp2(m_old[g] - m_new[g]) for g in groups}
        p = {g: _bf(jnp.exp2(s[g] - m_new[g])) for g in groups}
        pv = {g: _dot(vt_ref[g * DSA_VROWS:(g + 1) * DSA_VROWS, pl.ds(r0, DSA_KC)], p[g]) for g in groups}
        acc = {g: alpha[g] * acc_s[g] + pv[g] for g in groups}
        for g in groups:
            acc_s[g] = acc[g]
            m_s[g] = m_new[g]
        return carry

    for groups in DSA_GROUP_SETS:
        lax.fori_loop(0, n_chunks, functools.partial(attn_chunk, groups=groups), 0)
    for g in range(A_KV_HEADS):
        acc = acc_s[g]
        o_t = acc[0:A_HEAD_DIM, :] / acc[A_HEAD_DIM:A_HEAD_DIM + 1, :]
        for j in range(A_GROUP):
            h = g * A_GROUP + j
            o_ref[:, h * A_HEAD_DIM:(h + 1) * A_HEAD_DIM] = _bf(o_t[:, j * DSA_TQ:(j + 1) * DSA_TQ].T)


def _dsa_attention(q_t, k_r, v_t, qi_t, ki_r, w_t, n_top):
    L = k_r.shape[0]
    n_q = A_HEADS * A_HEAD_DIM
    n_kv = A_KV_HEADS * A_HEAD_DIM
    n_qi = IDX_HEADS * IDX_DIM
    assert L % DSA_KC == 0
    resident = dict(pipeline_mode=pl.Buffered(1))
    return pl.pallas_call(
        functools.partial(_dsa_kernel, n_top=n_top),
        out_shape=jax.ShapeDtypeStruct((L, n_q), MXU_DTYPE),
        grid=(L // DSA_TQ,),
        in_specs=[
            pl.BlockSpec((n_q, DSA_TQ), lambda i: (0, i)),
            pl.BlockSpec((n_qi, DSA_TQ), lambda i: (0, i)),
            pl.BlockSpec((LANES, DSA_TQ), lambda i: (0, i)),
            pl.BlockSpec((L, LANES), lambda i: (0, 0), **resident),
            pl.BlockSpec((L, n_kv), lambda i: (0, 0), **resident),
            pl.BlockSpec((A_KV_HEADS * DSA_VROWS, L), lambda i: (0, 0), **resident),
        ],
        out_specs=pl.BlockSpec((DSA_TQ, n_q), lambda i: (i, 0)),
        scratch_shapes=[
            pltpu.VMEM((IDX_DIM, IDX_HEADS * DSA_TQ), MXU_DTYPE),
            pltpu.VMEM((A_KV_HEADS, A_HEAD_DIM, A_GROUP * DSA_TQ), MXU_DTYPE),
            pltpu.VMEM((L, DSA_TQ), jnp.float32),
            pltpu.VMEM((L, DSA_TQ), jnp.int32),
            pltpu.VMEM((A_KV_HEADS, 1, A_GROUP * DSA_TQ), jnp.float32),
            pltpu.VMEM((A_KV_HEADS, DSA_VROWS, A_GROUP * DSA_TQ), jnp.float32),
        ],
        compiler_params=_cparams(("arbitrary",)),
        name="dsa_attention",
    )(q_t, qi_t, w_t, ki_r, k_r, v_t)


def _rope_tables(L, n_half, period, theta_pow_dim):
    inv = ROPE_THETA ** (-jnp.arange(0, theta_pow_dim, 2, dtype=jnp.float32) / theta_pow_dim)
    ang = jnp.arange(L, dtype=jnp.float32)[:, None] * inv[None, :]
    cos, sin = jnp.cos(ang), jnp.sin(ang)
    pad = period - 2 * n_half
    tc = jnp.concatenate([cos, cos, jnp.ones((L, pad), jnp.float32)], axis=1)
    s1 = jnp.concatenate([-sin, jnp.zeros((L, period - n_half), jnp.float32)], axis=1)
    s2 = jnp.concatenate([jnp.zeros((L, n_half), jnp.float32), sin, jnp.zeros((L, pad), jnp.float32)], axis=1)
    reps = LANES // period
    return jnp.stack([jnp.tile(t, (1, reps)) for t in (tc, s1, s2)])


RWKV_CHUNK = 64
RWKV_SLAB = 1024
RWKV_TB = 256
P_MU_R, P_MU_K, P_MU_V, P_W0, P_A0, P_KK, P_KA, P_RK, P_LNW, P_LNB = range(10)


def _head_ones():
    r = lax.broadcasted_iota(jnp.int32, (LANES, LANES), 0) // RWKV_HEAD
    c = lax.broadcasted_iota(jnp.int32, (LANES, LANES), 1) // RWKV_HEAD
    return jnp.where(r == c, 1.0, 0.0).astype(MXU_DTYPE)


def _head_sum(x, ones_bd):
    hi = _bf(x)
    lo = _bf(x - hi.astype(jnp.float32))
    tiles = [_dot(hi[:, t * LANES:(t + 1) * LANES], ones_bd) + _dot(lo[:, t * LANES:(t + 1) * LANES], ones_bd)
             for t in range(x.shape[1] // LANES)]
    return jnp.concatenate(tiles, axis=1)


def _pair_stack(x):
    lane = lax.broadcasted_iota(jnp.int32, x.shape, 1)
    return jnp.concatenate([jnp.where(lane < RWKV_HEAD, x, 0.0), jnp.where(lane >= RWKV_HEAD, x, 0.0)], axis=0)


def _rwkv_kernel(zr_ref, zrx_ref, zk_ref, zkx_ref, zv_ref, zvx_ref, zs_ref,
                 hr_ref, hrx_ref, hk_ref, hkx_ref, hv_ref, hvx_ref, hs_ref, p_ref, mus_ref, lora_ref,
                 o_ref, h_s, r_s, lw_s, cum_s, k2_s, v_s, kk_s, a_s, g_s, bon_s, oo_s):
    C = RWKV_CHUNK
    tb = zr_ref.shape[0]
    i = pl.program_id(1)

    @pl.when(i == 0)
    def _():
        h_s[...] = jnp.zeros_like(h_s)

    row = lax.broadcasted_iota(jnp.int32, (tb, 1), 0)
    has_prev = jnp.where(i > 0, 1.0, 0.0)

    def unskew(main_ref, next_ref):
        x = jnp.concatenate([main_ref[...], next_ref[...]], axis=1)
        return pltpu.roll(x, x.shape[1] - ZE_SKEW, 1)[:, 0:main_ref.shape[1]]

    def mix(z, halo, mu):
        prev = halo[SUBLANES - 1:SUBLANES, :] * has_prev
        shifted = jnp.where(row == 0, prev, pltpu.roll(z, 1, 0))
        return z + (shifted - z) * mu

    r = mix(unskew(zr_ref, zrx_ref), unskew(hr_ref, hrx_ref), p_ref[P_MU_R:P_MU_R + 1, :])
    k = mix(unskew(zk_ref, zkx_ref), unskew(hk_ref, hkx_ref), p_ref[P_MU_K:P_MU_K + 1, :])
    v = mix(unskew(zv_ref, zvx_ref), unskew(hv_ref, hvx_ref), p_ref[P_MU_V:P_MU_V + 1, :])
    sm = mix(zs_ref[...], hs_ref[...], mus_ref[...])
    wd, ad, gd = sm[:, 0:LANES], sm[:, LANES:2 * LANES], sm[:, 2 * LANES:4 * LANES]
    x = -(p_ref[P_W0:P_W0 + 1, :] + _dot_x3(jnp.tanh(wd), lora_ref[0:LANES, :]))
    softplus = jnp.maximum(x, 0.0) + jnp.log(1.0 + jnp.exp(-jnp.abs(x)))
    lw = -jnp.exp(-softplus - 0.5)
    a = _sigmoid(p_ref[P_A0:P_A0 + 1, :] + _dot(_bf(ad), _bf(lora_ref[LANES:2 * LANES, :])))
    g_s[...] = _dot(_bf(_sigmoid(gd)), _bf(lora_ref[2 * LANES:4 * LANES, :]))
    ones_bd = _head_ones()
    kk = k * p_ref[P_KK:P_KK + 1, :]
    kk = kk * lax.rsqrt(jnp.maximum(_head_sum(kk * kk, ones_bd), 1e-24))
    k2 = k * (1.0 + (a - 1.0) * p_ref[P_KA:P_KA + 1, :])
    bon_s[...] = _head_sum(r * k2 * p_ref[P_RK:P_RK + 1, :], ones_bd)
    tr = lax.broadcasted_iota(jnp.int32, (tb, tb), 0)
    tc = lax.broadcasted_iota(jnp.int32, (tb, tb), 1)
    tril_bd = jnp.where(jnp.logical_and(tc <= tr, tc // C == tr // C), 1.0, 0.0).astype(MXU_DTYPE)
    l1, l2, l3 = _split3(lw)
    cum_s[...] = _dot(tril_bd, l1) + _dot(tril_bd, l2) + _dot(tril_bd, l3)
    r_s[...] = r
    lw_s[...] = lw
    k2_s[...] = k2
    v_s[...] = v
    kk_s[...] = kk
    a_s[...] = a

    n2 = 2 * C
    rr = lax.broadcasted_iota(jnp.int32, (n2, n2), 0)
    cc = lax.broadcasted_iota(jnp.int32, (n2, n2), 1)
    strict = (cc % C) < (rr % C)
    incl = (cc % C) <= (rr % C)
    eye = jnp.where(rr == cc, 1.0, 0.0)

    n_pairs = RWKV_SLAB // LANES
    pairs = range(n_pairs)

    def chunk(c, carry):
        r0 = pl.multiple_of(c * C, C)

        def ld(ref):
            return [ref[pl.ds(r0, C), p * LANES:(p + 1) * LANES] for p in pairs]

        rc, lwc, kc, vc, kkc, ac, cum = ld(r_s), ld(lw_s), ld(k2_s), ld(v_s), ld(kk_s), ld(a_s), ld(cum_s)
        h_old = [h_s[p] for p in pairs]
        cl = [cum[p][C - 1:C, :] for p in pairs]
        e_neg = [jnp.exp(-cum[p]) for p in pairs]
        e_end = [jnp.exp(cl[p] - cum[p]) for p in pairs]
        beta = [kkc[p] * ac[p] for p in pairs]
        ae_s = [_pair_stack(-kkc[p] * jnp.exp(cum[p] - lwc[p])) for p in pairs]
        rp_s = [_pair_stack(rc[p] * jnp.exp(cum[p])) for p in pairs]
        bm_s = [_pair_stack(beta[p] * e_neg[p]) for p in pairs]
        km_s = [_pair_stack(kc[p] * e_neg[p]) for p in pairs]
        bt_t = [_bf(_pair_stack(beta[p] * e_end[p]).T) for p in pairs]
        kt_t = [_bf(_pair_stack(kc[p] * e_end[p]).T) for p in pairs]
        v_st = [_bf(_pair_stack(vc[p])) for p in pairs]
        aa = [_dot_nt(_bf(jnp.concatenate([ae_s[p], rp_s[p]], axis=0)),
                      _bf(jnp.concatenate([bm_s[p], km_s[p]], axis=0))) for p in pairs]
        a_ab = [jnp.where(strict, aa[p][0:n2, 0:n2], 0.0) for p in pairs]
        a_ak = [_bf(jnp.where(strict, aa[p][0:n2, n2:2 * n2], 0.0)) for p in pairs]
        a_rb = [_bf(jnp.where(incl, aa[p][n2:2 * n2, 0:n2], 0.0)) for p in pairs]
        a_rk = [_bf(jnp.where(incl, aa[p][n2:2 * n2, n2:2 * n2], 0.0)) for p in pairs]
        t_inv = [eye + a_ab[p] for p in pairs]
        n_b = [_bf(a_ab[p]) for p in pairs]
        n_pow = [_dot(n_b[p], n_b[p]) for p in pairs]
        for _ in range(int(math.log2(C)) - 2):
            n_b = [_bf(n_pow[p]) for p in pairs]
            both = [_dot(jnp.concatenate([_bf(t_inv[p]), n_b[p]], axis=0), n_b[p]) for p in pairs]
            t_inv = [t_inv[p] + both[p][0:n2, :] for p in pairs]
            n_pow = [both[p][n2:2 * n2, :] for p in pairs]
        t_inv = [t_inv[p] + _dot(_bf(t_inv[p]), _bf(n_pow[p])) for p in pairs]
        t_b = [_bf(t_inv[p]) for p in pairs]
        w_k = [_bf(_dot(a_ak[p], v_st[p])) for p in pairs]
        t_rhs = [_dot(t_b[p], jnp.concatenate([_bf(ae_s[p]), w_k[p]], axis=1)) for p in pairs]
        a_til = [_bf(t_rhs[p][:, 0:LANES]) for p in pairs]
        v_til = [t_rhs[p][:, LANES:2 * LANES] for p in pairs]
        o_intra = [_dot(a_rk[p], v_st[p]) for p in pairs]
        h_kv = [_dot(kt_t[p], v_st[p]) for p in pairs]
        decay_col = [jnp.exp(jnp.broadcast_to(cl[p], (LANES, LANES)).T) for p in pairs]
        h_b = [_bf(h_old[p]) for p in pairs]
        u_b = [_bf(_dot(a_til[p], h_b[p]) + v_til[p]) for p in pairs]
        o_st = [_dot(_bf(rp_s[p]), h_b[p]) + _dot(a_rb[p], u_b[p]) + o_intra[p] for p in pairs]
        h_new = [decay_col[p] * h_old[p] + _dot(bt_t[p], u_b[p]) + h_kv[p] for p in pairs]
        for p in pairs:
            h_s[p] = h_new[p]
            oo_s[pl.ds(r0, C), p * LANES:(p + 1) * LANES] = o_st[p][0:C, :] + o_st[p][C:n2, :]
        return carry

    lax.fori_loop(0, tb // C, chunk, 0)

    o = oo_s[...]
    mean = _head_sum(o, ones_bd) * (1.0 / RWKV_HEAD)
    d = o - mean
    var = _head_sum(d * d, ones_bd) * (1.0 / RWKV_HEAD)
    y = d * lax.rsqrt(var + RWKV_GN_EPS) * p_ref[P_LNW:P_LNW + 1, :] + p_ref[P_LNB:P_LNB + 1, :]
    y = y + bon_s[...] * v_s[...]
    o_ref[...] = (y * g_s[...]).astype(o_ref.dtype)


def _rwkv_params(mu, w0, w_up, a0, a_up, g_up, k_k, k_a, r_k, ln_w, ln_b):
    d = RWKV_DIM
    mu_r, mu_k, mu_v = mu[0:d], mu[d:2 * d], mu[2 * d:3 * d]
    o = 3 * d
    mu_wd, mu_ad, mu_gd = mu[o:o + DECAY_RANK], mu[o + DECAY_RANK:o + DECAY_RANK + AAA_RANK], mu[o + DECAY_RANK + AAA_RANK:]
    rows = [mu_r, mu_k, mu_v, w0, a0, k_k, k_a, r_k.reshape(d), ln_w, ln_b]
    p_rows = jnp.concatenate([jnp.stack(rows), jnp.zeros((16 - len(rows), d), jnp.float32)], axis=0)

    def pad_to(x, n, axis):
        widths = [(0, 0)] * x.ndim
        widths[axis] = (0, n - x.shape[axis])
        return jnp.pad(x, widths)

    mu_small = jnp.concatenate([pad_to(mu_wd, LANES, 0), pad_to(mu_ad, LANES, 0), mu_gd]).reshape(1, 4 * LANES)
    lora = jnp.concatenate([pad_to(w_up, LANES, 0), pad_to(a_up, LANES, 0), g_up], axis=0)
    return p_rows, mu_small, lora


def _rwkv_mix(z, z_lora, p_rows, mu_small, lora):
    L = z.shape[0]
    tb, w = RWKV_TB, RWKV_SLAB
    assert L % tb == 0
    hb = tb // SUBLANES
    tiles = w // LANES

    def prev_rows(i):
        return jnp.maximum(i * hb - 1, 0)

    def windows(col0, rows, row_map):
        base = col0 - ZE_SKEW
        assert base % w == 0
        return [pl.BlockSpec((rows, w), lambda s, i: (row_map(i), base // w + s)),
                pl.BlockSpec((rows, LANES), lambda s, i: (row_map(i), base // LANES + (s + 1) * tiles))]

    body = [spec for c in (ZE_R, ZE_RK, ZE_RV) for spec in windows(c, tb, lambda i: i)]
    halo = [spec for c in (ZE_R, ZE_RK, ZE_RV) for spec in windows(c, SUBLANES, prev_rows)]
    f32 = jnp.float32
    return pl.pallas_call(
        _rwkv_kernel,
        out_shape=jax.ShapeDtypeStruct((L, RWKV_DIM), MXU_DTYPE),
        grid=(RWKV_DIM // w, L // tb),
        in_specs=body + [pl.BlockSpec((tb, ZS_COLS), lambda s, i: (i, 0))]
        + halo + [pl.BlockSpec((SUBLANES, ZS_COLS), lambda s, i: (prev_rows(i), 0))]
        + [
            pl.BlockSpec((16, w), lambda s, i: (0, s)),
            pl.BlockSpec((1, ZS_COLS), lambda s, i: (0, 0)),
            pl.BlockSpec((ZS_COLS, w), lambda s, i: (0, s)),
        ],
        out_specs=pl.BlockSpec((tb, w), lambda s, i: (i, s)),
        scratch_shapes=[pltpu.VMEM((w // LANES, LANES, LANES), f32)] + [pltpu.VMEM((tb, w), f32) for _ in range(10)],
        compiler_params=_cparams(("parallel", "arbitrary")),
        name="rwkv7_mix",
    )(*([z] * 6), z_lora, *([z] * 6), z_lora, p_rows, mu_small, lora)


S5_TB = 1024
S5_SEG = SUBLANES
S5_SLAB_GROUPS = LANES // S5_GROUP
S5_SW = S5_SLAB_GROUPS * S5_STATE


def _cmul_add(ar, ai, xr, xi, br, bi):
    return ar * xr - ai * xi + br, ar * xi + ai * xr + bi


def _s5_kernel(u_ref, w_ref, c_ref, apow_ref, d_ref, zg_ref, zgb_ref, st_s, up_s, bu_s, x_s, y_s):
    tb = u_ref.shape[0]
    ts = tb // S5_SEG
    sw = S5_SW
    i = pl.program_id(1)

    @pl.when(i == 0)
    def _():
        st_s[...] = jnp.zeros_like(st_s)

    for tau in range(ts):
        up_s[tau * S5_SEG:(tau + 1) * S5_SEG, :] = u_ref[pl.ds(tau, S5_SEG, stride=ts), :]
    bu_s[...] = _dot(_bf(up_s[...]), _bf(w_ref[0]))
    a1 = apow_ref[0, 0:1, :]
    ar = jnp.broadcast_to(a1[:, 0:sw], (S5_SEG, sw))
    ai = jnp.broadcast_to(a1[:, sw:2 * sw], (S5_SEG, sw))

    def scan(tau, x):
        r0 = pl.multiple_of(tau * S5_SEG, S5_SEG)
        b = bu_s[pl.ds(r0, S5_SEG), :]
        nr, ni = _cmul_add(ar, ai, x[0], x[1], b[:, 0:sw], b[:, sw:2 * sw])
        x_s[pl.ds(r0, S5_SEG), :] = jnp.concatenate([nr, ni], axis=1)
        return nr, ni

    zero = jnp.zeros((S5_SEG, sw), jnp.float32)
    xr, xi = lax.fori_loop(0, ts, scan, (zero, zero), unroll=4)

    a_ts = apow_ref[0, ts - 1:ts, :]
    tr, ti = a_ts[:, 0:sw], a_ts[:, sw:2 * sw]
    cr, ci = st_s[:, 0:sw], st_s[:, sw:2 * sw]
    ent_r, ent_i = [], []
    for s in range(S5_SEG):
        ent_r.append(cr)
        ent_i.append(ci)
        cr, ci = _cmul_add(tr, ti, cr, ci, xr[s:s + 1, :], xi[s:s + 1, :])
    st_s[...] = jnp.concatenate([cr, ci], axis=1)
    er = jnp.concatenate(ent_r, axis=0)
    ei = jnp.concatenate(ent_i, axis=0)

    def fix(tau, carry):
        r0 = pl.multiple_of(tau * S5_SEG, S5_SEG)
        ap = apow_ref[0, pl.ds(tau, 1), :]
        x = x_s[pl.ds(r0, S5_SEG), :]
        nr, ni = _cmul_add(ap[:, 0:sw], ap[:, sw:2 * sw], er, ei, x[:, 0:sw], x[:, sw:2 * sw])
        x_s[pl.ds(r0, S5_SEG), :] = jnp.concatenate([nr, ni], axis=1)
        return carry

    lax.fori_loop(0, ts, fix, 0, unroll=4)
    c_out = _bf(c_ref[0])
    half = tb // 2
    yp = jnp.concatenate([_dot(_bf(x_s[0:half, :]), c_out), _dot(_bf(x_s[half:tb, :]), c_out)], axis=0)
    for tau in range(ts):
        y_s[pl.ds(tau, S5_SEG, stride=ts), :] = yp[tau * S5_SEG:(tau + 1) * S5_SEG, :]
    y = y_s[...] + d_ref[...] * u_ref[...]
    zg = 0.5 * y * (1.0 + jnp.tanh(math.sqrt(2.0 / math.pi) * (y + 0.044715 * (y * y * y))))
    zg_ref[...] = zg
    zgb_ref[...] = zg.astype(zgb_ref.dtype)


def _s5_params(lam_re, lam_im, log_step, b_re, b_im, c_re, c_im, n_pow):
    lr = jnp.minimum(lam_re, -1e-4)
    li = lam_im
    step = jnp.exp(log_step)[:, None]
    mag = jnp.exp(lr * step)
    abar_r = mag * jnp.cos(li * step)
    abar_i = mag * jnp.sin(li * step)
    den = lr * lr + li * li
    cr = (lr * (abar_r - 1.0) + li * abar_i) / den
    ci = (lr * abar_i - li * (abar_r - 1.0)) / den
    bbar_r = cr[..., None] * b_re - ci[..., None] * b_im
    bbar_i = cr[..., None] * b_im + ci[..., None] * b_re
    ns = S5_GROUPS // S5_SLAB_GROUPS
    eye = jnp.eye(S5_SLAB_GROUPS, dtype=jnp.float32)

    def in_mat(bb):
        t = jnp.einsum('ab,sapi->saibp', eye, bb.reshape(ns, S5_SLAB_GROUPS, S5_STATE, S5_GROUP))
        return t.reshape(ns, LANES, S5_SW)

    def out_mat(cc):
        t = jnp.einsum('ab,saop->sapbo', eye, cc.reshape(ns, S5_SLAB_GROUPS, S5_GROUP, S5_STATE))
        return t.reshape(ns, S5_SW, LANES)

    w_in = jnp.concatenate([in_mat(bbar_r), in_mat(bbar_i)], axis=2)
    w_out = jnp.concatenate([out_mat(c_re), -out_mat(c_im)], axis=1)
    n = jnp.arange(1, n_pow + 1, dtype=jnp.float32)[None, :, None]
    lrs = (lr * step).reshape(ns, 1, S5_SW)
    lis = (li * step).reshape(ns, 1, S5_SW)
    pm = jnp.exp(n * lrs)
    apow = jnp.concatenate([pm * jnp.cos(n * lis), pm * jnp.sin(n * lis)], axis=2)
    return _bf(w_in), _bf(w_out), apow


def _s5_mix(z, w_in, w_out, apow, d_skip):
    L = z.shape[0]
    tb = S5_TB
    assert L % tb == 0
    ns = S5_DIM // LANES
    f32 = jnp.float32
    return pl.pallas_call(
        _s5_kernel,
        out_shape=(jax.ShapeDtypeStruct((L, S5_DIM), f32), jax.ShapeDtypeStruct((L, S5_DIM), MXU_DTYPE)),
        grid=(ns, L // tb),
        in_specs=[
            pl.BlockSpec((tb, LANES), lambda s, i: (i, s)),
            pl.BlockSpec((1, LANES, 2 * S5_SW), lambda s, i: (s, 0, 0)),
            pl.BlockSpec((1, 2 * S5_SW, LANES), lambda s, i: (s, 0, 0)),
            pl.BlockSpec((1, tb // S5_SEG, 2 * S5_SW), lambda s, i: (s, 0, 0)),
            pl.BlockSpec((1, LANES), lambda s, i: (0, s)),
        ],
        out_specs=(pl.BlockSpec((tb, LANES), lambda s, i: (i, s)), pl.BlockSpec((tb, LANES), lambda s, i: (i, s))),
        scratch_shapes=[
            pltpu.VMEM((1, 2 * S5_SW), f32),
            pltpu.VMEM((tb, LANES), f32),
            pltpu.VMEM((tb, 2 * S5_SW), f32),
            pltpu.VMEM((tb, 2 * S5_SW), f32),
            pltpu.VMEM((tb, LANES), f32),
        ],
        compiler_params=_cparams(("parallel", "arbitrary")),
        name="s5_mix",
    )(z, w_in, w_out, apow, d_skip.reshape(1, S5_DIM))


RET_CHUNK = 512


def _ret_kernel(q_ref, k_ref, v_ref, g_ref, cos_ref, sin_ref, intra_ref, rowdec_ref, o_ref, s_s):
    c = pl.program_id(1)

    @pl.when(c == 0)
    def _():
        s_s[...] = jnp.zeros_like(s_s)

    half = RET_HEAD_DIM // 2
    cos, sin = cos_ref[...], sin_ref[...]

    def rot(x):
        x1, x2 = x[:, 0:half], x[:, half:2 * half]
        return jnp.concatenate([x1 * cos - x2 * sin, x1 * sin + x2 * cos], axis=1)

    q = rot(q_ref[...])
    k = rot(k_ref[...]) * (RET_HEAD_DIM ** -0.5)
    vb = _bf(v_ref[...])
    qb = _bf(q)
    dec = rowdec_ref[0]

    def lanes2(x):
        return jnp.concatenate([x, x], axis=1)

    xi, zeta, g_chunk = lanes2(dec[:, 0:LANES]), lanes2(dec[:, LANES:2 * LANES]), lanes2(dec[:, 2 * LANES:3 * LANES])
    att = _dot_nt(qb, _bf(k)) * intra_ref[0]
    s_old = s_s[...]
    o = _dot(_bf(att), vb) + _dot(qb, _bf(s_old)) * xi
    s_s[...] = s_old * g_chunk[0:1, :] + _dot(_bf((k * zeta).T), vb)
    mean = jnp.mean(o, axis=-1, keepdims=True)
    d = o - mean
    var = jnp.mean(d * d, axis=-1, keepdims=True)
    gate = g_ref[...]
    o_ref[...] = (gate * _sigmoid(gate) * (d * lax.rsqrt(var + RET_GN_EPS))).astype(o_ref.dtype)


def _ret_tables(L):
    C = RET_CHUNK
    inv = 1.0 / (RET_ROPE_BASE ** jnp.linspace(0.0, 1.0, RET_HEAD_DIM // 2, dtype=jnp.float32))
    ang = jnp.arange(L, dtype=jnp.float32)[:, None] * inv[None, :]
    log_g = jnp.log(1.0 - 2.0 ** (-5.0 - jnp.arange(RET_HEADS, dtype=jnp.float32)))
    pos = jnp.arange(C, dtype=jnp.float32)
    diff = pos[:, None] - pos[None, :]
    intra = jnp.where(diff >= 0, jnp.exp(jnp.maximum(diff, 0.0)[None] * log_g[:, None, None]), 0.0)
    xi = jnp.exp((pos + 1.0)[None, :] * log_g[:, None])
    zeta = jnp.exp((C - 1.0 - pos)[None, :] * log_g[:, None])
    g_chunk = jnp.broadcast_to(jnp.exp(C * log_g)[:, None], (RET_HEADS, C))
    rowdec = jnp.concatenate([jnp.broadcast_to(t[:, :, None], (RET_HEADS, C, LANES)) for t in (xi, zeta, g_chunk)], axis=2)
    return jnp.cos(ang), jnp.sin(ang), intra, rowdec


def _retention_mix(z, cos, sin, intra, rowdec):
    L = z.shape[0]
    C, hd = RET_CHUNK, RET_HEAD_DIM
    assert L % C == 0
    base = S5_DIM // hd

    def blk(j):
        return pl.BlockSpec((C, hd), lambda h, c: (c, base + j * RET_HEADS + h))

    return pl.pallas_call(
        _ret_kernel,
        out_shape=jax.ShapeDtypeStruct((L, RET_HEADS * hd), MXU_DTYPE),
        grid=(RET_HEADS, L // C),
        in_specs=[
            blk(0), blk(1), blk(2), blk(3),
            pl.BlockSpec((C, hd // 2), lambda h, c: (c, 0)),
            pl.BlockSpec((C, hd // 2), lambda h, c: (c, 0)),
            pl.BlockSpec((1, C, C), lambda h, c: (h, 0, 0)),
            pl.BlockSpec((1, C, 3 * LANES), lambda h, c: (h, 0, 0)),
        ],
        out_specs=pl.BlockSpec((C, hd), lambda h, c: (c, h)),
        scratch_shapes=[pltpu.VMEM((hd, hd), jnp.float32)],
        compiler_params=_cparams(("parallel", "arbitrary")),
        name="retention_mix",
    )(z, z, z, z, cos, sin, intra, rowdec)


def _lora_w_in(w):
    d = w.shape[0]
    o_ad = ZE_LORA + DECAY_RANK
    o_gd = o_ad + AAA_RANK
    parts = [
        w[:, ZE_LORA:o_ad], jnp.zeros((d, ZS_AD - ZS_WD - DECAY_RANK), w.dtype),
        w[:, o_ad:o_gd], jnp.zeros((d, ZS_GD - ZS_AD - AAA_RANK), w.dtype),
        w[:, o_gd:o_gd + GATE_RANK],
    ]
    out = jnp.concatenate(parts, axis=1)
    assert out.shape[1] == ZS_COLS
    return out


def _ffn(h, norm_g, w_gate, w_up, w_down_b, layer):
    hn = _rmsnorm(h, norm_g, MXU_DTYPE)
    mid = _matmul([hn], [(w_gate, layer, 0), (w_up, layer, 0)], [(0, 0), (0, 1)], [], _ep_swiglu, MXU_DTYPE,
                  "ffn_gate_up")
    return _matmul([mid], [(w_down_b, layer, 0)], [(0, 0)], [("tile", h)], _ep_resid, jnp.float32,
                   "ffn_down", tk=w_down_b.shape[1] // FFN_DOWN_K_SPLITS)


def _out_proj(h, o_first, o_second, w_out, layer):
    return _matmul([o_first, o_second], [(w_out, layer, 0), (w_out, layer, 1)], [(0, 0), (1, 1)], [("tile", h)],
                   _ep_resid2, jnp.float32, "out_proj")


def kernel(x, norm_mix, norm_ffn, ffn_gate, ffn_up, ffn_down, e_w_in, e_w_out, e_mu, e_w0, e_w_up, e_a0, e_a_up, e_g_up, e_k_k, e_k_a, e_r_k, e_ln_w, e_ln_b, o_w_in, o_w_out, o_lam_re, o_lam_im, o_log_step, o_b_re, o_b_im, o_c_re, o_c_im, o_d_skip, o_w_glu, o_b_glu, final_norm):
    assert x.shape[0] == 1
    h = x.reshape(x.shape[1], x.shape[2])
    L = h.shape[0]
    n_top = min(TOPK_MAX, L // 4)

    ffn_down_b = _bf(ffn_down)
    hn = _rmsnorm(h, norm_mix[0], MXU_DTYPE)
    z = _matmul([hn], [(e_w_in, 0, 0)], [(0, 0)], [], _ep_plain, jnp.float32, "even_in_proj", n_out=ZE_MAIN_COLS)
    z_lora = _matmul([hn], [(_lora_w_in(e_w_in[0]), 0, 0)], [(0, 0)], [], _ep_plain, jnp.float32, "even_lora_proj")
    tab_a = _rope_tables(L, A_ROT_HALF, A_HEAD_DIM, 2 * A_ROT_HALF)
    tab_i = _rope_tables(L, IDX_ROT_HALF, IDX_DIM, 2 * IDX_ROT_HALF)
    q_t, k_r, v_t, qi_t, ki_r, w_t = _dsa_prep(z, tab_a, tab_i)
    o_a = _dsa_attention(q_t, k_r, v_t, qi_t, ki_r, w_t, n_top)
    p_rows, mu_small, lora = _rwkv_params(e_mu[0], e_w0[0], e_w_up[0], e_a0[0], e_a_up[0], e_g_up[0], e_k_k[0],
                                          e_k_a[0], e_r_k[0], e_ln_w[0], e_ln_b[0])
    o_b = _rwkv_mix(z, z_lora, p_rows, mu_small, lora)
    h = _out_proj(h, o_a, o_b, e_w_out, 0)
    h = _ffn(h, norm_ffn[0], ffn_gate, ffn_up, ffn_down_b, 0)

    hn = _rmsnorm(h, norm_mix[1], MXU_DTYPE)
    z = _matmul([hn], [(o_w_in, 0, 0)], [(0, 0)], [], _ep_plain, jnp.float32, "odd_in_proj")
    s5_in, s5_out, s5_apow = _s5_params(o_lam_re[0], o_lam_im[0], o_log_step[0], o_b_re[0], o_b_im[0], o_c_re[0],
                                        o_c_im[0], S5_TB // S5_SEG)
    zg, zg_b = _s5_mix(z, s5_in, s5_out, s5_apow, o_d_skip[0])
    o_c = _matmul([zg_b], [(o_w_glu, 0, 0)], [(0, 0)], [("tile", zg), ("row", o_b_glu[0].reshape(1, S5_DIM))], _ep_glu,
                  MXU_DTYPE, "s5_glu")
    o_d = _retention_mix(z, *_ret_tables(L))
    h = _out_proj(h, o_c, o_d, o_w_out, 0)
    h = _ffn(h, norm_ffn[1], ffn_gate, ffn_up, ffn_down_b, 1)

    return _rmsnorm(h, final_norm, jnp.float32).reshape(x.shape)
```

```python
import functools
import math

import jax
import jax.numpy as jnp
from jax import lax
from jax.experimental import pallas as pl
from jax.experimental.pallas import tpu as pltpu

NORM_EPS = 1e-6
A_HEAD_DIM = 128
A_HEADS = 16
A_KV_HEADS = 4
A_GROUP = A_HEADS // A_KV_HEADS
A_ROT_HALF = 16
IDX_HEADS = 32
IDX_DIM = 64
IDX_ROT_HALF = 8
TOPK_MAX = 256
ROPE_THETA = 500000.0
RWKV_HEAD = 64
RWKV_DIM = 2048
DECAY_RANK = 96
AAA_RANK = 96
GATE_RANK = 256
RWKV_GN_EPS = 1e-5 * RWKV_HEAD
S5_DIM = 2048
S5_GROUP = 16
S5_GROUPS = 128
S5_STATE = 64
RET_HEAD_DIM = 256
RET_HEADS = 8
RET_ROPE_BASE = 10000.0
RET_GN_EPS = 1e-5

LANES = 128
SUBLANES = 8
VMEM_LIMIT_BYTES = 56 * 2**20

MXU_DTYPE = jnp.bfloat16
NEG_BIG = -1e30
INT_MIN = -2**31

ZE_Q, ZE_K, ZE_V, ZE_QI = 0, 2048, 2560, 3072
ZE_KI = 5120
ZE_WI_LANE = IDX_DIM
ZE_R, ZE_RK, ZE_RV = 5216, 7264, 9312
ZE_LORA = 11360
ZE_SKEW = ZE_R % LANES
ZE_MAIN_COLS = 11776
ZS_WD, ZS_AD, ZS_GD, ZS_COLS = 0, 128, 256, 512


def _cparams(sem):
    return pltpu.CompilerParams(dimension_semantics=sem, vmem_limit_bytes=VMEM_LIMIT_BYTES)


def _bf(x):
    return x.astype(MXU_DTYPE)


def _dot(a, b):
    return jnp.dot(a, b, preferred_element_type=jnp.float32)


def _dot_nt(a, b):
    return lax.dot_general(a, b, (((1,), (1,)), ((), ())), preferred_element_type=jnp.float32)


def _split3(x):
    h1 = _bf(x)
    r1 = x - h1.astype(jnp.float32)
    h2 = _bf(r1)
    h3 = _bf(r1 - h2.astype(jnp.float32))
    return h1, h2, h3


def _dot_x3(a, b):
    ah = _bf(a)
    al = _bf(a - ah.astype(jnp.float32))
    bh = _bf(b)
    bl = _bf(b - bh.astype(jnp.float32))
    return _dot(ah, bh) + _dot(ah, bl) + _dot(al, bh)


def _sigmoid(x):
    return 1.0 / (1.0 + jnp.exp(-x))


def _rmsnorm_kernel(x_ref, g_ref, o_ref):
    x = x_ref[...]
    ms = jnp.mean(x * x, axis=-1, keepdims=True)
    o_ref[...] = (x * lax.rsqrt(ms + NORM_EPS) * g_ref[...]).astype(o_ref.dtype)


def _rmsnorm(x, g, out_dtype, tm=512):
    m, d = x.shape
    return pl.pallas_call(
        _rmsnorm_kernel,
        out_shape=jax.ShapeDtypeStruct((m, d), out_dtype),
        grid=(m // tm,),
        in_specs=[pl.BlockSpec((tm, d), lambda i: (i, 0)), pl.BlockSpec((1, d), lambda i: (0, 0))],
        out_specs=pl.BlockSpec((tm, d), lambda i: (i, 0)),
        compiler_params=_cparams(("parallel",)),
        name="rmsnorm",
    )(x, g.reshape(1, d))


def _mm_kernel(*refs, n_a, n_w, n_e, pairs, epilogue, nk):
    a_refs = refs[:n_a]
    w_refs = refs[n_a:n_a + n_w]
    e_refs = refs[n_a + n_w:n_a + n_w + n_e]
    o_ref = refs[n_a + n_w + n_e]
    acc_refs = refs[n_a + n_w + n_e + 1:]
    if nk == 1:
        accs = [_dot(a_refs[i][...], _bf(w_refs[j][...])) for i, j in pairs]
        o_ref[...] = epilogue(accs, [e[...] for e in e_refs]).astype(o_ref.dtype)
        return
    k = pl.program_id(2)

    @pl.when(k == 0)
    def _():
        for acc in acc_refs:
            acc[...] = jnp.zeros_like(acc)

    for acc, (i, j) in zip(acc_refs, pairs):
        acc[...] += _dot(a_refs[i][...], _bf(w_refs[j][...]))

    @pl.when(k == nk - 1)
    def _():
        o_ref[...] = epilogue([acc[...] for acc in acc_refs], [e[...] for e in e_refs]).astype(o_ref.dtype)


def _weight_spec(w, layer, k_off, tk, tn):
    if w.ndim == 3:
        return pl.BlockSpec((None, tk, tn), lambda i, j, k: (layer, k + k_off, j))
    return pl.BlockSpec((tk, tn), lambda i, j, k: (k + k_off, j))


MM_TILES = {
    "even_in_proj": (1024, 512), "even_lora_proj": (1024, 512), "odd_in_proj": (1024, 512), "s5_glu": (1024, 512),
    "out_proj": (1024, 512), "ffn_gate_up": (2048, 256), "ffn_down": (512, 512),
}
FFN_DOWN_K_SPLITS = 1


def _matmul(a_list, w_list, pairs, extras, epilogue, out_dtype, name, tk=None, n_out=None):
    m, kdim = a_list[0].shape
    n = w_list[0][0].shape[-1] if n_out is None else n_out
    tk = kdim if tk is None else tk
    tm, tn = MM_TILES[name]
    tm = min(tm, m)
    nk = kdim // tk
    assert m % tm == 0 and n % tn == 0 and kdim % tk == 0
    in_specs = [pl.BlockSpec((tm, tk), lambda i, j, k: (i, k)) for _ in a_list]
    in_specs += [_weight_spec(w, layer, k_off, tk, tn) for w, layer, k_off in w_list]
    w_list = [w for w, _, _ in w_list]
    e_arrays = []
    for kind, arr in extras:
        e_arrays.append(arr)
        if kind == "tile":
            in_specs.append(pl.BlockSpec((tm, tn), lambda i, j, k: (i, j)))
        else:
            in_specs.append(pl.BlockSpec((1, tn), lambda i, j, k: (0, j)))
    scratch = [] if nk == 1 else [pltpu.VMEM((tm, tn), jnp.float32) for _ in pairs]
    kern = functools.partial(_mm_kernel, n_a=len(a_list), n_w=len(w_list), n_e=len(extras),
                             pairs=tuple(pairs), epilogue=epilogue, nk=nk)
    return pl.pallas_call(
        kern,
        out_shape=jax.ShapeDtypeStruct((m, n), out_dtype),
        grid=(m // tm, n // tn, nk),
        in_specs=in_specs,
        out_specs=pl.BlockSpec((tm, tn), lambda i, j, k: (i, j)),
        scratch_shapes=scratch,
        compiler_params=_cparams(("parallel", "parallel", "arbitrary")),
        name=name,
    )(*a_list, *w_list, *e_arrays)


def _ep_plain(accs, ex):
    return accs[0]


def _ep_swiglu(accs, ex):
    g = accs[0]
    return g * _sigmoid(g) * accs[1]


def _ep_resid(accs, ex):
    return ex[0] + accs[0]


def _ep_resid2(accs, ex):
    return ex[0] + (accs[0] + accs[1])


def _ep_glu(accs, ex):
    zg, b = ex
    return zg * _sigmoid(accs[0] + b)


def _rope_tile(x, tc, s1, s2, n_half):
    return x * tc + pltpu.roll(x, n_half, 1) * s2 + pltpu.roll(x, LANES - n_half, 1) * s1


DSA_VPAD = 16
DSA_VROWS = A_HEAD_DIM + DSA_VPAD


def _dsa_prep_kernel(zq_ref, zs_ref, ta_ref, ti_ref, qt_ref, k_ref, vt_ref, qit_ref, ki_ref, wt_ref):
    ta_c, ta_1, ta_2 = ta_ref[0], ta_ref[1], ta_ref[2]
    ti_c, ti_1, ti_2 = ti_ref[0], ti_ref[1], ti_ref[2]
    q_scale = A_HEAD_DIM ** -0.5 * math.log2(math.e)
    for h in range(A_HEADS):
        x = zq_ref[:, ZE_Q + h * LANES:ZE_Q + (h + 1) * LANES]
        y = _rope_tile(x, ta_c, ta_1, ta_2, A_ROT_HALF) * q_scale
        qt_ref[h * LANES:(h + 1) * LANES, :] = _bf(y.T)
    for g in range(A_KV_HEADS):
        x = zq_ref[:, ZE_K + g * LANES:ZE_K + (g + 1) * LANES]
        k_ref[:, g * LANES:(g + 1) * LANES] = _bf(_rope_tile(x, ta_c, ta_1, ta_2, A_ROT_HALF))
        v = zq_ref[:, ZE_V + g * LANES:ZE_V + (g + 1) * LANES]
        vt_ref[g * DSA_VROWS:g * DSA_VROWS + A_HEAD_DIM, :] = _bf(v.T)
        vt_ref[g * DSA_VROWS + A_HEAD_DIM:(g + 1) * DSA_VROWS, :] = jnp.ones((DSA_VPAD, v.shape[0]), MXU_DTYPE)
    for c in range(IDX_HEADS * IDX_DIM // LANES):
        x = zq_ref[:, ZE_QI + c * LANES:ZE_QI + (c + 1) * LANES]
        y = _rope_tile(x, ti_c, ti_1, ti_2, IDX_ROT_HALF)
        qit_ref[c * LANES:(c + 1) * LANES, :] = _bf(y.T)
    ki_ref[...] = _bf(_rope_tile(zs_ref[...], ti_c, ti_1, ti_2, IDX_ROT_HALF))
    w_scale = (IDX_DIM ** -0.5) * (IDX_HEADS ** -0.5)
    wt_ref[...] = (zs_ref[...] * w_scale).T


def _dsa_prep(z, tab_a, tab_i, tm=512):
    L = z.shape[0]
    n_q = A_HEADS * A_HEAD_DIM
    n_kv = A_KV_HEADS * A_HEAD_DIM
    n_qi = IDX_HEADS * IDX_DIM
    out_shape = (
        jax.ShapeDtypeStruct((n_q, L), MXU_DTYPE),
        jax.ShapeDtypeStruct((L, n_kv), MXU_DTYPE),
        jax.ShapeDtypeStruct((A_KV_HEADS * DSA_VROWS, L), MXU_DTYPE),
        jax.ShapeDtypeStruct((n_qi, L), MXU_DTYPE),
        jax.ShapeDtypeStruct((L, LANES), MXU_DTYPE),
        jax.ShapeDtypeStruct((LANES, L), jnp.float32),
    )
    return pl.pallas_call(
        _dsa_prep_kernel,
        out_shape=out_shape,
        grid=(L // tm,),
        in_specs=[
            pl.BlockSpec((tm, ZE_KI), lambda i: (i, 0)),
            pl.BlockSpec((tm, LANES), lambda i: (i, ZE_KI // LANES)),
            pl.BlockSpec((3, tm, LANES), lambda i: (0, i, 0)),
            pl.BlockSpec((3, tm, LANES), lambda i: (0, i, 0)),
        ],
        out_specs=(
            pl.BlockSpec((n_q, tm), lambda i: (0, i)),
            pl.BlockSpec((tm, n_kv), lambda i: (i, 0)),
            pl.BlockSpec((A_KV_HEADS * DSA_VROWS, tm), lambda i: (0, i)),
            pl.BlockSpec((n_qi, tm), lambda i: (0, i)),
            pl.BlockSpec((tm, LANES), lambda i: (i, 0)),
            pl.BlockSpec((LANES, tm), lambda i: (0, i)),
        ),
        compiler_params=_cparams(("parallel",)),
        name="dsa_prep",
    )(z, z, tab_a, tab_i)


DSA_TQ = 128
DSA_KC = 512
DSA_KA = 256
DSA_GROUP_SETS = ((0, 1, 2, 3),)


def _dsa_kernel(qt_ref, qit_ref, wt_ref, ki_ref, k_ref, vt_ref, o_ref,
                qi_s, qg_s, sc_s, key_s, m_s, acc_s, *, n_top):
    qb = pl.program_id(0)
    q0 = qb * DSA_TQ
    n_chunks = (q0 + DSA_TQ + DSA_KC - 1) // DSA_KC
    q_pos = q0 + lax.broadcasted_iota(jnp.int32, (1, DSA_TQ), 1)

    for h in range(IDX_HEADS):
        qi_s[:, h * DSA_TQ:(h + 1) * DSA_TQ] = qit_ref[h * IDX_DIM:(h + 1) * IDX_DIM, :]
    for g in range(A_KV_HEADS):
        for j in range(A_GROUP):
            h = g * A_GROUP + j
            qg_s[g, :, j * DSA_TQ:(j + 1) * DSA_TQ] = qt_ref[h * A_HEAD_DIM:(h + 1) * A_HEAD_DIM, :]

    def score_chunk(c, carry):
        for sub in range(DSA_KC // DSA_KA):
            r0 = pl.multiple_of(c * DSA_KC + sub * DSA_KA, DSA_KA)
            kic = ki_ref[pl.ds(r0, DSA_KA), 0:IDX_DIM]
            logits = _dot(kic, qi_s[...])
            acc = jnp.zeros((DSA_KA, DSA_TQ), jnp.float32)
            for h in range(IDX_HEADS):
                w_h = wt_ref[ZE_WI_LANE + h:ZE_WI_LANE + h + 1, :]
                acc = acc + jnp.maximum(logits[:, h * DSA_TQ:(h + 1) * DSA_TQ], 0.0) * w_h
            bits = lax.bitcast_convert_type(acc, jnp.int32)
            okey = bits ^ ((bits >> 31) & jnp.int32(0x7FFFFFFF))
            k_pos = r0 + lax.broadcasted_iota(jnp.int32, (DSA_KA, 1), 0)
            key_s[pl.ds(r0, DSA_KA), :] = jnp.where(k_pos <= q_pos, okey, jnp.int32(INT_MIN))
        return carry

    lax.fori_loop(0, n_chunks, score_chunk, 0)

    def count_keys(cand, strict):
        def body(c, cnt):
            r0 = pl.multiple_of(c * DSA_KC, DSA_KC)
            blk = key_s[pl.ds(r0, DSA_KC), :]
            hit = jnp.where((blk > cand) if strict else (blk >= cand), jnp.int32(1), jnp.int32(0))
            return cnt + jnp.sum(hit.reshape(DSA_KC // SUBLANES, SUBLANES, DSA_TQ), axis=0)
        cnt8 = lax.fori_loop(0, n_chunks, body, jnp.zeros((SUBLANES, DSA_TQ), jnp.int32))
        return jnp.sum(cnt8, axis=0, keepdims=True)

    def bisect(it, u):
        cand_u = u | lax.shift_left(jnp.int32(1), 31 - it)
        cnt = count_keys(cand_u ^ jnp.int32(INT_MIN), False)
        return jnp.where(cnt >= n_top, cand_u, u)

    thr = lax.fori_loop(0, 32, bisect, jnp.zeros((1, DSA_TQ), jnp.int32)) ^ jnp.int32(INT_MIN)
    thr = jnp.maximum(thr, jnp.int32(INT_MIN + 1))
    n_sel = count_keys(thr, False)

    def bias_plain():
        def body(c, carry):
            r0 = pl.multiple_of(c * DSA_KC, DSA_KC)
            sc_s[pl.ds(r0, DSA_KC), :] = jnp.where(key_s[pl.ds(r0, DSA_KC), :] >= thr, 0.0, NEG_BIG)
            return carry
        lax.fori_loop(0, n_chunks, body, 0)

    def bias_ties():
        need = (n_top - count_keys(thr, True)).astype(jnp.float32)
        rr = lax.broadcasted_iota(jnp.int32, (DSA_KC, DSA_KC), 0)
        cc = lax.broadcasted_iota(jnp.int32, (DSA_KC, DSA_KC), 1)
        before = jnp.where(cc < rr, 1.0, 0.0).astype(MXU_DTYPE)

        def body(c, seen):
            r0 = pl.multiple_of(c * DSA_KC, DSA_KC)
            blk = key_s[pl.ds(r0, DSA_KC), :]
            tie = jnp.where(blk == thr, 1.0, 0.0)
            rank = _dot(before, _bf(tie)) + seen
            sel = jnp.logical_or(blk > thr, jnp.logical_and(blk == thr, rank < need))
            sc_s[pl.ds(r0, DSA_KC), :] = jnp.where(sel, 0.0, NEG_BIG)
            return seen + jnp.sum(tie, axis=0, keepdims=True)
        lax.fori_loop(0, n_chunks, body, jnp.zeros((1, DSA_TQ), jnp.float32))

    lax.cond(jnp.max(n_sel) > n_top, bias_ties, bias_plain)

    m_s[...] = jnp.full(m_s.shape, NEG_BIG, jnp.float32)
    acc_s[...] = jnp.zeros(acc_s.shape, jnp.float32)

    def attn_chunk(c, carry, groups):
        r0 = pl.multiple_of(c * DSA_KC, DSA_KC)
        bias = sc_s[pl.ds(r0, DSA_KC), :]
        bias_g = jnp.concatenate([bias] * A_GROUP, axis=1)
        s = {g: _dot(k_ref[pl.ds(r0, DSA_KC), g * A_HEAD_DIM:(g + 1) * A_HEAD_DIM], qg_s[g]) + bias_g for g in groups}
        m_old = {g: m_s[g] for g in groups}
        m_new = {g: jnp.maximum(m_old[g], jnp.max(s[g], axis=0, keepdims=True)) for g in groups}
        alpha = {g: jnp.exp2(m_old[g] - m_new[g]) for g in groups}
        p = {g: _bf(jnp.exp2(s[g] - m_new[g])) for g in groups}
        pv = {g: _dot(vt_ref[g * DSA_VROWS:(g + 1) * DSA_VROWS, pl.ds(r0, DSA_KC)], p[g]) for g in groups}
        acc = {g: alpha[g] * acc_s[g] + pv[g] for g in groups}
        for g in groups:
            acc_s[g] = acc[g]
            m_s[g] = m_new[g]
        return carry

    for groups in DSA_GROUP_SETS:
        lax.fori_loop(0, n_chunks, functools.partial(attn_chunk, groups=groups), 0)
    for g in range(A_KV_HEADS):
        acc = acc_s[g]
        o_t = acc[0:A_HEAD_DIM, :] / acc[A_HEAD_DIM:A_HEAD_DIM + 1, :]
        for j in range(A_GROUP):
            h = g * A_GROUP + j
            o_ref[:, h * A_HEAD_DIM:(h + 1) * A_HEAD_DIM] = _bf(o_t[:, j * DSA_TQ:(j + 1) * DSA_TQ].T)


def _dsa_attention(q_t, k_r, v_t, qi_t, ki_r, w_t, n_top):
    L = k_r.shape[0]
    n_q = A_HEADS * A_HEAD_DIM
    n_kv = A_KV_HEADS * A_HEAD_DIM
    n_qi = IDX_HEADS * IDX_DIM
    assert L % DSA_KC == 0
    resident = dict(pipeline_mode=pl.Buffered(1))
    return pl.pallas_call(
        functools.partial(_dsa_kernel, n_top=n_top),
        out_shape=jax.ShapeDtypeStruct((L, n_q), MXU_DTYPE),
        grid=(L // DSA_TQ,),
        in_specs=[
            pl.BlockSpec((n_q, DSA_TQ), lambda i: (0, i)),
            pl.BlockSpec((n_qi, DSA_TQ), lambda i: (0, i)),
            pl.BlockSpec((LANES, DSA_TQ), lambda i: (0, i)),
            pl.BlockSpec((L, LANES), lambda i: (0, 0), **resident),
            pl.BlockSpec((L, n_kv), lambda i: (0, 0), **resident),
            pl.BlockSpec((A_KV_HEADS * DSA_VROWS, L), lambda i: (0, 0), **resident),
        ],
        out_specs=pl.BlockSpec((DSA_TQ, n_q), lambda i: (i, 0)),
        scratch_shapes=[
            pltpu.VMEM((IDX_DIM, IDX_HEADS * DSA_TQ), MXU_DTYPE),
            pltpu.VMEM((A_KV_HEADS, A_HEAD_DIM, A_GROUP * DSA_TQ), MXU_DTYPE),
            pltpu.VMEM((L, DSA_TQ), jnp.float32),
            pltpu.VMEM((L, DSA_TQ), jnp.int32),
            pltpu.VMEM((A_KV_HEADS, 1, A_GROUP * DSA_TQ), jnp.float32),
            pltpu.VMEM((A_KV_HEADS, DSA_VROWS, A_GROUP * DSA_TQ), jnp.float32),
        ],
        compiler_params=_cparams(("arbitrary",)),
        name="dsa_attention",
    )(q_t, qi_t, w_t, ki_r, k_r, v_t)


def _rope_tables(L, n_half, period, theta_pow_dim):
    inv = ROPE_THETA ** (-jnp.arange(0, theta_pow_dim, 2, dtype=jnp.float32) / theta_pow_dim)
    ang = jnp.arange(L, dtype=jnp.float32)[:, None] * inv[None, :]
    cos, sin = jnp.cos(ang), jnp.sin(ang)
    pad = period - 2 * n_half
    tc = jnp.concatenate([cos, cos, jnp.ones((L, pad), jnp.float32)], axis=1)
    s1 = jnp.concatenate([-sin, jnp.zeros((L, period - n_half), jnp.float32)], axis=1)
    s2 = jnp.concatenate([jnp.zeros((L, n_half), jnp.float32), sin, jnp.zeros((L, pad), jnp.float32)], axis=1)
    reps = LANES // period
    return jnp.stack([jnp.tile(t, (1, reps)) for t in (tc, s1, s2)])


RWKV_CHUNK = 64
RWKV_SLAB = 1024
RWKV_TB = 256
P_MU_R, P_MU_K, P_MU_V, P_W0, P_A0, P_KK, P_KA, P_RK, P_LNW, P_LNB = range(10)


def _head_ones():
    r = lax.broadcasted_iota(jnp.int32, (LANES, LANES), 0) // RWKV_HEAD
    c = lax.broadcasted_iota(jnp.int32, (LANES, LANES), 1) // RWKV_HEAD
    return jnp.where(r == c, 1.0, 0.0).astype(MXU_DTYPE)


def _head_sum(x, ones_bd):
    hi = _bf(x)
    lo = _bf(x - hi.astype(jnp.float32))
    tiles = [_dot(hi[:, t * LANES:(t + 1) * LANES], ones_bd) + _dot(lo[:, t * LANES:(t + 1) * LANES], ones_bd)
             for t in range(x.shape[1] // LANES)]
    return jnp.concatenate(tiles, axis=1)


def _pair_stack(x):
    lane = lax.broadcasted_iota(jnp.int32, x.shape, 1)
    return jnp.concatenate([jnp.where(lane < RWKV_HEAD, x, 0.0), jnp.where(lane >= RWKV_HEAD, x, 0.0)], axis=0)


def _rwkv_kernel(zr_ref, zrx_ref, zk_ref, zkx_ref, zv_ref, zvx_ref, zs_ref,
                 hr_ref, hrx_ref, hk_ref, hkx_ref, hv_ref, hvx_ref, hs_ref, p_ref, mus_ref, lora_ref,
                 o_ref, h_s, r_s, lw_s, cum_s, k2_s, v_s, kk_s, a_s, g_s, bon_s, oo_s):
    C = RWKV_CHUNK
    tb = zr_ref.shape[0]
    i = pl.program_id(1)

    @pl.when(i == 0)
    def _():
        h_s[...] = jnp.zeros_like(h_s)

    row = lax.broadcasted_iota(jnp.int32, (tb, 1), 0)
    has_prev = jnp.where(i > 0, 1.0, 0.0)

    def unskew(main_ref, next_ref):
        x = jnp.concatenate([main_ref[...], next_ref[...]], axis=1)
        return pltpu.roll(x, x.shape[1] - ZE_SKEW, 1)[:, 0:main_ref.shape[1]]

    def mix(z, halo, mu):
        prev = halo[SUBLANES - 1:SUBLANES, :] * has_prev
        shifted = jnp.where(row == 0, prev, pltpu.roll(z, 1, 0))
        return z + (shifted - z) * mu

    r = mix(unskew(zr_ref, zrx_ref), unskew(hr_ref, hrx_ref), p_ref[P_MU_R:P_MU_R + 1, :])
    k = mix(unskew(zk_ref, zkx_ref), unskew(hk_ref, hkx_ref), p_ref[P_MU_K:P_MU_K + 1, :])
    v = mix(unskew(zv_ref, zvx_ref), unskew(hv_ref, hvx_ref), p_ref[P_MU_V:P_MU_V + 1, :])
    sm = mix(zs_ref[...], hs_ref[...], mus_ref[...])
    wd, ad, gd = sm[:, 0:LANES], sm[:, LANES:2 * LANES], sm[:, 2 * LANES:4 * LANES]
    x = -(p_ref[P_W0:P_W0 + 1, :] + _dot_x3(jnp.tanh(wd), lora_ref[0:LANES, :]))
    softplus = jnp.maximum(x, 0.0) + jnp.log(1.0 + jnp.exp(-jnp.abs(x)))
    lw = -jnp.exp(-softplus - 0.5)
    a = _sigmoid(p_ref[P_A0:P_A0 + 1, :] + _dot(_bf(ad), _bf(lora_ref[LANES:2 * LANES, :])))
    g_s[...] = _dot(_bf(_sigmoid(gd)), _bf(lora_ref[2 * LANES:4 * LANES, :]))
    ones_bd = _head_ones()
    kk = k * p_ref[P_KK:P_KK + 1, :]
    kk = kk * lax.rsqrt(jnp.maximum(_head_sum(kk * kk, ones_bd), 1e-24))
    k2 = k * (1.0 + (a - 1.0) * p_ref[P_KA:P_KA + 1, :])
    bon_s[...] = _head_sum(r * k2 * p_ref[P_RK:P_RK + 1, :], ones_bd)
    tr = lax.broadcasted_iota(jnp.int32, (tb, tb), 0)
    tc = lax.broadcasted_iota(jnp.int32, (tb, tb), 1)
    tril_bd = jnp.where(jnp.logical_and(tc <= tr, tc // C == tr // C), 1.0, 0.0).astype(MXU_DTYPE)
    l1, l2, l3 = _split3(lw)
    cum_s[...] = _dot(tril_bd, l1) + _dot(tril_bd, l2) + _dot(tril_bd, l3)
    r_s[...] = r
    lw_s[...] = lw
    k2_s[...] = k2
    v_s[...] = v
    kk_s[...] = kk
    a_s[...] = a

    n2 = 2 * C
    rr = lax.broadcasted_iota(jnp.int32, (n2, n2), 0)
    cc = lax.broadcasted_iota(jnp.int32, (n2, n2), 1)
    strict = (cc % C) < (rr % C)
    incl = (cc % C) <= (rr % C)
    eye = jnp.where(rr == cc, 1.0, 0.0)

    n_pairs = RWKV_SLAB // LANES
    pairs = range(n_pairs)

    def chunk(c, carry):
        r0 = pl.multiple_of(c * C, C)

        def ld(ref):
            return [ref[pl.ds(r0, C), p * LANES:(p + 1) * LANES] for p in pairs]

        rc, lwc, kc, vc, kkc, ac, cum = ld(r_s), ld(lw_s), ld(k2_s), ld(v_s), ld(kk_s), ld(a_s), ld(cum_s)
        h_old = [h_s[p] for p in pairs]
        cl = [cum[p][C - 1:C, :] for p in pairs]
        e_neg = [jnp.exp(-cum[p]) for p in pairs]
        e_end = [jnp.exp(cl[p] - cum[p]) for p in pairs]
        beta = [kkc[p] * ac[p] for p in pairs]
        ae_s = [_pair_stack(-kkc[p] * jnp.exp(cum[p] - lwc[p])) for p in pairs]
        rp_s = [_pair_stack(rc[p] * jnp.exp(cum[p])) for p in pairs]
        bm_s = [_pair_stack(beta[p] * e_neg[p]) for p in pairs]
        km_s = [_pair_stack(kc[p] * e_neg[p]) for p in pairs]
        bt_t = [_bf(_pair_stack(beta[p] * e_end[p]).T) for p in pairs]
        kt_t = [_bf(_pair_stack(kc[p] * e_end[p]).T) for p in pairs]
        v_st = [_bf(_pair_stack(vc[p])) for p in pairs]
        aa = [_dot_nt(_bf(jnp.concatenate([ae_s[p], rp_s[p]], axis=0)),
                      _bf(jnp.concatenate([bm_s[p], km_s[p]], axis=0))) for p in pairs]
        a_ab = [jnp.where(strict, aa[p][0:n2, 0:n2], 0.0) for p in pairs]
        a_ak = [_bf(jnp.where(strict, aa[p][0:n2, n2:2 * n2], 0.0)) for p in pairs]
        a_rb = [_bf(jnp.where(incl, aa[p][n2:2 * n2, 0:n2], 0.0)) for p in pairs]
        a_rk = [_bf(jnp.where(incl, aa[p][n2:2 * n2, n2:2 * n2], 0.0)) for p in pairs]
        t_inv = [eye + a_ab[p] for p in pairs]
        n_b = [_bf(a_ab[p]) for p in pairs]
        n_pow = [_dot(n_b[p], n_b[p]) for p in pairs]
        for _ in range(int(math.log2(C)) - 2):
            n_b = [_bf(n_pow[p]) for p in pairs]
            both = [_dot(jnp.concatenate([_bf(t_inv[p]), n_b[p]], axis=0), n_b[p]) for p in pairs]
            t_inv = [t_inv[p] + both[p][0:n2, :] for p in pairs]
            n_pow = [both[p][n2:2 * n2, :] for p in pairs]
        t_inv = [t_inv[p] + _dot(_bf(t_inv[p]), _bf(n_pow[p])) for p in pairs]
        t_b = [_bf(t_inv[p]) for p in pairs]
        w_k = [_bf(_dot(a_ak[p], v_st[p])) for p in pairs]
        t_rhs = [_dot(t_b[p], jnp.concatenate([_bf(ae_s[p]), w_k[p]], axis=1)) for p in pairs]
        a_til = [_bf(t_rhs[p][:, 0:LANES]) for p in pairs]
        v_til = [t_rhs[p][:, LANES:2 * LANES] for p in pairs]
        o_intra = [_dot(a_rk[p], v_st[p]) for p in pairs]
        h_kv = [_dot(kt_t[p], v_st[p]) for p in pairs]
        decay_col = [jnp.exp(jnp.broadcast_to(cl[p], (LANES, LANES)).T) for p in pairs]
        h_b = [_bf(h_old[p]) for p in pairs]
        u_b = [_bf(_dot(a_til[p], h_b[p]) + v_til[p]) for p in pairs]
        o_st = [_dot(_bf(rp_s[p]), h_b[p]) + _dot(a_rb[p], u_b[p]) + o_intra[p] for p in pairs]
        h_new = [decay_col[p] * h_old[p] + _dot(bt_t[p], u_b[p]) + h_kv[p] for p in pairs]
        for p in pairs:
            h_s[p] = h_new[p]
            oo_s[pl.ds(r0, C), p * LANES:(p + 1) * LANES] = o_st[p][0:C, :] + o_st[p][C:n2, :]
        return carry

    lax.fori_loop(0, tb // C, chunk, 0)

    o = oo_s[...]
    mean = _head_sum(o, ones_bd) * (1.0 / RWKV_HEAD)
    d = o - mean
    var = _head_sum(d * d, ones_bd) * (1.0 / RWKV_HEAD)
    y = d * lax.rsqrt(var + RWKV_GN_EPS) * p_ref[P_LNW:P_LNW + 1, :] + p_ref[P_LNB:P_LNB + 1, :]
    y = y + bon_s[...] * v_s[...]
    o_ref[...] = (y * g_s[...]).astype(o_ref.dtype)


def _rwkv_params(mu, w0, w_up, a0, a_up, g_up, k_k, k_a, r_k, ln_w, ln_b):
    d = RWKV_DIM
    mu_r, mu_k, mu_v = mu[0:d], mu[d:2 * d], mu[2 * d:3 * d]
    o = 3 * d
    mu_wd, mu_ad, mu_gd = mu[o:o + DECAY_RANK], mu[o + DECAY_RANK:o + DECAY_RANK + AAA_RANK], mu[o + DECAY_RANK + AAA_RANK:]
    rows = [mu_r, mu_k, mu_v, w0, a0, k_k, k_a, r_k.reshape(d), ln_w, ln_b]
    p_rows = jnp.concatenate([jnp.stack(rows), jnp.zeros((16 - len(rows), d), jnp.float32)], axis=0)

    def pad_to(x, n, axis):
        widths = [(0, 0)] * x.ndim
        widths[axis] = (0, n - x.shape[axis])
        return jnp.pad(x, widths)

    mu_small = jnp.concatenate([pad_to(mu_wd, LANES, 0), pad_to(mu_ad, LANES, 0), mu_gd]).reshape(1, 4 * LANES)
    lora = jnp.concatenate([pad_to(w_up, LANES, 0), pad_to(a_up, LANES, 0), g_up], axis=0)
    return p_rows, mu_small, lora


def _rwkv_mix(z, z_lora, p_rows, mu_small, lora):
    L = z.shape[0]
    tb, w = RWKV_TB, RWKV_SLAB
    assert L % tb == 0
    hb = tb // SUBLANES
    tiles = w // LANES

    def prev_rows(i):
        return jnp.maximum(i * hb - 1, 0)

    def windows(col0, rows, row_map):
        base = col0 - ZE_SKEW
        assert base % w == 0
        return [pl.BlockSpec((rows, w), lambda s, i: (row_map(i), base // w + s)),
                pl.BlockSpec((rows, LANES), lambda s, i: (row_map(i), base // LANES + (s + 1) * tiles))]

    body = [spec for c in (ZE_R, ZE_RK, ZE_RV) for spec in windows(c, tb, lambda i: i)]
    halo = [spec for c in (ZE_R, ZE_RK, ZE_RV) for spec in windows(c, SUBLANES, prev_rows)]
    f32 = jnp.float32
    return pl.pallas_call(
        _rwkv_kernel,
        out_shape=jax.ShapeDtypeStruct((L, RWKV_DIM), MXU_DTYPE),
        grid=(RWKV_DIM // w, L // tb),
        in_specs=body + [pl.BlockSpec((tb, ZS_COLS), lambda s, i: (i, 0))]
        + halo + [pl.BlockSpec((SUBLANES, ZS_COLS), lambda s, i: (prev_rows(i), 0))]
        + [
            pl.BlockSpec((16, w), lambda s, i: (0, s)),
            pl.BlockSpec((1, ZS_COLS), lambda s, i: (0, 0)),
            pl.BlockSpec((ZS_COLS, w), lambda s, i: (0, s)),
        ],
        out_specs=pl.BlockSpec((tb, w), lambda s, i: (i, s)),
        scratch_shapes=[pltpu.VMEM((w // LANES, LANES, LANES), f32)] + [pltpu.VMEM((tb, w), f32) for _ in range(10)],
        compiler_params=_cparams(("parallel", "arbitrary")),
        name="rwkv7_mix",
    )(*([z] * 6), z_lora, *([z] * 6), z_lora, p_rows, mu_small, lora)


S5_TB = 1024
S5_SEG = SUBLANES
S5_SLAB_GROUPS = LANES // S5_GROUP
S5_SW = S5_SLAB_GROUPS * S5_STATE
S5_NSUB = 2


def _cmul_add(ar, ai, xr, xi, br, bi):
    return ar * xr - ai * xi + br, ar * xi + ai * xr + bi


def _s5_kernel(*refs):
    u_refs = refs[:S5_NSUB]
    w_ref, c_ref, apow_ref, d_ref, zg_ref, zgb_ref, st_s, up_s, bu_s, x_s, y_s = refs[S5_NSUB:]
    tb = u_refs[0].shape[0]
    ts = tb // S5_SEG
    sw = S5_SW
    subs = range(S5_NSUB)
    i = pl.program_id(1)

    @pl.when(i == 0)
    def _():
        st_s[...] = jnp.zeros_like(st_s)

    for tau in range(ts):
        for n in subs:
            up_s[n, tau * S5_SEG:(tau + 1) * S5_SEG, :] = u_refs[n][pl.ds(tau, S5_SEG, stride=ts), :]
    for n in subs:
        bu_s[n] = _dot(_bf(up_s[n]), _bf(w_ref[n]))
    a1 = [apow_ref[n, 0:1, :] for n in subs]
    ar = [jnp.broadcast_to(a1[n][:, 0:sw], (S5_SEG, sw)) for n in subs]
    ai = [jnp.broadcast_to(a1[n][:, sw:2 * sw], (S5_SEG, sw)) for n in subs]

    def scan(tau, x):
        r0 = pl.multiple_of(tau * S5_SEG, S5_SEG)
        out = []
        for n in subs:
            b = bu_s[n, pl.ds(r0, S5_SEG), :]
            nr, ni = _cmul_add(ar[n], ai[n], x[n][0], x[n][1], b[:, 0:sw], b[:, sw:2 * sw])
            x_s[n, pl.ds(r0, S5_SEG), :] = jnp.concatenate([nr, ni], axis=1)
            out.append((nr, ni))
        return tuple(out)

    zero = jnp.zeros((S5_SEG, sw), jnp.float32)
    x_end = lax.fori_loop(0, ts, scan, tuple((zero, zero) for _ in subs), unroll=4)

    er, ei = [], []
    for n in subs:
        a_ts = apow_ref[n, ts - 1:ts, :]
        tr, ti = a_ts[:, 0:sw], a_ts[:, sw:2 * sw]
        cr, ci = st_s[n, :, 0:sw], st_s[n, :, sw:2 * sw]
        ent_r, ent_i = [], []
        for s in range(S5_SEG):
            ent_r.append(cr)
            ent_i.append(ci)
            cr, ci = _cmul_add(tr, ti, cr, ci, x_end[n][0][s:s + 1, :], x_end[n][1][s:s + 1, :])
        st_s[n] = jnp.concatenate([cr, ci], axis=1)
        er.append(jnp.concatenate(ent_r, axis=0))
        ei.append(jnp.concatenate(ent_i, axis=0))

    def fix(tau, carry):
        r0 = pl.multiple_of(tau * S5_SEG, S5_SEG)
        for n in subs:
            ap = apow_ref[n, pl.ds(tau, 1), :]
            x = x_s[n, pl.ds(r0, S5_SEG), :]
            nr, ni = _cmul_add(ap[:, 0:sw], ap[:, sw:2 * sw], er[n], ei[n], x[:, 0:sw], x[:, sw:2 * sw])
            x_s[n, pl.ds(r0, S5_SEG), :] = jnp.concatenate([nr, ni], axis=1)
        return carry

    lax.fori_loop(0, ts, fix, 0, unroll=4)
    half = tb // 2
    yp = []
    for n in subs:
        c_out = _bf(c_ref[n])
        yp.append(jnp.concatenate([_dot(_bf(x_s[n, 0:half, :]), c_out), _dot(_bf(x_s[n, half:tb, :]), c_out)], axis=0))
    for tau in range(ts):
        for n in subs:
            y_s[n, pl.ds(tau, S5_SEG, stride=ts), :] = yp[n][tau * S5_SEG:(tau + 1) * S5_SEG, :]
    y = jnp.concatenate([y_s[n] + d_ref[:, n * LANES:(n + 1) * LANES] * u_refs[n][...] for n in subs], axis=1)
    zg = 0.5 * y * (1.0 + jnp.tanh(math.sqrt(2.0 / math.pi) * (y + 0.044715 * (y * y * y))))
    zg_ref[...] = zg
    zgb_ref[...] = zg.astype(zgb_ref.dtype)


def _s5_params(lam_re, lam_im, log_step, b_re, b_im, c_re, c_im, n_pow):
    lr = jnp.minimum(lam_re, -1e-4)
    li = lam_im
    step = jnp.exp(log_step)[:, None]
    mag = jnp.exp(lr * step)
    abar_r = mag * jnp.cos(li * step)
    abar_i = mag * jnp.sin(li * step)
    den = lr * lr + li * li
    cr = (lr * (abar_r - 1.0) + li * abar_i) / den
    ci = (lr * abar_i - li * (abar_r - 1.0)) / den
    bbar_r = cr[..., None] * b_re - ci[..., None] * b_im
    bbar_i = cr[..., None] * b_im + ci[..., None] * b_re
    ns = S5_GROUPS // S5_SLAB_GROUPS
    eye = jnp.eye(S5_SLAB_GROUPS, dtype=jnp.float32)

    def in_mat(bb):
        t = jnp.einsum('ab,sapi->saibp', eye, bb.reshape(ns, S5_SLAB_GROUPS, S5_STATE, S5_GROUP))
        return t.reshape(ns, LANES, S5_SW)

    def out_mat(cc):
        t = jnp.einsum('ab,saop->sapbo', eye, cc.reshape(ns, S5_SLAB_GROUPS, S5_GROUP, S5_STATE))
        return t.reshape(ns, S5_SW, LANES)

    w_in = jnp.concatenate([in_mat(bbar_r), in_mat(bbar_i)], axis=2)
    w_out = jnp.concatenate([out_mat(c_re), -out_mat(c_im)], axis=1)
    n = jnp.arange(1, n_pow + 1, dtype=jnp.float32)[None, :, None]
    lrs = (lr * step).reshape(ns, 1, S5_SW)
    lis = (li * step).reshape(ns, 1, S5_SW)
    pm = jnp.exp(n * lrs)
    apow = jnp.concatenate([pm * jnp.cos(n * lis), pm * jnp.sin(n * lis)], axis=2)
    return _bf(w_in), _bf(w_out), apow


def _s5_mix(z, w_in, w_out, apow, d_skip):
    L = z.shape[0]
    tb = S5_TB
    assert L % tb == 0
    ns = S5_DIM // LANES
    wide = S5_NSUB * LANES
    f32 = jnp.float32
    return pl.pallas_call(
        _s5_kernel,
        out_shape=(jax.ShapeDtypeStruct((L, S5_DIM), f32), jax.ShapeDtypeStruct((L, S5_DIM), MXU_DTYPE)),
        grid=(ns // S5_NSUB, L // tb),
        in_specs=[pl.BlockSpec((tb, LANES), functools.partial(lambda s, i, n: (i, S5_NSUB * s + n), n=n))
                  for n in range(S5_NSUB)] + [
            pl.BlockSpec((S5_NSUB, LANES, 2 * S5_SW), lambda s, i: (s, 0, 0)),
            pl.BlockSpec((S5_NSUB, 2 * S5_SW, LANES), lambda s, i: (s, 0, 0)),
            pl.BlockSpec((S5_NSUB, tb // S5_SEG, 2 * S5_SW), lambda s, i: (s, 0, 0)),
            pl.BlockSpec((1, wide), lambda s, i: (0, s)),
        ],
        out_specs=(pl.BlockSpec((tb, wide), lambda s, i: (i, s)), pl.BlockSpec((tb, wide), lambda s, i: (i, s))),
        scratch_shapes=[
            pltpu.VMEM((S5_NSUB, 1, 2 * S5_SW), f32),
            pltpu.VMEM((S5_NSUB, tb, LANES), f32),
            pltpu.VMEM((S5_NSUB, tb, 2 * S5_SW), f32),
            pltpu.VMEM((S5_NSUB, tb, 2 * S5_SW), f32),
            pltpu.VMEM((S5_NSUB, tb, LANES), f32),
        ],
        compiler_params=_cparams(("parallel", "arbitrary")),
        name="s5_mix",
    )(*([z] * S5_NSUB), w_in, w_out, apow, d_skip.reshape(1, S5_DIM))


RET_CHUNK = 512


def _ret_kernel(q_ref, k_ref, v_ref, g_ref, cos_ref, sin_ref, intra_ref, rowdec_ref, o_ref, s_s):
    c = pl.program_id(1)

    @pl.when(c == 0)
    def _():
        s_s[...] = jnp.zeros_like(s_s)

    half = RET_HEAD_DIM // 2
    cos, sin = cos_ref[...], sin_ref[...]

    def rot(x):
        x1, x2 = x[:, 0:half], x[:, half:2 * half]
        return jnp.concatenate([x1 * cos - x2 * sin, x1 * sin + x2 * cos], axis=1)

    q = rot(q_ref[...])
    k = rot(k_ref[...]) * (RET_HEAD_DIM ** -0.5)
    vb = _bf(v_ref[...])
    qb = _bf(q)
    dec = rowdec_ref[0]

    def lanes2(x):
        return jnp.concatenate([x, x], axis=1)

    xi, zeta, g_chunk = lanes2(dec[:, 0:LANES]), lanes2(dec[:, LANES:2 * LANES]), lanes2(dec[:, 2 * LANES:3 * LANES])
    att = _dot_nt(qb, _bf(k)) * intra_ref[0]
    s_old = s_s[...]
    o = _dot(_bf(att), vb) + _dot(qb, _bf(s_old)) * xi
    s_s[...] = s_old * g_chunk[0:1, :] + _dot(_bf((k * zeta).T), vb)
    mean = jnp.mean(o, axis=-1, keepdims=True)
    d = o - mean
    var = jnp.mean(d * d, axis=-1, keepdims=True)
    gate = g_ref[...]
    o_ref[...] = (gate * _sigmoid(gate) * (d * lax.rsqrt(var + RET_GN_EPS))).astype(o_ref.dtype)


def _ret_tables(L):
    C = RET_CHUNK
    inv = 1.0 / (RET_ROPE_BASE ** jnp.linspace(0.0, 1.0, RET_HEAD_DIM // 2, dtype=jnp.float32))
    ang = jnp.arange(L, dtype=jnp.float32)[:, None] * inv[None, :]
    log_g = jnp.log(1.0 - 2.0 ** (-5.0 - jnp.arange(RET_HEADS, dtype=jnp.float32)))
    pos = jnp.arange(C, dtype=jnp.float32)
    diff = pos[:, None] - pos[None, :]
    intra = jnp.where(diff >= 0, jnp.exp(jnp.maximum(diff, 0.0)[None] * log_g[:, None, None]), 0.0)
    xi = jnp.exp((pos + 1.0)[None, :] * log_g[:, None])
    zeta = jnp.exp((C - 1.0 - pos)[None, :] * log_g[:, None])
    g_chunk = jnp.broadcast_to(jnp.exp(C * log_g)[:, None], (RET_HEADS, C))
    rowdec = jnp.concatenate([jnp.broadcast_to(t[:, :, None], (RET_HEADS, C, LANES)) for t in (xi, zeta, g_chunk)], axis=2)
    return jnp.cos(ang), jnp.sin(ang), intra, rowdec


def _retention_mix(z, cos, sin, intra, rowdec):
    L = z.shape[0]
    C, hd = RET_CHUNK, RET_HEAD_DIM
    assert L % C == 0
    base = S5_DIM // hd

    def blk(j):
        return pl.BlockSpec((C, hd), lambda h, c: (c, base + j * RET_HEADS + h))

    return pl.pallas_call(
        _ret_kernel,
        out_shape=jax.ShapeDtypeStruct((L, RET_HEADS * hd), MXU_DTYPE),
        grid=(RET_HEADS, L // C),
        in_specs=[
            blk(0), blk(1), blk(2), blk(3),
            pl.BlockSpec((C, hd // 2), lambda h, c: (c, 0)),
            pl.BlockSpec((C, hd // 2), lambda h, c: (c, 0)),
            pl.BlockSpec((1, C, C), lambda h, c: (h, 0, 0)),
            pl.BlockSpec((1, C, 3 * LANES), lambda h, c: (h, 0, 0)),
        ],
        out_specs=pl.BlockSpec((C, hd), lambda h, c: (c, h)),
        scratch_shapes=[pltpu.VMEM((hd, hd), jnp.float32)],
        compiler_params=_cparams(("parallel", "arbitrary")),
        name="retention_mix",
    )(z, z, z, z, cos, sin, intra, rowdec)


def _lora_w_in(w):
    d = w.shape[0]
    o_ad = ZE_LORA + DECAY_RANK
    o_gd = o_ad + AAA_RANK
    parts = [
        w[:, ZE_LORA:o_ad], jnp.zeros((d, ZS_AD - ZS_WD - DECAY_RANK), w.dtype),
        w[:, o_ad:o_gd], jnp.zeros((d, ZS_GD - ZS_AD - AAA_RANK), w.dtype),
        w[:, o_gd:o_gd + GATE_RANK],
    ]
    out = jnp.concatenate(parts, axis=1)
    assert out.shape[1] == ZS_COLS
    return out


def _ffn(h, norm_g, w_gate, w_up, w_down_b, layer):
    hn = _rmsnorm(h, norm_g, MXU_DTYPE)
    mid = _matmul([hn], [(w_gate, layer, 0), (w_up, layer, 0)], [(0, 0), (0, 1)], [], _ep_swiglu, MXU_DTYPE,
                  "ffn_gate_up")
    return _matmul([mid], [(w_down_b, layer, 0)], [(0, 0)], [("tile", h)], _ep_resid, jnp.float32,
                   "ffn_down", tk=w_down_b.shape[1] // FFN_DOWN_K_SPLITS)


def _out_proj(h, o_first, o_second, w_out, layer):
    return _matmul([o_first, o_second], [(w_out, layer, 0), (w_out, layer, 1)], [(0, 0), (1, 1)], [("tile", h)],
                   _ep_resid2, jnp.float32, "out_proj")


def kernel(x, norm_mix, norm_ffn, ffn_gate, ffn_up, ffn_down, e_w_in, e_w_out, e_mu, e_w0, e_w_up, e_a0, e_a_up, e_g_up, e_k_k, e_k_a, e_r_k, e_ln_w, e_ln_b, o_w_in, o_w_out, o_lam_re, o_lam_im, o_log_step, o_b_re, o_b_im, o_c_re, o_c_im, o_d_skip, o_w_glu, o_b_glu, final_norm):
    assert x.shape[0] == 1
    h = x.reshape(x.shape[1], x.shape[2])
    L = h.shape[0]
    n_top = min(TOPK_MAX, L // 4)

    ffn_down_b = _bf(ffn_down)
    hn = _rmsnorm(h, norm_mix[0], MXU_DTYPE)
    z = _matmul([hn], [(e_w_in, 0, 0)], [(0, 0)], [], _ep_plain, jnp.float32, "even_in_proj", n_out=ZE_MAIN_COLS)
    z_lora = _matmul([hn], [(_lora_w_in(e_w_in[0]), 0, 0)], [(0, 0)], [], _ep_plain, jnp.float32, "even_lora_proj")
    tab_a = _rope_tables(L, A_ROT_HALF, A_HEAD_DIM, 2 * A_ROT_HALF)
    tab_i = _rope_tables(L, IDX_ROT_HALF, IDX_DIM, 2 * IDX_ROT_HALF)
    q_t, k_r, v_t, qi_t, ki_r, w_t = _dsa_prep(z, tab_a, tab_i)
    o_a = _dsa_attention(q_t, k_r, v_t, qi_t, ki_r, w_t, n_top)
    p_rows, mu_small, lora = _rwkv_params(e_mu[0], e_w0[0], e_w_up[0], e_a0[0], e_a_up[0], e_g_up[0], e_k_k[0],
                                          e_k_a[0], e_r_k[0], e_ln_w[0], e_ln_b[0])
    o_b = _rwkv_mix(z, z_lora, p_rows, mu_small, lora)
    h = _out_proj(h, o_a, o_b, e_w_out, 0)
    h = _ffn(h, norm_ffn[0], ffn_gate, ffn_up, ffn_down_b, 0)

    hn = _rmsnorm(h, norm_mix[1], MXU_DTYPE)
    z = _matmul([hn], [(o_w_in, 0, 0)], [(0, 0)], [], _ep_plain, jnp.float32, "odd_in_proj")
    s5_in, s5_out, s5_apow = _s5_params(o_lam_re[0], o_lam_im[0], o_log_step[0], o_b_re[0], o_b_im[0], o_c_re[0],
                                        o_c_im[0], S5_TB // S5_SEG)
    zg, zg_b = _s5_mix(z, s5_in, s5_out, s5_apow, o_d_skip[0])
    o_c = _matmul([zg_b], [(o_w_glu, 0, 0)], [(0, 0)], [("tile", zg), ("row", o_b_glu[0].reshape(1, S5_DIM))], _ep_glu,
                  MXU_DTYPE, "s5_glu")
    o_d = _retention_mix(z, *_ret_tables(L))
    h = _out_proj(h, o_c, o_d, o_w_out, 0)
    h = _ffn(h, norm_ffn[1], ffn_gate, ffn_up, ffn_down_b, 1)

    return _rmsnorm(h, final_norm, jnp.float32).reshape(x.shape)
```
